```python
import math
import jax, jax.numpy as jnp
from jax import lax
import numpy as np

D_MODEL = 2048
BATCH = 1
SEQ = 8192
DEPTH = 1

HEAD_DIM = 128
HEADS_PER_GROUP = 4
DILATED_GROUPS = ((128, 1), (512, 4), (2048, 16))
N_ATTN_GROUPS = len(DILATED_GROUPS)
N_ATTN_HEADS = N_ATTN_GROUPS * HEADS_PER_GROUP
ATTN_WIDTH = N_ATTN_HEADS * HEAD_DIM
ATTN_OUT_WIDTH = HEADS_PER_GROUP * HEAD_DIM
ATTN_BLOCK = 128
REL_BUCKETS = 32
REL_MAX_DISTANCE = 2048
SSM_GROUP_CH = 16
SSM_STATE = 64
SSM_WIDTH = 1024
SSM_GROUPS = SSM_WIDTH // SSM_GROUP_CH
DT_MIN = 1e-3
DT_MAX = 1e-1
A_RE_MAX = -1e-4
IN_WIDTH = 3 * ATTN_WIDTH + SSM_WIDTH + 2 * D_MODEL
IN_SPLITS = (ATTN_WIDTH, 2 * ATTN_WIDTH, 3 * ATTN_WIDTH, 3 * ATTN_WIDTH + SSM_WIDTH,
             3 * ATTN_WIDTH + SSM_WIDTH + D_MODEL)
N_EXPERTS = 256
TOP_K = 8
N_EXPERT_GROUPS = 8
TOPK_GROUPS = 4
EXPERT_FF = 512
SHARED_FF = 512
ROUTED_SCALE = 2.5
MOE_BLOCK = 128
DEEPNORM_ALPHA = (2 * DEPTH) ** 0.25
DEEPNORM_BETA = (8 * DEPTH) ** -0.25
LN_EPS = 1e-5

kernel_name = "hybrid_dilattn_s5_moe_deepnorm_adaln"


def _layer_norm(x, g, b):
    xf = x.astype(jnp.float32)
    mu = xf.mean(-1, keepdims=True)
    var = jnp.square(xf - mu).mean(-1, keepdims=True)
    y = (xf - mu) * lax.rsqrt(var + LN_EPS)
    return (y * g.astype(jnp.float32) + b.astype(jnp.float32)).astype(x.dtype)


def _t5_bucket(dist):
    max_exact = REL_BUCKETS // 2
    d_f = jnp.maximum(dist, 1).astype(jnp.float32)
    large = max_exact + (jnp.log(d_f / max_exact) / math.log(REL_MAX_DISTANCE / max_exact)
                         * (REL_BUCKETS - max_exact)).astype(jnp.int32)
    large = jnp.minimum(large, REL_BUCKETS - 1)
    return jnp.where(dist < max_exact, dist, large)


def _banded_window_attention(q, k, v, bias_by_dist):
    n, s, h, dh = q.shape
    w = bias_by_dist.shape[-1] - 1
    nb = -(-s // ATTN_BLOCK)
    pad = nb * ATTN_BLOCK - s
    padcfg = ((0, 0), (0, pad), (0, 0), (0, 0))
    q, k, v = (jnp.pad(t, padcfg) for t in (q, k, v))

    def blocks(t):
        return t.reshape(n, nb, ATTN_BLOCK, h, dh)

    def with_prev(t):
        tb = blocks(t)
        prev = jnp.pad(tb[:, :-1], ((0, 0), (1, 0), (0, 0), (0, 0), (0, 0)))
        return jnp.concatenate([prev, tb], axis=2)

    qb, kc, vc = blocks(q), with_prev(k), with_prev(v)
    scores = jnp.einsum('nbqhd,nbkhd->nbhqk', qb, kc).astype(jnp.float32) * (dh ** -0.5)
    rel = (jnp.arange(ATTN_BLOCK)[:, None] + ATTN_BLOCK) - jnp.arange(2 * ATTN_BLOCK)[None, :]
    in_window = (rel >= 0) & (rel <= w)
    key_pos = (jnp.arange(nb)[:, None] * ATTN_BLOCK - ATTN_BLOCK
               + jnp.arange(2 * ATTN_BLOCK)[None, :])
    mask = in_window[None] & (key_pos >= 0)[:, None, :]
    bias = bias_by_dist.astype(jnp.float32)[:, jnp.clip(rel, 0, w)]
    scores = jnp.where(mask[None, :, None], scores + bias[None, None], -jnp.inf)
    lse = jax.nn.logsumexp(scores, axis=-1)
    p = jnp.exp(scores - lse[..., None])
    out = jnp.einsum('nbhqk,nbkhd->nbqhd', p.astype(v.dtype), vc)
    out = out.reshape(n, nb * ATTN_BLOCK, h, dh)[:, :s]
    lse = lse.transpose(0, 1, 3, 2).reshape(n, nb * ATTN_BLOCK, h)[:, :s]
    return out, lse


def _dilated_attention_group(q, k, v, dilation, bias_by_dist):
    b, l, h, dh = q.shape
    m = -(-l // dilation)
    pad = m * dilation - l

    def to_sub(t):
        t = jnp.pad(t, ((0, 0), (0, pad), (0, 0), (0, 0)))
        return t.reshape(b, m, dilation, h, dh).transpose(0, 2, 1, 3, 4).reshape(b * dilation, m, h, dh)

    out, lse = _banded_window_attention(to_sub(q), to_sub(k), to_sub(v), bias_by_dist)
    out = out.reshape(b, dilation, m, h, dh).transpose(0, 2, 1, 3, 4).reshape(b, m * dilation, h, dh)[:, :l]
    lse = lse.reshape(b, dilation, m, h).transpose(0, 2, 1, 3).reshape(b, m * dilation, h)[:, :l]
    return out, lse


def _dilated_attention(q, k, v, rel_bias):
    b, l = q.shape[:2]
    outs, lses = [], []
    for gi, (window, dilation) in enumerate(DILATED_GROUPS):
        hs = slice(gi * HEADS_PER_GROUP, (gi + 1) * HEADS_PER_GROUP)
        n_keys = window // dilation
        bucket = _t5_bucket(jnp.arange(n_keys + 1, dtype=jnp.int32) * dilation)
        bias = rel_bias[bucket][:, hs].T
        o, s = _dilated_attention_group(q[:, :, hs], k[:, :, hs], v[:, :, hs], dilation, bias)
        outs.append(o)
        lses.append(s)
    wts = jax.nn.softmax(jnp.stack(lses), axis=0)
    out = jnp.einsum('gblh,gblhd->blhd', wts.astype(q.dtype), jnp.stack(outs))
    return out.reshape(b, l, ATTN_OUT_WIDTH)


def _s5_ssm(u, a_re, a_im, log_dt, b_re, b_im, c_re, c_im, d_skip):
    f32 = jnp.float32
    bsz, l, _ = u.shape
    uf = u.astype(f32).reshape(bsz, l, SSM_GROUPS, SSM_GROUP_CH)
    lam = lax.complex(jnp.minimum(a_re.astype(f32), A_RE_MAX), a_im.astype(f32))
    dt = jnp.exp(log_dt.astype(f32))[:, None]
    lam_bar = jnp.exp(lam * dt)
    b_bar = ((lam_bar - 1.0) / lam)[:, :, None] * lax.complex(b_re.astype(f32), b_im.astype(f32))
    bu = lax.complex(jnp.einsum('blgh,gph->blgp', uf, b_bar.real),
                     jnp.einsum('blgh,gph->blgp', uf, b_bar.imag))

    def combine(left, right):
        a_l, s_l = left
        a_r, s_r = right
        return a_r * a_l, a_r * s_l + s_r

    _, states = lax.associative_scan(combine, (jnp.broadcast_to(lam_bar, bu.shape), bu), axis=1)
    y = (jnp.einsum('blgp,ghp->blgh', states.real, c_re.astype(f32))
         - jnp.einsum('blgp,ghp->blgh', states.imag, c_im.astype(f32)))
    y = y + d_skip.astype(f32) * uf
    return y.reshape(bsz, l, SSM_WIDTH).astype(u.dtype)


def _moe(u, w_router, router_bias, e_w_in, e_w_out, s_w_in, s_w_out):
    bsz, l, d = u.shape
    t = bsz * l
    uf = u.reshape(t, d)
    scores = jax.nn.sigmoid((uf @ w_router).astype(jnp.float32))
    sel = scores + router_bias.astype(jnp.float32)
    grouped = sel.reshape(t, N_EXPERT_GROUPS, N_EXPERTS // N_EXPERT_GROUPS)
    group_score = lax.top_k(grouped, 2)[0].sum(-1)
    _, top_groups = lax.top_k(group_score, TOPK_GROUPS)
    group_ok = (top_groups[..., None] == jnp.arange(N_EXPERT_GROUPS)).any(axis=1)
    expert_ok = jnp.repeat(group_ok, N_EXPERTS // N_EXPERT_GROUPS, axis=1)
    _, top_idx = lax.top_k(jnp.where(expert_ok, sel, -jnp.inf), TOP_K)
    w = jnp.take_along_axis(scores, top_idx, axis=1)
    w = w / w.sum(-1, keepdims=True) * ROUTED_SCALE
    e_flat = top_idx.reshape(-1)
    tok_flat = jnp.repeat(jnp.arange(t, dtype=jnp.int32), TOP_K)
    g_flat = w.reshape(-1)
    order = jnp.argsort(e_flat)
    e_s, t_s, g_s = e_flat[order], tok_flat[order], g_flat[order]
    counts = jnp.bincount(e_flat, length=N_EXPERTS)
    starts = jnp.cumsum(counts) - counts
    padded = (counts + MOE_BLOCK - 1) // MOE_BLOCK * MOE_BLOCK
    pad_ends = jnp.cumsum(padded)
    pad_starts = pad_ends - padded
    dest = pad_starts[e_s] + jnp.arange(t * TOP_K, dtype=jnp.int32) - starts[e_s]
    n_rows = -(-(t * TOP_K + N_EXPERTS * (MOE_BLOCK - 1)) // MOE_BLOCK) * MOE_BLOCK
    n_blocks = n_rows // MOE_BLOCK
    row_tok = jnp.zeros((n_rows,), jnp.int32).at[dest].set(t_s)
    row_gate = jnp.zeros((n_rows,), jnp.float32).at[dest].set(g_s)
    block_expert = jnp.minimum(
        jnp.searchsorted(pad_ends, jnp.arange(n_blocks) * MOE_BLOCK, side='right'), N_EXPERTS - 1)

    def expert_block(args):
        tok, e = args
        h = uf[tok] @ e_w_in[e]
        return (jax.nn.silu(h[:, :EXPERT_FF]) * h[:, EXPERT_FF:]) @ e_w_out[e]

    y = lax.map(expert_block, (row_tok.reshape(n_blocks, MOE_BLOCK), block_expert))
    y = y.reshape(n_rows, d) * row_gate[:, None].astype(u.dtype)
    routed = jax.ops.segment_sum(y, row_tok, num_segments=t)
    hs = uf @ s_w_in
    shared = (jax.nn.silu(hs[:, :SHARED_FF]) * hs[:, SHARED_FF:]) @ s_w_out
    return (shared + routed).reshape(bsz, l, d)


def setup_inputs(seed: int = 0) -> dict:
    key = jax.random.key(seed)
    ks = jax.random.split(key, 32)
    f32 = jnp.float32
    D, L, B = D_MODEL, SEQ, BATCH
    G, P, HG = SSM_GROUPS, SSM_STATE, SSM_GROUP_CH

    def nrm(k, shape, scale):
        return jax.random.normal(k, shape, f32) * scale

    return {
        'x': nrm(ks[0], (B, L, D), 1.0),
        'c': nrm(ks[1], (B, D), 1.0),
        'w_ada': nrm(ks[2], (DEPTH, D, 6 * D), D ** -0.5),
        'b_ada': nrm(ks[3], (DEPTH, 6 * D), 0.02),
        'w_in': nrm(ks[4], (DEPTH, D, IN_WIDTH), D ** -0.5),
        'rel_bias': nrm(ks[5], (REL_BUCKETS, N_ATTN_HEADS), 0.5),
        'ssm_a_re': -0.5 + nrm(ks[6], (DEPTH, G, P), 0.01),
        'ssm_a_im': jnp.pi * jnp.arange(P, dtype=f32) + nrm(ks[7], (DEPTH, G, P), 0.01),
        'ssm_log_dt': jax.random.uniform(ks[8], (DEPTH, G), f32, math.log(DT_MIN), math.log(DT_MAX)),
        'ssm_b_re': nrm(ks[9], (DEPTH, G, P, HG), (2 * HG) ** -0.5),
        'ssm_b_im': nrm(ks[10], (DEPTH, G, P, HG), (2 * HG) ** -0.5),
        'ssm_c_re': nrm(ks[11], (DEPTH, G, HG, P), P ** -0.5),
        'ssm_c_im': nrm(ks[12], (DEPTH, G, HG, P), P ** -0.5),
        'ssm_d': nrm(ks[13], (DEPTH, G, HG), 1.0),
        'w_attn_out': nrm(ks[14], (DEPTH, ATTN_OUT_WIDTH, D), ATTN_OUT_WIDTH ** -0.5),
        'w_ssm_glu': nrm(ks[15], (DEPTH, SSM_WIDTH, 2 * D), SSM_WIDTH ** -0.5),
        'w_o': nrm(ks[16], (DEPTH, D, D), D ** -0.5 * DEEPNORM_BETA),
        'ln1_g': 1.0 + nrm(ks[17], (DEPTH, D), 0.02),
        'ln1_b': nrm(ks[18], (DEPTH, D), 0.02),
        'w_router': nrm(ks[19], (DEPTH, D, N_EXPERTS), D ** -0.5),
        'router_bias': nrm(ks[20], (DEPTH, N_EXPERTS), 0.01),
        'e_w_in': nrm(ks[21], (DEPTH, N_EXPERTS, D, 2 * EXPERT_FF), D ** -0.5),
        'e_w_out': nrm(ks[22], (DEPTH, N_EXPERTS, EXPERT_FF, D), EXPERT_FF ** -0.5 * DEEPNORM_BETA),
        's_w_in': nrm(ks[23], (DEPTH, D, 2 * SHARED_FF), D ** -0.5),
        's_w_out': nrm(ks[24], (DEPTH, SHARED_FF, D), SHARED_FF ** -0.5 * DEEPNORM_BETA),
        'ln2_g': 1.0 + nrm(ks[25], (DEPTH, D), 0.02),
        'ln2_b': nrm(ks[26], (DEPTH, D), 0.02),
    }


def reference(x, c, w_ada, b_ada, w_in, rel_bias, ssm_a_re, ssm_a_im, ssm_log_dt, ssm_b_re, ssm_b_im,
              ssm_c_re, ssm_c_im, ssm_d, w_attn_out, w_ssm_glu, w_o, ln1_g, ln1_b, w_router, router_bias,
              e_w_in, e_w_out, s_w_in, s_w_out, ln2_g, ln2_b):
    bsz, l, d = x.shape
    cond = jax.nn.silu(c)
    for i in range(DEPTH):
        mod = cond @ w_ada[i] + b_ada[i]
        shift1, scale1, gate1, shift2, scale2, gate2 = [m[:, None, :] for m in jnp.split(mod, 6, axis=-1)]
        u = x * (1 + scale1) + shift1
        z = u @ w_in[i]
        q, k, v, s_in, g_attn, g_ssm = jnp.split(z, IN_SPLITS, axis=-1)
        heads = (bsz, l, N_ATTN_HEADS, HEAD_DIM)
        attn = _dilated_attention(q.reshape(heads), k.reshape(heads), v.reshape(heads), rel_bias)
        a_branch = attn @ w_attn_out[i]
        y_ssm = _s5_ssm(s_in, ssm_a_re[i], ssm_a_im[i], ssm_log_dt[i], ssm_b_re[i], ssm_b_im[i],
                        ssm_c_re[i], ssm_c_im[i], ssm_d[i])
        glu = y_ssm @ w_ssm_glu[i]
        s_branch = glu[..., :d] * jax.nn.sigmoid(glu[..., d:])
        merged = jax.nn.sigmoid(g_attn) * a_branch + jax.nn.sigmoid(g_ssm) * s_branch
        x = _layer_norm(DEEPNORM_ALPHA * x + gate1 * (merged @ w_o[i]), ln1_g[i], ln1_b[i])
        u2 = x * (1 + scale2) + shift2
        ffn = _moe(u2, w_router[i], router_bias[i], e_w_in[i], e_w_out[i], s_w_in[i], s_w_out[i])
        x = _layer_norm(DEEPNORM_ALPHA * x + gate2 * ffn, ln2_g[i], ln2_b[i])
    return x
```

```python
import functools
import math

import jax
import jax.numpy as jnp
from jax import lax
from jax.experimental import pallas as pl
from jax.experimental.pallas import tpu as pltpu

F32 = jnp.float32
BF16 = jnp.bfloat16

D_MODEL = 2048
HEAD_DIM = 128
HEADS_PER_GROUP = 4
DILATED_GROUPS = ((128, 1), (512, 4), (2048, 16))
N_ATTN_GROUPS = len(DILATED_GROUPS)
N_ATTN_HEADS = N_ATTN_GROUPS * HEADS_PER_GROUP
ATTN_WIDTH = N_ATTN_HEADS * HEAD_DIM
ATTN_OUT_WIDTH = HEADS_PER_GROUP * HEAD_DIM
ATTN_BLOCK = 128
REL_BUCKETS = 32
REL_MAX_DISTANCE = 2048
SSM_GROUP_CH = 16
SSM_STATE = 64
SSM_WIDTH = 1024
SSM_GROUPS = SSM_WIDTH // SSM_GROUP_CH
A_RE_MAX = -1e-4
QKV_WIDTH = 3 * ATTN_WIDTH
GATE_OFF = QKV_WIDTH + SSM_WIDTH
N_EXPERTS = 256
TOP_K = 8
N_EXPERT_GROUPS = 8
TOPK_GROUPS = 4
EXPERT_FF = 512
SHARED_FF = 512
ROUTED_SCALE = 2.5
MOE_BLOCK = 128
DEPTH = 1
DEEPNORM_ALPHA = (2 * DEPTH) ** 0.25
LN_EPS = 1e-5
NEG_BIG = -1e30

SSM_CHUNK = 128
SSM_BLK_GROUPS = 16
SSM_NBLK = SSM_GROUPS // SSM_BLK_GROUPS
SSM_BLK_CH = SSM_BLK_GROUPS * SSM_GROUP_CH
SSM_BLK_ST = SSM_BLK_GROUPS * SSM_STATE
SSM_STATES = SSM_GROUPS * SSM_STATE

VMEM_LIMIT = 56 * 1024 * 1024


def _params(sem, vmem=VMEM_LIMIT):
    return pltpu.CompilerParams(dimension_semantics=sem, vmem_limit_bytes=vmem)


def _mod_kernel(c_ref, w_ref, b_ref, o_ref):
    c = c_ref[...]
    cond = c * jax.nn.sigmoid(c)
    o_ref[...] = jnp.dot(cond, w_ref[...], preferred_element_type=F32,
                         precision=lax.Precision.HIGHEST) + b_ref[...]


def _modulation(c, w_ada, b_ada):
    d, n = w_ada.shape
    tn = 1024
    c8 = jnp.broadcast_to(c, (8, d))
    out = pl.pallas_call(
        _mod_kernel,
        out_shape=jax.ShapeDtypeStruct((8, n), F32),
        grid=(n // tn,),
        in_specs=[pl.BlockSpec((8, d), lambda j: (0, 0)),
                  pl.BlockSpec((d, tn), lambda j: (0, j)),
                  pl.BlockSpec((1, tn), lambda j: (0, j))],
        out_specs=pl.BlockSpec((8, tn), lambda j: (0, j)),
        compiler_params=_params(("arbitrary",)),
        name="adaln_mod",
    )(c8, w_ada, b_ada.reshape(1, n))
    return out[0:1]


def _modulate_kernel(x_ref, sc_ref, sh_ref, o_ref):
    o_ref[...] = (x_ref[...] * (1.0 + sc_ref[...]) + sh_ref[...]).astype(o_ref.dtype)


def _modulate(x, scale, shift):
    m, d = x.shape
    tm = 512
    return pl.pallas_call(
        _modulate_kernel,
        out_shape=jax.ShapeDtypeStruct((m, d), BF16),
        grid=(m // tm,),
        in_specs=[pl.BlockSpec((tm, d), lambda i: (i, 0)),
                  pl.BlockSpec((1, d), lambda i: (0, 0)),
                  pl.BlockSpec((1, d), lambda i: (0, 0))],
        out_specs=pl.BlockSpec((tm, d), lambda i: (i, 0)),
        compiler_params=_params(("arbitrary",)),
        name="modulate",
    )(x, scale, shift)


def _mm_kernel(a_ref, w_ref, o_ref):
    o_ref[...] = jnp.dot(a_ref[...], w_ref[...], preferred_element_type=F32).astype(o_ref.dtype)


def _matmul(a, w, col_off, n, tm, tn, out_dtype, name):
    m, k = a.shape
    assert col_off % tn == 0 and n % tn == 0 and m % tm == 0
    off_blocks = col_off // tn
    return pl.pallas_call(
        _mm_kernel,
        out_shape=jax.ShapeDtypeStruct((m, n), out_dtype),
        grid=(n // tn, m // tm),
        in_specs=[pl.BlockSpec((tm, k), lambda j, i: (i, 0)),
                  pl.BlockSpec((k, tn), lambda j, i: (0, j + off_blocks))],
        out_specs=pl.BlockSpec((tm, tn), lambda j, i: (i, j)),
        compiler_params=_params(("arbitrary", "arbitrary")),
        name=name,
    )(a, w)


def _attn_kernel(bp_ref, bc_ref, q_ref, kp_ref, kc_ref, vp_ref, vc_ref, o_ref, lse_ref):
    blk = pl.program_id(1)
    scale = HEAD_DIM ** -0.5
    nt = (((1,), (1,)), ((), ()))
    first = blk == 0
    for h in range(HEADS_PER_GROUP):
        hs = slice(h * HEAD_DIM, (h + 1) * HEAD_DIM)
        q = q_ref[:, hs]
        s_p = lax.dot_general(q, kp_ref[:, hs], nt, preferred_element_type=F32) * scale + bp_ref[h]
        s_c = lax.dot_general(q, kc_ref[:, hs], nt, preferred_element_type=F32) * scale + bc_ref[h]
        s_p = jnp.where(first, NEG_BIG, s_p)
        m = jnp.maximum(jnp.max(s_p, axis=-1, keepdims=True), jnp.max(s_c, axis=-1, keepdims=True))
        p_p = jnp.exp(s_p - m)
        p_c = jnp.exp(s_c - m)
        l = jnp.sum(p_p, axis=-1, keepdims=True) + jnp.sum(p_c, axis=-1, keepdims=True)
        o = (jnp.dot(p_p.astype(BF16), vp_ref[:, hs], preferred_element_type=F32)
             + jnp.dot(p_c.astype(BF16), vc_ref[:, hs], preferred_element_type=F32))
        o_ref[:, hs] = o / l
        lse_ref[:, hs] = jnp.broadcast_to(m + jnp.log(l), (ATTN_BLOCK, HEAD_DIM))


def _attention_group(qkv, bias_prev, bias_cur, gi, dilation):
    l, width = qkv.shape
    m = l // dilation
    nb = m // ATTN_BLOCK
    view = qkv.reshape(m, dilation * width)
    cpr = width // ATTN_OUT_WIDTH
    gq, gk, gv = gi, N_ATTN_GROUPS + gi, 2 * N_ATTN_GROUPS + gi

    def cur(cb):
        return pl.BlockSpec((ATTN_BLOCK, ATTN_OUT_WIDTH), lambda r, b: (b, r * cpr + cb))

    def prev(cb):
        return pl.BlockSpec((ATTN_BLOCK, ATTN_OUT_WIDTH),
                            lambda r, b: (jnp.maximum(b - 1, 0), r * cpr + cb))

    bias_spec = pl.BlockSpec((HEADS_PER_GROUP, ATTN_BLOCK, ATTN_BLOCK), lambda r, b: (0, 0, 0))
    out_spec = pl.BlockSpec((ATTN_BLOCK, ATTN_OUT_WIDTH), lambda r, b: (b, r))
    out, lse = pl.pallas_call(
        _attn_kernel,
        out_shape=(jax.ShapeDtypeStruct((m, dilation * ATTN_OUT_WIDTH), F32),
                   jax.ShapeDtypeStruct((m, dilation * ATTN_OUT_WIDTH), F32)),
        grid=(dilation, nb),
        in_specs=[bias_spec, bias_spec, cur(gq), prev(gk), cur(gk), prev(gv), cur(gv)],
        out_specs=(out_spec, out_spec),
        compiler_params=_params(("arbitrary", "arbitrary")),
        name=f"dilated_attn_g{gi}",
    )(bias_prev, bias_cur, view, view, view, view, view)
    return out.reshape(l, ATTN_OUT_WIDTH), lse.reshape(l, ATTN_OUT_WIDTH)


def _t5_bucket(dist):
    max_exact = REL_BUCKETS // 2
    d_f = jnp.maximum(dist, 1).astype(F32)
    large = max_exact + (jnp.log(d_f / max_exact) / math.log(REL_MAX_DISTANCE / max_exact)
                         * (REL_BUCKETS - max_exact)).astype(jnp.int32)
    large = jnp.minimum(large, REL_BUCKETS - 1)
    return jnp.where(dist < max_exact, dist, large)


def _attn_bias_tables(rel_bias, gi, window, dilation):
    n_keys = window // dilation
    bucket = _t5_bucket(jnp.arange(n_keys + 1, dtype=jnp.int32) * dilation)
    hs = slice(gi * HEADS_PER_GROUP, (gi + 1) * HEADS_PER_GROUP)
    by_dist = rel_bias[bucket][:, hs].T.astype(F32)
    qi = jnp.arange(ATTN_BLOCK)[:, None]
    kj = jnp.arange(ATTN_BLOCK)[None, :]
    d_prev = qi + ATTN_BLOCK - kj
    d_cur = qi - kj
    tabs = []
    for dist in (d_prev, d_cur):
        ok = (dist >= 0) & (dist <= n_keys)
        tabs.append(jnp.where(ok[None], by_dist[:, jnp.clip(dist, 0, n_keys)], NEG_BIG))
    return tabs


def _attn_combine_kernel(o1, o2, o3, l1, l2, l3, out_ref):
    a1, a2, a3 = l1[...], l2[...], l3[...]
    m = jnp.maximum(jnp.maximum(a1, a2), a3)
    e1, e2, e3 = jnp.exp(a1 - m), jnp.exp(a2 - m), jnp.exp(a3 - m)
    num = e1 * o1[...] + e2 * o2[...] + e3 * o3[...]
    out_ref[...] = (num / (e1 + e2 + e3)).astype(out_ref.dtype)


def _attn_combine(outs, lses):
    l, w = outs[0].shape
    tm = 1024
    spec = pl.BlockSpec((tm, w), lambda i: (i, 0))
    return pl.pallas_call(
        _attn_combine_kernel,
        out_shape=jax.ShapeDtypeStruct((l, w), BF16),
        grid=(l // tm,),
        in_specs=[spec] * 6,
        out_specs=spec,
        compiler_params=_params(("arbitrary",)),
        name="attn_combine",
    )(*outs, *lses)


def _ssm_kernel(u_ref, bre_ref, bim_ref, cre_ref, cim_ref, enr_ref, eni_ref, epr_ref, epi_ref,
                lamr_ref, lami_ref, d_ref, tri_ref, y_ref, car_ref, cai_ref):
    @pl.when(pl.program_id(0) == 0)
    def _():
        car_ref[...] = jnp.zeros_like(car_ref)
        cai_ref[...] = jnp.zeros_like(cai_ref)

    tc = u_ref.shape[0]
    tri = tri_ref[...]
    for j in range(SSM_NBLK):
        ch = slice(j * SSM_BLK_CH, (j + 1) * SSM_BLK_CH)
        st = slice(j * SSM_BLK_ST, (j + 1) * SSM_BLK_ST)
        u = u_ref[:, ch]
        ub = u.astype(BF16)
        bur = jnp.dot(ub, bre_ref[j], preferred_element_type=F32)
        bui = jnp.dot(ub, bim_ref[j], preferred_element_type=F32)
        enr, eni = enr_ref[:, st], eni_ref[:, st]
        xr = bur * enr - bui * eni
        xi = bur * eni + bui * enr
        pr = jnp.dot(tri, xr.astype(BF16), preferred_element_type=F32)
        pi = jnp.dot(tri, xi.astype(BF16), preferred_element_type=F32)
        cr, ci = car_ref[:, st], cai_ref[:, st]
        lr, li = lamr_ref[:, st], lami_ref[:, st]
        tr = pr + (lr * cr - li * ci)
        ti = pi + (lr * ci + li * cr)
        epr, epi = epr_ref[:, st], epi_ref[:, st]
        sr = epr * tr - epi * ti
        si = epr * ti + epi * tr
        car_ref[:, st] = sr[tc - 1:tc]
        cai_ref[:, st] = si[tc - 1:tc]
        y = (jnp.dot(sr.astype(BF16), cre_ref[j], preferred_element_type=F32)
             - jnp.dot(si.astype(BF16), cim_ref[j], preferred_element_type=F32))
        y_ref[:, ch] = (y + d_ref[:, ch] * u).astype(y_ref.dtype)


def _ssm_tables(a_re, a_im, log_dt, b_re, b_im, c_re, c_im):
    g, p, hc = b_re.shape
    lam_re = jnp.minimum(a_re.astype(F32), A_RE_MAX)
    lam_im = a_im.astype(F32)
    dt = jnp.exp(log_dt.astype(F32))[:, None]
    mag = jnp.exp(lam_re * dt)
    lb_re, lb_im = mag * jnp.cos(lam_im * dt), mag * jnp.sin(lam_im * dt)
    imag = jnp.exp(-lam_re * dt)
    li_re, li_im = imag * jnp.cos(lam_im * dt), -imag * jnp.sin(lam_im * dt)
    den = lam_re * lam_re + lam_im * lam_im
    nr, ni = lb_re - 1.0, lb_im
    f_re = (nr * lam_re + ni * lam_im) / den
    f_im = (ni * lam_re - nr * lam_im) / den
    bb_re = f_re[:, :, None] * b_re - f_im[:, :, None] * b_im
    bb_im = f_re[:, :, None] * b_im + f_im[:, :, None] * b_re

    def powers(pr, pi):
        er, ei = jnp.ones((1, g * p), F32), jnp.zeros((1, g * p), F32)
        pr, pi = pr.reshape(1, g * p), pi.reshape(1, g * p)
        while er.shape[0] < SSM_CHUNK:
            er, ei = (jnp.concatenate([er, er * pr - ei * pi], axis=0),
                      jnp.concatenate([ei, er * pi + ei * pr], axis=0))
            pr, pi = pr * pr - pi * pi, 2.0 * pr * pi
        return er, ei

    epr, epi = powers(lb_re, lb_im)
    enr, eni = powers(li_re, li_im)
    eye = jnp.eye(SSM_BLK_GROUPS, dtype=F32)

    def b_blocks(b):
        b = b.reshape(SSM_NBLK, SSM_BLK_GROUPS, p, hc)
        return jnp.einsum('jgph,gk->jghkp', b, eye).reshape(SSM_NBLK, SSM_BLK_CH, SSM_BLK_ST).astype(BF16)

    def c_blocks(c):
        c = c.astype(F32).reshape(SSM_NBLK, SSM_BLK_GROUPS, hc, p)
        return jnp.einsum('jghp,gk->jkpgh', c, eye).reshape(SSM_NBLK, SSM_BLK_ST, SSM_BLK_CH).astype(BF16)

    return dict(bre=b_blocks(bb_re), bim=b_blocks(bb_im), cre=c_blocks(c_re), cim=c_blocks(c_im),
                enr=enr, eni=eni, epr=epr, epi=epi,
                lamr=lb_re.reshape(1, g * p), lami=lb_im.reshape(1, g * p))


def _ssm(s_in, tabs, d_skip):
    l, w = s_in.shape
    tc = SSM_CHUNK
    tri = jnp.tril(jnp.ones((tc, tc), F32)).astype(BF16)
    full = lambda shape: pl.BlockSpec(shape, lambda c: (0,) * len(shape))
    return pl.pallas_call(
        _ssm_kernel,
        out_shape=jax.ShapeDtypeStruct((l, w), BF16),
        grid=(l // tc,),
        in_specs=[pl.BlockSpec((tc, w), lambda c: (c, 0)),
                  full((SSM_NBLK, SSM_BLK_CH, SSM_BLK_ST)), full((SSM_NBLK, SSM_BLK_CH, SSM_BLK_ST)),
                  full((SSM_NBLK, SSM_BLK_ST, SSM_BLK_CH)), full((SSM_NBLK, SSM_BLK_ST, SSM_BLK_CH)),
                  full((tc, SSM_STATES)), full((tc, SSM_STATES)),
                  full((tc, SSM_STATES)), full((tc, SSM_STATES)),
                  full((1, SSM_STATES)), full((1, SSM_STATES)),
                  full((1, w)), full((tc, tc))],
        out_specs=pl.BlockSpec((tc, w), lambda c: (c, 0)),
        scratch_shapes=[pltpu.VMEM((1, SSM_STATES), F32), pltpu.VMEM((1, SSM_STATES), F32)],
        compiler_params=_params(("arbitrary",)),
        name="s5_ssm",
    )(s_in, tabs['bre'], tabs['bim'], tabs['cre'], tabs['cim'], tabs['enr'], tabs['eni'],
      tabs['epr'], tabs['epi'], tabs['lamr'], tabs['lami'], d_skip.reshape(1, w).astype(F32), tri)


def _merge_kernel(attn_ref, y_ref, u_ref, wa_ref, wg1_ref, wg2_ref, wia_ref, wis_ref, o_ref):
    a = jnp.dot(attn_ref[...], wa_ref[...], preferred_element_type=F32)
    y = y_ref[...]
    s = (jnp.dot(y, wg1_ref[...], preferred_element_type=F32)
         * jax.nn.sigmoid(jnp.dot(y, wg2_ref[...], preferred_element_type=F32)))
    u = u_ref[...]
    ga = jax.nn.sigmoid(jnp.dot(u, wia_ref[...], preferred_element_type=F32))
    gs = jax.nn.sigmoid(jnp.dot(u, wis_ref[...], preferred_element_type=F32))
    o_ref[...] = (ga * a + gs * s).astype(o_ref.dtype)


def _merge(attn, y_ssm, u, w_attn_out, w_ssm_glu, w_in):
    l, d = u.shape
    tm, tn = 512, 512
    nd = d // tn
    go = GATE_OFF // tn
    return pl.pallas_call(
        _merge_kernel,
        out_shape=jax.ShapeDtypeStruct((l, d), BF16),
        grid=(nd, l // tm),
        in_specs=[pl.BlockSpec((tm, ATTN_OUT_WIDTH), lambda j, i: (i, 0)),
                  pl.BlockSpec((tm, SSM_WIDTH), lambda j, i: (i, 0)),
                  pl.BlockSpec((tm, d), lambda j, i: (i, 0)),
                  pl.BlockSpec((ATTN_OUT_WIDTH, tn), lambda j, i: (0, j)),
                  pl.BlockSpec((SSM_WIDTH, tn), lambda j, i: (0, j)),
                  pl.BlockSpec((SSM_WIDTH, tn), lambda j, i: (0, j + nd)),
                  pl.BlockSpec((d, tn), lambda j, i: (0, j + go)),
                  pl.BlockSpec((d, tn), lambda j, i: (0, j + go + nd))],
        out_specs=pl.BlockSpec((tm, tn), lambda j, i: (i, j)),
        compiler_params=_params(("arbitrary", "arbitrary")),
        name="branch_merge",
    )(attn, y_ssm, u, w_attn_out, w_ssm_glu, w_ssm_glu, w_in, w_in)


def _layer_norm(h, g, b):
    mu = jnp.mean(h, axis=-1, keepdims=True)
    c = h - mu
    var = jnp.mean(c * c, axis=-1, keepdims=True)
    return c * lax.rsqrt(var + LN_EPS) * g + b


def _outproj_kernel(mg_ref, x_ref, wo_ref, wr_ref, g1_ref, lg_ref, lb_ref, sc_ref, sh_ref,
                    x1_ref, u2_ref, sco_ref):
    mix = jnp.dot(mg_ref[...], wo_ref[...], preferred_element_type=F32)
    x1 = _layer_norm(DEEPNORM_ALPHA * x_ref[...] + g1_ref[...] * mix, lg_ref[...], lb_ref[...])
    x1_ref[...] = x1
    u2 = (x1 * (1.0 + sc_ref[...]) + sh_ref[...]).astype(BF16)
    u2_ref[...] = u2
    sco_ref[...] = jax.nn.sigmoid(jnp.dot(u2, wr_ref[...], preferred_element_type=F32))


def _outproj(merged, x, w_o, w_router, gate1, ln_g, ln_b, scale2, shift2):
    l, d = x.shape
    e = w_router.shape[1]
    tm = 512
    row = lambda w: pl.BlockSpec((tm, w), lambda i: (i, 0))
    vec = pl.BlockSpec((1, d), lambda i: (0, 0))
    return pl.pallas_call(
        _outproj_kernel,
        out_shape=(jax.ShapeDtypeStruct((l, d), F32), jax.ShapeDtypeStruct((l, d), BF16),
                   jax.ShapeDtypeStruct((l, e), F32)),
        grid=(l // tm,),
        in_specs=[row(d), row(d), pl.BlockSpec((d, d), lambda i: (0, 0)),
                  pl.BlockSpec((d, e), lambda i: (0, 0)), vec, vec, vec, vec, vec],
        out_specs=(row(d), row(d), row(e)),
        compiler_params=_params(("arbitrary",)),
        name="outproj_ln1_router",
    )(merged, x, w_o, w_router, gate1, ln_g, ln_b, scale2, shift2)


def _expert_kernel(be_ref, bv_ref, x_ref, g_ref, win_ref, wout_ref, y_ref):
    b = pl.program_id(0)

    @pl.when(bv_ref[b] > 0)
    def _():
        h = jnp.dot(x_ref[...], win_ref[0].astype(BF16), preferred_element_type=F32)
        hg, hu = h[:, :EXPERT_FF], h[:, EXPERT_FF:]
        act = (hg * jax.nn.sigmoid(hg) * hu).astype(BF16)
        y = jnp.dot(act, wout_ref[0].astype(BF16), preferred_element_type=F32)
        y_ref[...] = y * g_ref[...]

    @pl.when(bv_ref[b] == 0)
    def _():
        y_ref[...] = jnp.zeros_like(y_ref)


def _experts(block_expert, block_valid, x_rows, row_gate, e_w_in, e_w_out):
    n_rows, d = x_rows.shape
    nb = n_rows // MOE_BLOCK
    ff2 = e_w_in.shape[2]
    grid_spec = pltpu.PrefetchScalarGridSpec(
        num_scalar_prefetch=2,
        grid=(nb,),
        in_specs=[pl.BlockSpec((MOE_BLOCK, d), lambda b, be, bv: (b, 0)),
                  pl.BlockSpec((MOE_BLOCK, 1), lambda b, be, bv: (b, 0)),
                  pl.BlockSpec((1, d, ff2), lambda b, be, bv: (be[b], 0, 0)),
                  pl.BlockSpec((1, ff2 // 2, d), lambda b, be, bv: (be[b], 0, 0))],
        out_specs=pl.BlockSpec((MOE_BLOCK, d), lambda b, be, bv: (b, 0)),
    )
    return pl.pallas_call(
        _expert_kernel,
        out_shape=jax.ShapeDtypeStruct((n_rows, d), F32),
        grid_spec=grid_spec,
        compiler_params=_params(("arbitrary",)),
        name="routed_experts",
    )(block_expert, block_valid, x_rows, row_gate, e_w_in, e_w_out)


def _final_kernel(u2_ref, x1_ref, r_ref, win_ref, wout_ref, g2_ref, lg_ref, lb_ref, o_ref):
    h = jnp.dot(u2_ref[...], win_ref[...], preferred_element_type=F32)
    hg, hu = h[:, :SHARED_FF], h[:, SHARED_FF:]
    act = (hg * jax.nn.sigmoid(hg) * hu).astype(BF16)
    shared = jnp.dot(act, wout_ref[...], preferred_element_type=F32)
    ffn = shared + r_ref[...]
    o_ref[...] = _layer_norm(DEEPNORM_ALPHA * x1_ref[...] + g2_ref[...] * ffn, lg_ref[...], lb_ref[...])


def _final(u2, x1, routed, s_w_in, s_w_out, gate2, ln_g, ln_b):
    l, d = x1.shape
    tm = 512
    row = pl.BlockSpec((tm, d), lambda i: (i, 0))
    vec = pl.BlockSpec((1, d), lambda i: (0, 0))
    return pl.pallas_call(
        _final_kernel,
        out_shape=jax.ShapeDtypeStruct((l, d), F32),
        grid=(l // tm,),
        in_specs=[row, row, row,
                  pl.BlockSpec(s_w_in.shape, lambda i: (0, 0)),
                  pl.BlockSpec(s_w_out.shape, lambda i: (0, 0)), vec, vec, vec],
        out_specs=row,
        compiler_params=_params(("arbitrary",)),
        name="shared_expert_ln2",
    )(u2, x1, routed, s_w_in, s_w_out, gate2, ln_g, ln_b)


def _route(scores, router_bias):
    t = scores.shape[0]
    sel = scores + router_bias.astype(F32)
    grouped = sel.reshape(t, N_EXPERT_GROUPS, N_EXPERTS // N_EXPERT_GROUPS)
    group_score = lax.top_k(grouped, 2)[0].sum(-1)
    _, top_groups = lax.top_k(group_score, TOPK_GROUPS)
    group_ok = (top_groups[..., None] == jnp.arange(N_EXPERT_GROUPS)).any(axis=1)
    expert_ok = jnp.repeat(group_ok, N_EXPERTS // N_EXPERT_GROUPS, axis=1)
    _, top_idx = lax.top_k(jnp.where(expert_ok, sel, -jnp.inf), TOP_K)
    w = jnp.take_along_axis(scores, top_idx, axis=1)
    w = w / w.sum(-1, keepdims=True) * ROUTED_SCALE
    return top_idx, w


def _dispatch(top_idx, w):
    t = top_idx.shape[0]
    e_flat = top_idx.reshape(-1)
    tok_flat = jnp.repeat(jnp.arange(t, dtype=jnp.int32), TOP_K)
    g_flat = w.reshape(-1)
    order = jnp.argsort(e_flat)
    e_s, t_s, g_s = e_flat[order], tok_flat[order], g_flat[order]
    counts = jnp.bincount(e_flat, length=N_EXPERTS)
    starts = jnp.cumsum(counts) - counts
    padded = (counts + MOE_BLOCK - 1) // MOE_BLOCK * MOE_BLOCK
    pad_ends = jnp.cumsum(padded)
    pad_starts = pad_ends - padded
    dest = pad_starts[e_s] + jnp.arange(t * TOP_K, dtype=jnp.int32) - starts[e_s]
    n_rows = -(-(t * TOP_K + N_EXPERTS * (MOE_BLOCK - 1)) // MOE_BLOCK) * MOE_BLOCK
    n_blocks = n_rows // MOE_BLOCK
    row_tok = jnp.zeros((n_rows,), jnp.int32).at[dest].set(t_s)
    row_gate = jnp.zeros((n_rows,), F32).at[dest].set(g_s)
    block_start = jnp.arange(n_blocks, dtype=jnp.int32) * MOE_BLOCK
    block_expert = jnp.minimum(jnp.searchsorted(pad_ends, block_start, side='right'),
                               N_EXPERTS - 1).astype(jnp.int32)
    block_valid = (block_start < pad_ends[-1]).astype(jnp.int32)
    return row_tok, row_gate, block_expert, block_valid


def kernel(x, c, w_ada, b_ada, w_in, rel_bias, ssm_a_re, ssm_a_im, ssm_log_dt, ssm_b_re, ssm_b_im, ssm_c_re, ssm_c_im, ssm_d, w_attn_out, w_ssm_glu, w_o, ln1_g, ln1_b, w_router, router_bias, e_w_in, e_w_out, s_w_in, s_w_out, ln2_g, ln2_b):
    bsz, l, d = x.shape
    assert bsz == 1
    xf = x.reshape(l, d)
    i = 0
    mod = _modulation(c, w_ada[i], b_ada[i])
    shift1, scale1, gate1, shift2, scale2, gate2 = [mod[:, k * d:(k + 1) * d] for k in range(6)]

    w_in_b = w_in[i].astype(BF16)
    u = _modulate(xf, scale1, shift1)
    qkv = _matmul(u, w_in_b, 0, QKV_WIDTH, 512, 768, BF16, "in_proj_qkv")
    s_in = _matmul(u, w_in_b, QKV_WIDTH, SSM_WIDTH, 512, 512, F32, "in_proj_ssm")

    outs, lses = [], []
    for gi, (window, dilation) in enumerate(DILATED_GROUPS):
        bias_prev, bias_cur = _attn_bias_tables(rel_bias, gi, window, dilation)
        o, s = _attention_group(qkv, bias_prev, bias_cur, gi, dilation)
        outs.append(o)
        lses.append(s)
    attn = _attn_combine(outs, lses)

    tabs = _ssm_tables(ssm_a_re[i], ssm_a_im[i], ssm_log_dt[i], ssm_b_re[i], ssm_b_im[i],
                       ssm_c_re[i], ssm_c_im[i])
    y_ssm = _ssm(s_in, tabs, ssm_d[i])

    merged = _merge(attn, y_ssm, u, w_attn_out[i].astype(BF16), w_ssm_glu[i].astype(BF16), w_in_b)
    x1, u2, scores = _outproj(merged, xf, w_o[i].astype(BF16), w_router[i].astype(BF16), gate1,
                              ln1_g[i].reshape(1, d), ln1_b[i].reshape(1, d), scale2, shift2)

    top_idx, w = _route(scores, router_bias[i])
    row_tok, row_gate, block_expert, block_valid = _dispatch(top_idx, w)
    x_rows = u2[row_tok]
    y_rows = _experts(block_expert, block_valid, x_rows, row_gate[:, None], e_w_in[i], e_w_out[i])
    routed = jax.ops.segment_sum(y_rows, row_tok, num_segments=l)

    out = _final(u2, x1, routed, s_w_in[i].astype(BF16), s_w_out[i].astype(BF16), gate2,
                 ln2_g[i].reshape(1, d), ln2_b[i].reshape(1, d))
    return out.reshape(bsz, l, d)
```

```python
import functools
import math

import jax
import jax.numpy as jnp
from jax import lax
from jax.experimental import pallas as pl
from jax.experimental.pallas import tpu as pltpu

F32 = jnp.float32
BF16 = jnp.bfloat16

D_MODEL = 2048
HEAD_DIM = 128
HEADS_PER_GROUP = 4
DILATED_GROUPS = ((128, 1), (512, 4), (2048, 16))
N_ATTN_GROUPS = len(DILATED_GROUPS)
N_ATTN_HEADS = N_ATTN_GROUPS * HEADS_PER_GROUP
ATTN_WIDTH = N_ATTN_HEADS * HEAD_DIM
ATTN_OUT_WIDTH = HEADS_PER_GROUP * HEAD_DIM
ATTN_BLOCK = 128
REL_BUCKETS = 32
REL_MAX_DISTANCE = 2048
SSM_GROUP_CH = 16
SSM_STATE = 64
SSM_WIDTH = 1024
SSM_GROUPS = SSM_WIDTH // SSM_GROUP_CH
A_RE_MAX = -1e-4
QKV_WIDTH = 3 * ATTN_WIDTH
GATE_OFF = QKV_WIDTH + SSM_WIDTH
N_EXPERTS = 256
TOP_K = 8
N_EXPERT_GROUPS = 8
TOPK_GROUPS = 4
EXPERT_FF = 512
SHARED_FF = 512
ROUTED_SCALE = 2.5
MOE_BLOCK = 128
DEPTH = 1
DEEPNORM_ALPHA = (2 * DEPTH) ** 0.25
LN_EPS = 1e-5
NEG_BIG = -1e30

SSM_CHUNK = 128
SSM_BLK_GROUPS = 16
SSM_NBLK = SSM_GROUPS // SSM_BLK_GROUPS
SSM_BLK_CH = SSM_BLK_GROUPS * SSM_GROUP_CH
SSM_BLK_ST = SSM_BLK_GROUPS * SSM_STATE
SSM_STATES = SSM_GROUPS * SSM_STATE

VMEM_LIMIT = 56 * 1024 * 1024


def _params(sem, vmem=VMEM_LIMIT):
    return pltpu.CompilerParams(dimension_semantics=sem, vmem_limit_bytes=vmem)


def _mod_kernel(c_ref, w_ref, b_ref, o_ref):
    c = c_ref[...]
    cond = c * jax.nn.sigmoid(c)
    o_ref[...] = jnp.dot(cond, w_ref[...], preferred_element_type=F32,
                         precision=lax.Precision.HIGHEST) + b_ref[...]


def _modulation(c, w_ada, b_ada):
    d, n = w_ada.shape
    tn = 1024
    c8 = jnp.broadcast_to(c, (8, d))
    out = pl.pallas_call(
        _mod_kernel,
        out_shape=jax.ShapeDtypeStruct((8, n), F32),
        grid=(n // tn,),
        in_specs=[pl.BlockSpec((8, d), lambda j: (0, 0)),
                  pl.BlockSpec((d, tn), lambda j: (0, j)),
                  pl.BlockSpec((1, tn), lambda j: (0, j))],
        out_specs=pl.BlockSpec((8, tn), lambda j: (0, j)),
        compiler_params=_params(("arbitrary",)),
        name="adaln_mod",
    )(c8, w_ada, b_ada.reshape(1, n))
    return out[0:1]


def _modulate_kernel(x_ref, sc_ref, sh_ref, o_ref):
    o_ref[...] = (x_ref[...] * (1.0 + sc_ref[...]) + sh_ref[...]).astype(o_ref.dtype)


def _modulate(x, scale, shift):
    m, d = x.shape
    tm = 512
    return pl.pallas_call(
        _modulate_kernel,
        out_shape=jax.ShapeDtypeStruct((m, d), BF16),
        grid=(m // tm,),
        in_specs=[pl.BlockSpec((tm, d), lambda i: (i, 0)),
                  pl.BlockSpec((1, d), lambda i: (0, 0)),
                  pl.BlockSpec((1, d), lambda i: (0, 0))],
        out_specs=pl.BlockSpec((tm, d), lambda i: (i, 0)),
        compiler_params=_params(("arbitrary",)),
        name="modulate",
    )(x, scale, shift)


def _mm_kernel(a_ref, w_ref, o_ref):
    o_ref[...] = jnp.dot(a_ref[...], w_ref[...], preferred_element_type=F32).astype(o_ref.dtype)


def _matmul(a, w, col_off, n, tm, tn, out_dtype, name):
    m, k = a.shape
    assert col_off % tn == 0 and n % tn == 0 and m % tm == 0
    off_blocks = col_off // tn
    return pl.pallas_call(
        _mm_kernel,
        out_shape=jax.ShapeDtypeStruct((m, n), out_dtype),
        grid=(n // tn, m // tm),
        in_specs=[pl.BlockSpec((tm, k), lambda j, i: (i, 0)),
                  pl.BlockSpec((k, tn), lambda j, i: (0, j + off_blocks))],
        out_specs=pl.BlockSpec((tm, tn), lambda j, i: (i, j)),
        compiler_params=_params(("arbitrary", "arbitrary")),
        name=name,
    )(a, w)


def _attn_kernel(bp_ref, bc_ref, q_ref, kp_ref, kc_ref, vp_ref, vc_ref, o_ref, lse_ref):
    blk = pl.program_id(1)
    scale = HEAD_DIM ** -0.5
    nt = (((1,), (1,)), ((), ()))
    first = blk == 0
    for h in range(HEADS_PER_GROUP):
        hs = slice(h * HEAD_DIM, (h + 1) * HEAD_DIM)
        q = q_ref[:, hs]
        s_p = lax.dot_general(q, kp_ref[:, hs], nt, preferred_element_type=F32) * scale + bp_ref[h]
        s_c = lax.dot_general(q, kc_ref[:, hs], nt, preferred_element_type=F32) * scale + bc_ref[h]
        s_p = jnp.where(first, NEG_BIG, s_p)
        m = jnp.maximum(jnp.max(s_p, axis=-1, keepdims=True), jnp.max(s_c, axis=-1, keepdims=True))
        p_p = jnp.exp(s_p - m)
        p_c = jnp.exp(s_c - m)
        l = jnp.sum(p_p, axis=-1, keepdims=True) + jnp.sum(p_c, axis=-1, keepdims=True)
        o = (jnp.dot(p_p.astype(BF16), vp_ref[:, hs], preferred_element_type=F32)
             + jnp.dot(p_c.astype(BF16), vc_ref[:, hs], preferred_element_type=F32))
        o_ref[:, hs] = o / l
        lse_ref[:, hs] = jnp.broadcast_to(m + jnp.log(l), (ATTN_BLOCK, HEAD_DIM))


def _attention_group(qkv, bias_prev, bias_cur, gi, dilation):
    l, width = qkv.shape
    m = l // dilation
    nb = m // ATTN_BLOCK
    view = qkv.reshape(m, dilation * width)
    cpr = width // ATTN_OUT_WIDTH
    gq, gk, gv = gi, N_ATTN_GROUPS + gi, 2 * N_ATTN_GROUPS + gi

    def cur(cb):
        return pl.BlockSpec((ATTN_BLOCK, ATTN_OUT_WIDTH), lambda r, b: (b, r * cpr + cb))

    def prev(cb):
        return pl.BlockSpec((ATTN_BLOCK, ATTN_OUT_WIDTH),
                            lambda r, b: (jnp.maximum(b - 1, 0), r * cpr + cb))

    bias_spec = pl.BlockSpec((HEADS_PER_GROUP, ATTN_BLOCK, ATTN_BLOCK), lambda r, b: (0, 0, 0))
    out_spec = pl.BlockSpec((ATTN_BLOCK, ATTN_OUT_WIDTH), lambda r, b: (b, r))
    out, lse = pl.pallas_call(
        _attn_kernel,
        out_shape=(jax.ShapeDtypeStruct((m, dilation * ATTN_OUT_WIDTH), F32),
                   jax.ShapeDtypeStruct((m, dilation * ATTN_OUT_WIDTH), F32)),
        grid=(dilation, nb),
        in_specs=[bias_spec, bias_spec, cur(gq), prev(gk), cur(gk), prev(gv), cur(gv)],
        out_specs=(out_spec, out_spec),
        compiler_params=_params(("arbitrary", "arbitrary")),
        name=f"dilated_attn_g{gi}",
    )(bias_prev, bias_cur, view, view, view, view, view)
    return out.reshape(l, ATTN_OUT_WIDTH), lse.reshape(l, ATTN_OUT_WIDTH)


def _t5_bucket(dist):
    max_exact = REL_BUCKETS // 2
    d_f = jnp.maximum(dist, 1).astype(F32)
    large = max_exact + (jnp.log(d_f / max_exact) / math.log(REL_MAX_DISTANCE / max_exact)
                         * (REL_BUCKETS - max_exact)).astype(jnp.int32)
    large = jnp.minimum(large, REL_BUCKETS - 1)
    return jnp.where(dist < max_exact, dist, large)


def _attn_bias_tables(rel_bias, gi, window, dilation):
    n_keys = window // dilation
    bucket = _t5_bucket(jnp.arange(n_keys + 1, dtype=jnp.int32) * dilation)
    hs = slice(gi * HEADS_PER_GROUP, (gi + 1) * HEADS_PER_GROUP)
    by_dist = rel_bias[bucket][:, hs].T.astype(F32)
    qi = jnp.arange(ATTN_BLOCK)[:, None]
    kj = jnp.arange(ATTN_BLOCK)[None, :]
    d_prev = qi + ATTN_BLOCK - kj
    d_cur = qi - kj
    tabs = []
    for dist in (d_prev, d_cur):
        ok = (dist >= 0) & (dist <= n_keys)
        tabs.append(jnp.where(ok[None], by_dist[:, jnp.clip(dist, 0, n_keys)], NEG_BIG))
    return tabs


def _attn_combine_kernel(o1, o2, o3, l1, l2, l3, out_ref):
    a1, a2, a3 = l1[...], l2[...], l3[...]
    m = jnp.maximum(jnp.maximum(a1, a2), a3)
    e1, e2, e3 = jnp.exp(a1 - m), jnp.exp(a2 - m), jnp.exp(a3 - m)
    num = e1 * o1[...] + e2 * o2[...] + e3 * o3[...]
    out_ref[...] = (num / (e1 + e2 + e3)).astype(out_ref.dtype)


def _attn_combine(outs, lses):
    l, w = outs[0].shape
    tm = 1024
    spec = pl.BlockSpec((tm, w), lambda i: (i, 0))
    return pl.pallas_call(
        _attn_combine_kernel,
        out_shape=jax.ShapeDtypeStruct((l, w), BF16),
        grid=(l // tm,),
        in_specs=[spec] * 6,
        out_specs=spec,
        compiler_params=_params(("arbitrary",)),
        name="attn_combine",
    )(*outs, *lses)


def _ssm_kernel(u_ref, bre_ref, bim_ref, cre_ref, cim_ref, enr_ref, eni_ref, epr_ref, epi_ref,
                lamr_ref, lami_ref, d_ref, tri_ref, y_ref, car_ref, cai_ref):
    @pl.when(pl.program_id(0) == 0)
    def _():
        car_ref[...] = jnp.zeros_like(car_ref)
        cai_ref[...] = jnp.zeros_like(cai_ref)

    tc = u_ref.shape[0]
    tri = tri_ref[...]
    for j in range(SSM_NBLK):
        ch = slice(j * SSM_BLK_CH, (j + 1) * SSM_BLK_CH)
        st = slice(j * SSM_BLK_ST, (j + 1) * SSM_BLK_ST)
        u = u_ref[:, ch]
        ub = u.astype(BF16)
        bur = jnp.dot(ub, bre_ref[j], preferred_element_type=F32)
        bui = jnp.dot(ub, bim_ref[j], preferred_element_type=F32)
        enr, eni = enr_ref[:, st], eni_ref[:, st]
        xr = bur * enr - bui * eni
        xi = bur * eni + bui * enr
        pr = jnp.dot(tri, xr.astype(BF16), preferred_element_type=F32)
        pi = jnp.dot(tri, xi.astype(BF16), preferred_element_type=F32)
        cr, ci = car_ref[:, st], cai_ref[:, st]
        lr, li = lamr_ref[:, st], lami_ref[:, st]
        tr = pr + (lr * cr - li * ci)
        ti = pi + (lr * ci + li * cr)
        epr, epi = epr_ref[:, st], epi_ref[:, st]
        sr = epr * tr - epi * ti
        si = epr * ti + epi * tr
        car_ref[:, st] = sr[tc - 1:tc]
        cai_ref[:, st] = si[tc - 1:tc]
        y = (jnp.dot(sr.astype(BF16), cre_ref[j], preferred_element_type=F32)
             - jnp.dot(si.astype(BF16), cim_ref[j], preferred_element_type=F32))
        y_ref[:, ch] = (y + d_ref[:, ch] * u).astype(y_ref.dtype)


def _ssm_tables(a_re, a_im, log_dt, b_re, b_im, c_re, c_im):
    g, p, hc = b_re.shape
    lam_re = jnp.minimum(a_re.astype(F32), A_RE_MAX)
    lam_im = a_im.astype(F32)
    dt = jnp.exp(log_dt.astype(F32))[:, None]
    mag = jnp.exp(lam_re * dt)
    lb_re, lb_im = mag * jnp.cos(lam_im * dt), mag * jnp.sin(lam_im * dt)
    imag = jnp.exp(-lam_re * dt)
    li_re, li_im = imag * jnp.cos(lam_im * dt), -imag * jnp.sin(lam_im * dt)
    den = lam_re * lam_re + lam_im * lam_im
    nr, ni = lb_re - 1.0, lb_im
    f_re = (nr * lam_re + ni * lam_im) / den
    f_im = (ni * lam_re - nr * lam_im) / den
    bb_re = f_re[:, :, None] * b_re - f_im[:, :, None] * b_im
    bb_im = f_re[:, :, None] * b_im + f_im[:, :, None] * b_re

    def powers(pr, pi):
        er, ei = jnp.ones((1, g * p), F32), jnp.zeros((1, g * p), F32)
        pr, pi = pr.reshape(1, g * p), pi.reshape(1, g * p)
        while er.shape[0] < SSM_CHUNK:
            er, ei = (jnp.concatenate([er, er * pr - ei * pi], axis=0),
                      jnp.concatenate([ei, er * pi + ei * pr], axis=0))
            pr, pi = pr * pr - pi * pi, 2.0 * pr * pi
        return er, ei

    epr, epi = powers(lb_re, lb_im)
    enr, eni = powers(li_re, li_im)
    eye = jnp.eye(SSM_BLK_GROUPS, dtype=F32)

    def b_blocks(b):
        b = b.reshape(SSM_NBLK, SSM_BLK_GROUPS, p, hc)
        return jnp.einsum('jgph,gk->jghkp', b, eye).reshape(SSM_NBLK, SSM_BLK_CH, SSM_BLK_ST).astype(BF16)

    def c_blocks(c):
        c = c.astype(F32).reshape(SSM_NBLK, SSM_BLK_GROUPS, hc, p)
        return jnp.einsum('jghp,gk->jkpgh', c, eye).reshape(SSM_NBLK, SSM_BLK_ST, SSM_BLK_CH).astype(BF16)

    return dict(bre=b_blocks(bb_re), bim=b_blocks(bb_im), cre=c_blocks(c_re), cim=c_blocks(c_im),
                enr=enr, eni=eni, epr=epr, epi=epi,
                lamr=lb_re.reshape(1, g * p), lami=lb_im.reshape(1, g * p))


def _ssm(s_in, tabs, d_skip):
    l, w = s_in.shape
    tc = SSM_CHUNK
    tri = jnp.tril(jnp.ones((tc, tc), F32)).astype(BF16)
    full = lambda shape: pl.BlockSpec(shape, lambda c: (0,) * len(shape))
    return pl.pallas_call(
        _ssm_kernel,
        out_shape=jax.ShapeDtypeStruct((l, w), BF16),
        grid=(l // tc,),
        in_specs=[pl.BlockSpec((tc, w), lambda c: (c, 0)),
                  full((SSM_NBLK, SSM_BLK_CH, SSM_BLK_ST)), full((SSM_NBLK, SSM_BLK_CH, SSM_BLK_ST)),
                  full((SSM_NBLK, SSM_BLK_ST, SSM_BLK_CH)), full((SSM_NBLK, SSM_BLK_ST, SSM_BLK_CH)),
                  full((tc, SSM_STATES)), full((tc, SSM_STATES)),
                  full((tc, SSM_STATES)), full((tc, SSM_STATES)),
                  full((1, SSM_STATES)), full((1, SSM_STATES)),
                  full((1, w)), full((tc, tc))],
        out_specs=pl.BlockSpec((tc, w), lambda c: (c, 0)),
        scratch_shapes=[pltpu.VMEM((1, SSM_STATES), F32), pltpu.VMEM((1, SSM_STATES), F32)],
        compiler_params=_params(("arbitrary",)),
        name="s5_ssm",
    )(s_in, tabs['bre'], tabs['bim'], tabs['cre'], tabs['cim'], tabs['enr'], tabs['eni'],
      tabs['epr'], tabs['epi'], tabs['lamr'], tabs['lami'], d_skip.reshape(1, w).astype(F32), tri)


def _merge_kernel(attn_ref, y_ref, u_ref, wa_ref, wg1_ref, wg2_ref, wia_ref, wis_ref, o_ref):
    a = jnp.dot(attn_ref[...], wa_ref[...], preferred_element_type=F32)
    y = y_ref[...]
    s = (jnp.dot(y, wg1_ref[...], preferred_element_type=F32)
         * jax.nn.sigmoid(jnp.dot(y, wg2_ref[...], preferred_element_type=F32)))
    u = u_ref[...]
    ga = jax.nn.sigmoid(jnp.dot(u, wia_ref[...], preferred_element_type=F32))
    gs = jax.nn.sigmoid(jnp.dot(u, wis_ref[...], preferred_element_type=F32))
    o_ref[...] = (ga * a + gs * s).astype(o_ref.dtype)


def _merge(attn, y_ssm, u, w_attn_out, w_ssm_glu, w_in):
    l, d = u.shape
    tm, tn = 512, 512
    nd = d // tn
    go = GATE_OFF // tn
    return pl.pallas_call(
        _merge_kernel,
        out_shape=jax.ShapeDtypeStruct((l, d), BF16),
        grid=(nd, l // tm),
        in_specs=[pl.BlockSpec((tm, ATTN_OUT_WIDTH), lambda j, i: (i, 0)),
                  pl.BlockSpec((tm, SSM_WIDTH), lambda j, i: (i, 0)),
                  pl.BlockSpec((tm, d), lambda j, i: (i, 0)),
                  pl.BlockSpec((ATTN_OUT_WIDTH, tn), lambda j, i: (0, j)),
                  pl.BlockSpec((SSM_WIDTH, tn), lambda j, i: (0, j)),
                  pl.BlockSpec((SSM_WIDTH, tn), lambda j, i: (0, j + nd)),
                  pl.BlockSpec((d, tn), lambda j, i: (0, j + go)),
                  pl.BlockSpec((d, tn), lambda j, i: (0, j + go + nd))],
        out_specs=pl.BlockSpec((tm, tn), lambda j, i: (i, j)),
        compiler_params=_params(("arbitrary", "arbitrary")),
        name="branch_merge",
    )(attn, y_ssm, u, w_attn_out, w_ssm_glu, w_ssm_glu, w_in, w_in)


def _layer_norm(h, g, b):
    mu = jnp.mean(h, axis=-1, keepdims=True)
    c = h - mu
    var = jnp.mean(c * c, axis=-1, keepdims=True)
    return c * lax.rsqrt(var + LN_EPS) * g + b


def _outproj_kernel(mg_ref, x_ref, wo_ref, wr_ref, g1_ref, lg_ref, lb_ref, sc_ref, sh_ref,
                    x1_ref, u2_ref, u2p_ref, sco_ref):
    mix = jnp.dot(mg_ref[...], wo_ref[...], preferred_element_type=F32)
    x1 = _layer_norm(DEEPNORM_ALPHA * x_ref[...] + g1_ref[...] * mix, lg_ref[...], lb_ref[...])
    x1_ref[...] = x1
    u2 = (x1 * (1.0 + sc_ref[...]) + sh_ref[...]).astype(BF16)
    u2_ref[...] = u2
    bits = lax.bitcast_convert_type(u2.astype(F32), jnp.uint32)
    half = bits.shape[1] // 2
    u2p_ref[...] = (bits[:, :half] >> 16) | (bits[:, half:] & jnp.uint32(0xFFFF0000))
    logits = lax.dot_general(wr_ref[...], u2, (((1,), (1,)), ((), ())), preferred_element_type=F32)
    sco_ref[...] = jax.nn.sigmoid(logits)


def _outproj(merged, x, w_o, w_router_t, gate1, ln_g, ln_b, scale2, shift2):
    l, d = x.shape
    e = w_router_t.shape[0]
    tm = 512
    row = lambda w: pl.BlockSpec((tm, w), lambda i: (i, 0))
    vec = pl.BlockSpec((1, d), lambda i: (0, 0))
    return pl.pallas_call(
        _outproj_kernel,
        out_shape=(jax.ShapeDtypeStruct((l, d), F32), jax.ShapeDtypeStruct((l, d), BF16),
                   jax.ShapeDtypeStruct((l, d // 2), jnp.uint32),
                   jax.ShapeDtypeStruct((e, l), F32)),
        grid=(l // tm,),
        in_specs=[row(d), row(d), pl.BlockSpec((d, d), lambda i: (0, 0)),
                  pl.BlockSpec((e, d), lambda i: (0, 0)), vec, vec, vec, vec, vec],
        out_specs=(row(d), row(d), row(d // 2), pl.BlockSpec((e, tm), lambda i: (0, i))),
        compiler_params=_params(("arbitrary",)),
        name="outproj_ln1_router",
    )(merged, x, w_o, w_router_t, gate1, ln_g, ln_b, scale2, shift2)


ROUTE_TILE = 512
EXPERTS_PER_GROUP = N_EXPERTS // N_EXPERT_GROUPS


def _route_kernel(s_ref, b_ref, tri_ref, idx_ref, w_ref, rank_ref, cnt_ref, carry_ref):
    @pl.when(pl.program_id(0) == 0)
    def _():
        carry_ref[...] = jnp.zeros_like(carry_ref)

    ne, tm = s_ref.shape
    neg_inf = -jnp.inf
    s = s_ref[...]
    sel = s + b_ref[...]
    e_iota = lax.broadcasted_iota(jnp.int32, (ne, tm), 0)

    gs_rows = []
    for g in range(N_EXPERT_GROUPS):
        xg = sel[g * EXPERTS_PER_GROUP:(g + 1) * EXPERTS_PER_GROUP]
        m1 = jnp.max(xg, axis=0, keepdims=True)
        n1 = jnp.sum((xg == m1).astype(F32), axis=0, keepdims=True)
        m2 = jnp.max(jnp.where(xg < m1, xg, neg_inf), axis=0, keepdims=True)
        gs_rows.append(m1 + jnp.where(n1 >= 2.0, m1, m2))
    gs = jnp.concatenate(gs_rows, axis=0)

    g_iota = lax.broadcasted_iota(jnp.int32, gs.shape, 0)
    beaten = jnp.zeros(gs.shape, jnp.int32)
    for g2 in range(N_EXPERT_GROUPS):
        row = gs[g2:g2 + 1]
        beats = (row > gs) | ((row == gs) & (g2 < g_iota))
        beaten = beaten + beats.astype(jnp.int32)
    g_ok = beaten < TOPK_GROUPS
    work = jnp.concatenate(
        [jnp.where(g_ok[g:g + 1], sel[g * EXPERTS_PER_GROUP:(g + 1) * EXPERTS_PER_GROUP], neg_inf)
         for g in range(N_EXPERT_GROUPS)], axis=0)

    idxs, vals = [], []
    chosen = jnp.zeros((ne, tm), F32)
    for _ in range(TOP_K):
        m = jnp.max(work, axis=0, keepdims=True)
        i = jnp.min(jnp.where(work == m, e_iota, ne), axis=0, keepdims=True)
        onehot = e_iota == i
        idxs.append(i)
        vals.append(jnp.sum(jnp.where(onehot, s, 0.0), axis=0, keepdims=True))
        chosen = jnp.where(onehot, 1.0, chosen)
        work = jnp.where(onehot, neg_inf, work)
    wsum = vals[0]
    for v in vals[1:]:
        wsum = wsum + v

    before = jnp.dot(chosen.astype(BF16), tri_ref[...], preferred_element_type=F32) + carry_ref[...]
    ranks = [jnp.sum(jnp.where(e_iota == i, before, 0.0), axis=0, keepdims=True) for i in idxs]
    carry_ref[...] = carry_ref[...] + jnp.sum(chosen, axis=1, keepdims=True)

    idx_ref[...] = jnp.concatenate(idxs, axis=0)
    w_ref[...] = jnp.concatenate([v / wsum * ROUTED_SCALE for v in vals], axis=0)
    rank_ref[...] = jnp.concatenate(ranks, axis=0).astype(jnp.int32)
    cnt_ref[...] = carry_ref[...]


def _route(scores_t, router_bias):
    ne, t = scores_t.shape
    tm = ROUTE_TILE
    tri = jnp.triu(jnp.ones((tm, tm), F32), k=1).astype(BF16)
    tok = pl.BlockSpec((TOP_K, tm), lambda i: (0, i))
    return pl.pallas_call(
        _route_kernel,
        out_shape=(jax.ShapeDtypeStruct((TOP_K, t), jnp.int32), jax.ShapeDtypeStruct((TOP_K, t), F32),
                   jax.ShapeDtypeStruct((TOP_K, t), jnp.int32), jax.ShapeDtypeStruct((ne, 1), F32)),
        grid=(t // tm,),
        in_specs=[pl.BlockSpec((ne, tm), lambda i: (0, i)),
                  pl.BlockSpec((ne, 1), lambda i: (0, 0)),
                  pl.BlockSpec((tm, tm), lambda i: (0, 0))],
        out_specs=(tok, tok, tok, pl.BlockSpec((ne, 1), lambda i: (0, 0))),
        scratch_shapes=[pltpu.VMEM((ne, 1), F32)],
        compiler_params=_params(("arbitrary",)),
        name="moe_route",
    )(scores_t, router_bias.astype(F32).reshape(ne, 1), tri)


DISPATCH_TILE = 256


def _dispatch_kernel(idx_ref, rank_ref, ps_ref, u_ref, xs_in_ref, xs_ref, sem):
    del xs_in_ref
    tm = u_ref.shape[0]

    def row_copy(t, p):
        return pltpu.make_async_copy(u_ref.at[pl.ds(t, 1)], xs_ref.at[pl.ds(p, 1)], sem)

    def issue(t, carry):
        for k in range(TOP_K):
            row_copy(t, ps_ref[idx_ref[k, t]] + rank_ref[k, t]).start()
        return carry

    lax.fori_loop(0, tm, issue, 0)

    def drain(t, carry):
        for _ in range(TOP_K):
            row_copy(0, 0).wait()
        return carry

    lax.fori_loop(0, tm, drain, 0)


def _dispatch(idx, rank, pad_start, u2p, n_rows):
    t, dw = u2p.shape
    tm = DISPATCH_TILE
    smem_tok = pl.BlockSpec((TOP_K, tm), lambda i: (0, i), memory_space=pltpu.SMEM)
    return pl.pallas_call(
        _dispatch_kernel,
        out_shape=jax.ShapeDtypeStruct((n_rows, dw), jnp.uint32),
        grid=(t // tm,),
        in_specs=[smem_tok, smem_tok,
                  pl.BlockSpec(memory_space=pltpu.SMEM),
                  pl.BlockSpec((tm, dw), lambda i: (i, 0)),
                  pl.BlockSpec(memory_space=pl.ANY)],
        out_specs=pl.BlockSpec(memory_space=pl.ANY),
        scratch_shapes=[pltpu.SemaphoreType.DMA],
        input_output_aliases={4: 0},
        compiler_params=_params(("arbitrary",)),
        name="moe_dispatch",
    )(idx, rank, pad_start, u2p, jnp.zeros((n_rows, dw), jnp.uint32))


def _unpack_rows(xp):
    lo = lax.bitcast_convert_type(xp << 16, F32).astype(BF16)
    hi = lax.bitcast_convert_type(xp & jnp.uint32(0xFFFF0000), F32).astype(BF16)
    return lo, hi


def _expert_kernel(be_ref, bv_ref, x_ref, win_ref, wout_ref, y_ref, wbi_ref, wbo_ref):
    b = pl.program_id(0)
    valid = bv_ref[b] > 0
    new_expert = (b == 0) | (be_ref[b] != be_ref[jnp.maximum(b - 1, 0)])

    @pl.when(valid & new_expert)
    def _():
        wbi_ref[...] = win_ref[0].astype(BF16)
        wbo_ref[...] = wout_ref[0].astype(BF16)

    @pl.when(valid)
    def _():
        lo, hi = _unpack_rows(x_ref[...])
        half = lo.shape[1]
        h = (jnp.dot(lo, wbi_ref[:half], preferred_element_type=F32)
             + jnp.dot(hi, wbi_ref[half:], preferred_element_type=F32))
        hg, hu = h[:, :EXPERT_FF], h[:, EXPERT_FF:]
        act = (hg * jax.nn.sigmoid(hg) * hu).astype(BF16)
        y_ref[...] = jnp.dot(act, wbo_ref[...], preferred_element_type=F32)

    @pl.when(jnp.logical_not(valid))
    def _():
        y_ref[...] = jnp.zeros_like(y_ref)


def _experts(block_expert, block_valid, x_rows, e_w_in, e_w_out):
    n_rows, dw = x_rows.shape
    nb = n_rows // MOE_BLOCK
    _, d, ff2 = e_w_in.shape
    grid_spec = pltpu.PrefetchScalarGridSpec(
        num_scalar_prefetch=2,
        grid=(nb,),
        in_specs=[pl.BlockSpec((MOE_BLOCK, dw), lambda b, be, bv: (b, 0)),
                  pl.BlockSpec((1, d, ff2), lambda b, be, bv: (be[b], 0, 0)),
                  pl.BlockSpec((1, ff2 // 2, d), lambda b, be, bv: (be[b], 0, 0))],
        out_specs=pl.BlockSpec((MOE_BLOCK, d), lambda b, be, bv: (b, 0)),
        scratch_shapes=[pltpu.VMEM((d, ff2), BF16), pltpu.VMEM((ff2 // 2, d), BF16)],
    )
    return pl.pallas_call(
        _expert_kernel,
        out_shape=jax.ShapeDtypeStruct((n_rows, d), F32),
        grid_spec=grid_spec,
        compiler_params=_params(("arbitrary",)),
        name="routed_experts",
    )(block_expert, block_valid, x_rows, e_w_in, e_w_out)


COMBINE_TILE = 128


def _final_kernel(idx_ref, rank_ref, ps_ref, u2_ref, x1_ref, w_ref, ys_ref, win_ref, wout_ref,
                  g2_ref, lg_ref, lb_ref, o_ref, buf_ref, sem):
    tm = u2_ref.shape[0]

    def row_copy(k, t, p):
        return pltpu.make_async_copy(ys_ref.at[pl.ds(p, 1)], buf_ref.at[k, pl.ds(t, 1)], sem)

    def issue(t, carry):
        for k in range(TOP_K):
            row_copy(k, t, ps_ref[idx_ref[k, t]] + rank_ref[k, t]).start()
        return carry

    lax.fori_loop(0, tm, issue, 0)

    h = jnp.dot(u2_ref[...], win_ref[...], preferred_element_type=F32)
    hg, hu = h[:, :SHARED_FF], h[:, SHARED_FF:]
    act = (hg * jax.nn.sigmoid(hg) * hu).astype(BF16)
    ffn = jnp.dot(act, wout_ref[...], preferred_element_type=F32)

    def drain(t, carry):
        for k in range(TOP_K):
            row_copy(k, 0, 0).wait()
        return carry

    lax.fori_loop(0, tm, drain, 0)

    w = w_ref[...]
    for k in range(TOP_K):
        ffn = ffn + w[:, k:k + 1] * buf_ref[k]
    o_ref[...] = _layer_norm(DEEPNORM_ALPHA * x1_ref[...] + g2_ref[...] * ffn, lg_ref[...], lb_ref[...])


def _final(idx, rank, pad_start, u2, x1, w_tok, y_rows, s_w_in, s_w_out, gate2, ln_g, ln_b):
    l, d = x1.shape
    tm = COMBINE_TILE
    row = pl.BlockSpec((tm, d), lambda i: (i, 0))
    vec = pl.BlockSpec((1, d), lambda i: (0, 0))
    smem_tok = pl.BlockSpec((TOP_K, tm), lambda i: (0, i), memory_space=pltpu.SMEM)
    return pl.pallas_call(
        _final_kernel,
        out_shape=jax.ShapeDtypeStruct((l, d), F32),
        grid=(l // tm,),
        in_specs=[smem_tok, smem_tok, pl.BlockSpec(memory_space=pltpu.SMEM),
                  row, row, pl.BlockSpec((tm, TOP_K), lambda i: (i, 0)),
                  pl.BlockSpec(memory_space=pl.ANY),
                  pl.BlockSpec(s_w_in.shape, lambda i: (0, 0)),
                  pl.BlockSpec(s_w_out.shape, lambda i: (0, 0)), vec, vec, vec],
        out_specs=row,
        scratch_shapes=[pltpu.VMEM((TOP_K, tm, d), F32), pltpu.SemaphoreType.DMA],
        compiler_params=_params(("arbitrary",)),
        name="combine_shared_ln2",
    )(idx, rank, pad_start, u2, x1, w_tok, y_rows, s_w_in, s_w_out, gate2, ln_g, ln_b)


def _block_layout(counts, n_tokens):
    padded = (counts + MOE_BLOCK - 1) // MOE_BLOCK * MOE_BLOCK
    pad_ends = jnp.cumsum(padded)
    pad_starts = (pad_ends - padded).astype(jnp.int32)
    n_rows = -(-(n_tokens * TOP_K + N_EXPERTS * (MOE_BLOCK - 1)) // MOE_BLOCK) * MOE_BLOCK
    block_start = jnp.arange(n_rows // MOE_BLOCK, dtype=jnp.int32) * MOE_BLOCK
    block_expert = jnp.minimum(jnp.sum((block_start[:, None] >= pad_ends[None, :]).astype(jnp.int32), axis=1),
                               N_EXPERTS - 1).astype(jnp.int32)
    block_valid = (block_start < pad_ends[-1]).astype(jnp.int32)
    return pad_starts, block_expert, block_valid, n_rows


def kernel(x, c, w_ada, b_ada, w_in, rel_bias, ssm_a_re, ssm_a_im, ssm_log_dt, ssm_b_re, ssm_b_im, ssm_c_re, ssm_c_im, ssm_d, w_attn_out, w_ssm_glu, w_o, ln1_g, ln1_b, w_router, router_bias, e_w_in, e_w_out, s_w_in, s_w_out, ln2_g, ln2_b):
    bsz, l, d = x.shape
    assert bsz == 1
    xf = x.reshape(l, d)
    i = 0
    mod = _modulation(c, w_ada[i], b_ada[i])
    shift1, scale1, gate1, shift2, scale2, gate2 = [mod[:, k * d:(k + 1) * d] for k in range(6)]

    w_in_b = w_in[i].astype(BF16)
    u = _modulate(xf, scale1, shift1)
    qkv = _matmul(u, w_in_b, 0, QKV_WIDTH, 512, 768, BF16, "in_proj_qkv")
    s_in = _matmul(u, w_in_b, QKV_WIDTH, SSM_WIDTH, 512, 512, F32, "in_proj_ssm")

    outs, lses = [], []
    for gi, (window, dilation) in enumerate(DILATED_GROUPS):
        bias_prev, bias_cur = _attn_bias_tables(rel_bias, gi, window, dilation)
        o, s = _attention_group(qkv, bias_prev, bias_cur, gi, dilation)
        outs.append(o)
        lses.append(s)
    attn = _attn_combine(outs, lses)

    tabs = _ssm_tables(ssm_a_re[i], ssm_a_im[i], ssm_log_dt[i], ssm_b_re[i], ssm_b_im[i],
                       ssm_c_re[i], ssm_c_im[i])
    y_ssm = _ssm(s_in, tabs, ssm_d[i])

    merged = _merge(attn, y_ssm, u, w_attn_out[i].astype(BF16), w_ssm_glu[i].astype(BF16), w_in_b)
    x1, u2, u2p, scores_t = _outproj(merged, xf, w_o[i].astype(BF16), w_router[i].T.astype(BF16), gate1,
                                     ln1_g[i].reshape(1, d), ln1_b[i].reshape(1, d), scale2, shift2)

    idx, w, rank, counts = _route(scores_t, router_bias[i])
    pad_start, block_expert, block_valid, n_rows = _block_layout(counts[:, 0].astype(jnp.int32), l)
    x_rows = _dispatch(idx, rank, pad_start, u2p, n_rows)
    y_rows = _experts(block_expert, block_valid, x_rows, e_w_in[i], e_w_out[i])
    out = _final(idx, rank, pad_start, u2, x1, w.T, y_rows, s_w_in[i].astype(BF16),
                 s_w_out[i].astype(BF16), gate2, ln2_g[i].reshape(1, d), ln2_b[i].reshape(1, d))
    return out.reshape(bsz, l, d)
```

```python
import functools
import math

import jax
import jax.numpy as jnp
from jax import lax
from jax.experimental import pallas as pl
from jax.experimental.pallas import tpu as pltpu

F32 = jnp.float32
BF16 = jnp.bfloat16

D_MODEL = 2048
HEAD_DIM = 128
HEADS_PER_GROUP = 4
DILATED_GROUPS = ((128, 1), (512, 4), (2048, 16))
N_ATTN_GROUPS = len(DILATED_GROUPS)
N_ATTN_HEADS = N_ATTN_GROUPS * HEADS_PER_GROUP
ATTN_WIDTH = N_ATTN_HEADS * HEAD_DIM
ATTN_OUT_WIDTH = HEADS_PER_GROUP * HEAD_DIM
ATTN_BLOCK = 128
REL_BUCKETS = 32
REL_MAX_DISTANCE = 2048
SSM_GROUP_CH = 16
SSM_STATE = 64
SSM_WIDTH = 1024
SSM_GROUPS = SSM_WIDTH // SSM_GROUP_CH
A_RE_MAX = -1e-4
QKV_WIDTH = 3 * ATTN_WIDTH
GATE_OFF = QKV_WIDTH + SSM_WIDTH
N_EXPERTS = 256
TOP_K = 8
N_EXPERT_GROUPS = 8
TOPK_GROUPS = 4
EXPERT_FF = 512
SHARED_FF = 512
ROUTED_SCALE = 2.5
MOE_BLOCK = 128
DEPTH = 1
DEEPNORM_ALPHA = (2 * DEPTH) ** 0.25
LN_EPS = 1e-5
NEG_BIG = -1e30
LANES = 128

SSM_CHUNK = 128
SSM_BLK_GROUPS = 16
SSM_NBLK = SSM_GROUPS // SSM_BLK_GROUPS
SSM_BLK_CH = SSM_BLK_GROUPS * SSM_GROUP_CH
SSM_BLK_ST = SSM_BLK_GROUPS * SSM_STATE
SSM_STATES = SSM_GROUPS * SSM_STATE

VMEM_LIMIT = 56 * 1024 * 1024


def _params(sem, vmem=VMEM_LIMIT):
    return pltpu.CompilerParams(dimension_semantics=sem, vmem_limit_bytes=vmem)


def _mod_kernel(c_ref, w_ref, b_ref, o_ref):
    c = c_ref[...]
    cond = c * jax.nn.sigmoid(c)
    o_ref[...] = jnp.dot(cond, w_ref[...], preferred_element_type=F32,
                         precision=lax.Precision.HIGHEST) + b_ref[...]


def _modulation(c, w_ada, b_ada):
    d, n = w_ada.shape
    tn = 1024
    c8 = jnp.broadcast_to(c, (8, d))
    out = pl.pallas_call(
        _mod_kernel,
        out_shape=jax.ShapeDtypeStruct((8, n), F32),
        grid=(n // tn,),
        in_specs=[pl.BlockSpec((8, d), lambda j: (0, 0)),
                  pl.BlockSpec((d, tn), lambda j: (0, j)),
                  pl.BlockSpec((1, tn), lambda j: (0, j))],
        out_specs=pl.BlockSpec((8, tn), lambda j: (0, j)),
        compiler_params=_params(("arbitrary",)),
        name="adaln_mod",
    )(c8, w_ada, b_ada.reshape(1, n))
    return out[0:1]


def _modulate_kernel(x_ref, sc_ref, sh_ref, o_ref):
    o_ref[...] = (x_ref[...] * (1.0 + sc_ref[...]) + sh_ref[...]).astype(o_ref.dtype)


def _modulate(x, scale, shift):
    m, d = x.shape
    tm = 512
    return pl.pallas_call(
        _modulate_kernel,
        out_shape=jax.ShapeDtypeStruct((m, d), BF16),
        grid=(m // tm,),
        in_specs=[pl.BlockSpec((tm, d), lambda i: (i, 0)),
                  pl.BlockSpec((1, d), lambda i: (0, 0)),
                  pl.BlockSpec((1, d), lambda i: (0, 0))],
        out_specs=pl.BlockSpec((tm, d), lambda i: (i, 0)),
        compiler_params=_params(("arbitrary",)),
        name="modulate",
    )(x, scale, shift)


def _mm_kernel(a_ref, w_ref, o_ref):
    o_ref[...] = jnp.dot(a_ref[...], w_ref[...], preferred_element_type=F32).astype(o_ref.dtype)


def _matmul(a, w, col_off, n, tm, tn, out_dtype, name):
    m, k = a.shape
    assert col_off % tn == 0 and n % tn == 0 and m % tm == 0
    off_blocks = col_off // tn
    return pl.pallas_call(
        _mm_kernel,
        out_shape=jax.ShapeDtypeStruct((m, n), out_dtype),
        grid=(n // tn, m // tm),
        in_specs=[pl.BlockSpec((tm, k), lambda j, i: (i, 0)),
                  pl.BlockSpec((k, tn), lambda j, i: (0, j + off_blocks))],
        out_specs=pl.BlockSpec((tm, tn), lambda j, i: (i, j)),
        compiler_params=_params(("arbitrary", "arbitrary")),
        name=name,
    )(a, w)


def _attn_kernel(bp_ref, bc_ref, q_ref, kp_ref, kc_ref, vp_ref, vc_ref, o_ref, lse_ref):
    blk = pl.program_id(1)
    scale = HEAD_DIM ** -0.5
    nt = (((1,), (1,)), ((), ()))
    first = blk == 0
    for h in range(HEADS_PER_GROUP):
        hs = slice(h * HEAD_DIM, (h + 1) * HEAD_DIM)
        q = q_ref[:, hs]
        s_p = lax.dot_general(q, kp_ref[:, hs], nt, preferred_element_type=F32) * scale + bp_ref[h]
        s_c = lax.dot_general(q, kc_ref[:, hs], nt, preferred_element_type=F32) * scale + bc_ref[h]
        s_p = jnp.where(first, NEG_BIG, s_p)
        m = jnp.maximum(jnp.max(s_p, axis=-1, keepdims=True), jnp.max(s_c, axis=-1, keepdims=True))
        p_p = jnp.exp(s_p - m)
        p_c = jnp.exp(s_c - m)
        l = jnp.sum(p_p, axis=-1, keepdims=True) + jnp.sum(p_c, axis=-1, keepdims=True)
        o = (jnp.dot(p_p.astype(BF16), vp_ref[:, hs], preferred_element_type=F32)
             + jnp.dot(p_c.astype(BF16), vc_ref[:, hs], preferred_element_type=F32))
        o_ref[:, hs] = o / l
        lse_ref[:, hs] = jnp.broadcast_to(m + jnp.log(l), (ATTN_BLOCK, HEAD_DIM))


def _inproj_dilated_kernel(a_ref, w_ref, o_ref, acc_ref):
    res = jnp.dot(a_ref[...], w_ref[...], preferred_element_type=F32)
    dilation, rows, tn = o_ref.shape
    if dilation == 1:
        o_ref[0] = res.astype(o_ref.dtype)
        return
    for c in range(tn // LANES):
        acc_ref[c] = res[:, c * LANES:(c + 1) * LANES]
    for r in range(dilation):
        for c in range(tn // LANES):
            o_ref[r, :, c * LANES:(c + 1) * LANES] = (
                acc_ref[c, pl.ds(r, rows, stride=dilation), :].astype(o_ref.dtype))


def _inproj_qkv_group(u, w_in, gi, dilation):
    l, k = u.shape
    tm, tn = 512, ATTN_OUT_WIDTH
    return pl.pallas_call(
        _inproj_dilated_kernel,
        out_shape=jax.ShapeDtypeStruct((dilation, l // dilation, 3 * tn), BF16),
        grid=(3, l // tm),
        in_specs=[pl.BlockSpec((tm, k), lambda j, i: (i, 0)),
                  pl.BlockSpec((k, tn), lambda j, i: (0, j * N_ATTN_GROUPS + gi))],
        out_specs=pl.BlockSpec((dilation, tm // dilation, tn), lambda j, i: (0, i, j)),
        scratch_shapes=[pltpu.VMEM((tn // LANES, tm, LANES), F32)],
        compiler_params=_params(("arbitrary", "arbitrary")),
        name=f"in_proj_qkv_g{gi}",
    )(u, w_in)


def _attention_group(qkv, bias_prev, bias_cur, gi):
    dilation, m, _ = qkv.shape
    nb = m // ATTN_BLOCK

    def cur(cb):
        return pl.BlockSpec((None, ATTN_BLOCK, ATTN_OUT_WIDTH), lambda r, b: (r, b, cb))

    def prev(cb):
        return pl.BlockSpec((None, ATTN_BLOCK, ATTN_OUT_WIDTH),
                            lambda r, b: (r, jnp.maximum(b - 1, 0), cb))

    bias_spec = pl.BlockSpec((HEADS_PER_GROUP, ATTN_BLOCK, ATTN_BLOCK), lambda r, b: (0, 0, 0))
    out_spec = pl.BlockSpec((None, ATTN_BLOCK, ATTN_OUT_WIDTH), lambda r, b: (r, b, 0))
    return pl.pallas_call(
        _attn_kernel,
        out_shape=(jax.ShapeDtypeStruct((dilation, m, ATTN_OUT_WIDTH), F32),
                   jax.ShapeDtypeStruct((dilation, m, ATTN_OUT_WIDTH), F32)),
        grid=(dilation, nb),
        in_specs=[bias_spec, bias_spec, cur(0), prev(1), cur(1), prev(2), cur(2)],
        out_specs=(out_spec, out_spec),
        compiler_params=_params(("arbitrary", "arbitrary")),
        name=f"dilated_attn_g{gi}",
    )(bias_prev, bias_cur, qkv, qkv, qkv, qkv, qkv)


def _t5_bucket(dist):
    max_exact = REL_BUCKETS // 2
    d_f = jnp.maximum(dist, 1).astype(F32)
    large = max_exact + (jnp.log(d_f / max_exact) / math.log(REL_MAX_DISTANCE / max_exact)
                         * (REL_BUCKETS - max_exact)).astype(jnp.int32)
    large = jnp.minimum(large, REL_BUCKETS - 1)
    return jnp.where(dist < max_exact, dist, large)


def _attn_bias_tables(rel_bias, gi, window, dilation):
    n_keys = window // dilation
    bucket = _t5_bucket(jnp.arange(n_keys + 1, dtype=jnp.int32) * dilation)
    hs = slice(gi * HEADS_PER_GROUP, (gi + 1) * HEADS_PER_GROUP)
    hi = lax.Precision.HIGHEST
    pick_bucket = (bucket[:, None] == jnp.arange(REL_BUCKETS)[None, :]).astype(F32)
    by_dist = jnp.dot(pick_bucket, rel_bias[:, hs].astype(F32), precision=hi)
    qi = jnp.arange(ATTN_BLOCK)[:, None]
    kj = jnp.arange(ATTN_BLOCK)[None, :]
    d_prev = qi + ATTN_BLOCK - kj
    d_cur = qi - kj
    tabs = []
    for dist in (d_prev, d_cur):
        ok = (dist >= 0) & (dist <= n_keys)
        pick_dist = (dist[:, :, None] == jnp.arange(n_keys + 1)[None, None, :]).astype(F32)
        vals = jnp.einsum('ijd,dh->hij', pick_dist, by_dist, precision=hi)
        tabs.append(jnp.where(ok[None], vals, NEG_BIG))
    return tabs


ATTN_COMBINE_TILE = 512


def _attn_combine_kernel(o1, o2, o3, l1, l2, l3, out_ref, *scratch):
    def token_order(ref, buf, c):
        dilation, rows, _ = ref.shape
        cs = slice(c * LANES, (c + 1) * LANES)
        if dilation == 1:
            return ref[0, :, cs]
        for r in range(dilation):
            buf[c, pl.ds(r, rows, stride=dilation), :] = ref[r, :, cs]
        return buf[c]

    for c in range(out_ref.shape[1] // LANES):
        v1, v2, v3, a1, a2, a3 = [token_order(ref, buf, c)
                                  for ref, buf in zip((o1, o2, o3, l1, l2, l3), scratch)]
        m = jnp.maximum(jnp.maximum(a1, a2), a3)
        e1, e2, e3 = jnp.exp(a1 - m), jnp.exp(a2 - m), jnp.exp(a3 - m)
        num = e1 * v1 + e2 * v2 + e3 * v3
        out_ref[:, c * LANES:(c + 1) * LANES] = (num / (e1 + e2 + e3)).astype(out_ref.dtype)


def _attn_combine(outs, lses):
    w = outs[0].shape[-1]
    l = outs[0].shape[0] * outs[0].shape[1]
    tm = ATTN_COMBINE_TILE

    def spec(a):
        dilation = a.shape[0]
        return pl.BlockSpec((dilation, tm // dilation, w), lambda i: (0, i, 0))

    return pl.pallas_call(
        _attn_combine_kernel,
        out_shape=jax.ShapeDtypeStruct((l, w), BF16),
        grid=(l // tm,),
        in_specs=[spec(a) for a in (*outs, *lses)],
        out_specs=pl.BlockSpec((tm, w), lambda i: (i, 0)),
        scratch_shapes=[pltpu.VMEM((w // LANES, tm, LANES), F32)] * 6,
        compiler_params=_params(("arbitrary",)),
        name="attn_combine",
    )(*outs, *lses)


def _ssm_kernel(u_ref, bre_ref, bim_ref, cre_ref, cim_ref, enr_ref, eni_ref, epr_ref, epi_ref,
                lamr_ref, lami_ref, d_ref, tri_ref, y_ref, car_ref, cai_ref):
    @pl.when(pl.program_id(0) == 0)
    def _():
        car_ref[...] = jnp.zeros_like(car_ref)
        cai_ref[...] = jnp.zeros_like(cai_ref)

    tc = u_ref.shape[0]
    tri = tri_ref[...]
    for j in range(SSM_NBLK):
        ch = slice(j * SSM_BLK_CH, (j + 1) * SSM_BLK_CH)
        st = slice(j * SSM_BLK_ST, (j + 1) * SSM_BLK_ST)
        u = u_ref[:, ch]
        ub = u.astype(BF16)
        bur = jnp.dot(ub, bre_ref[j], preferred_element_type=F32)
        bui = jnp.dot(ub, bim_ref[j], preferred_element_type=F32)
        enr, eni = enr_ref[:, st], eni_ref[:, st]
        xr = bur * enr - bui * eni
        xi = bur * eni + bui * enr
        pr = jnp.dot(tri, xr.astype(BF16), preferred_element_type=F32)
        pi = jnp.dot(tri, xi.astype(BF16), preferred_element_type=F32)
        cr, ci = car_ref[:, st], cai_ref[:, st]
        lr, li = lamr_ref[:, st], lami_ref[:, st]
        tr = pr + (lr * cr - li * ci)
        ti = pi + (lr * ci + li * cr)
        epr, epi = epr_ref[:, st], epi_ref[:, st]
        sr = epr * tr - epi * ti
        si = epr * ti + epi * tr
        car_ref[:, st] = sr[tc - 1:tc]
        cai_ref[:, st] = si[tc - 1:tc]
        y = (jnp.dot(sr.astype(BF16), cre_ref[j], preferred_element_type=F32)
             - jnp.dot(si.astype(BF16), cim_ref[j], preferred_element_type=F32))
        y_ref[:, ch] = (y + d_ref[:, ch] * u).astype(y_ref.dtype)


def _ssm_tables(a_re, a_im, log_dt, b_re, b_im, c_re, c_im):
    g, p, hc = b_re.shape
    lam_re = jnp.minimum(a_re.astype(F32), A_RE_MAX)
    lam_im = a_im.astype(F32)
    dt = jnp.exp(log_dt.astype(F32))[:, None]
    mag = jnp.exp(lam_re * dt)
    lb_re, lb_im = mag * jnp.cos(lam_im * dt), mag * jnp.sin(lam_im * dt)
    imag = jnp.exp(-lam_re * dt)
    li_re, li_im = imag * jnp.cos(lam_im * dt), -imag * jnp.sin(lam_im * dt)
    den = lam_re * lam_re + lam_im * lam_im
    nr, ni = lb_re - 1.0, lb_im
    f_re = (nr * lam_re + ni * lam_im) / den
    f_im = (ni * lam_re - nr * lam_im) / den
    bb_re = f_re[:, :, None] * b_re - f_im[:, :, None] * b_im
    bb_im = f_re[:, :, None] * b_im + f_im[:, :, None] * b_re

    def powers(pr, pi):
        er, ei = jnp.ones((1, g * p), F32), jnp.zeros((1, g * p), F32)
        pr, pi = pr.reshape(1, g * p), pi.reshape(1, g * p)
        while er.shape[0] < SSM_CHUNK:
            er, ei = (jnp.concatenate([er, er * pr - ei * pi], axis=0),
                      jnp.concatenate([ei, er * pi + ei * pr], axis=0))
            pr, pi = pr * pr - pi * pi, 2.0 * pr * pi
        return er, ei

    epr, epi = powers(lb_re, lb_im)
    enr, eni = powers(li_re, li_im)
    eye = jnp.eye(SSM_BLK_GROUPS, dtype=F32)

    def b_blocks(b):
        b = b.reshape(SSM_NBLK, SSM_BLK_GROUPS, p, hc)
        return jnp.einsum('jgph,gk->jghkp', b, eye).reshape(SSM_NBLK, SSM_BLK_CH, SSM_BLK_ST).astype(BF16)

    def c_blocks(c):
        c = c.astype(F32).reshape(SSM_NBLK, SSM_BLK_GROUPS, hc, p)
        return jnp.einsum('jghp,gk->jkpgh', c, eye).reshape(SSM_NBLK, SSM_BLK_ST, SSM_BLK_CH).astype(BF16)

    return dict(bre=b_blocks(bb_re), bim=b_blocks(bb_im), cre=c_blocks(c_re), cim=c_blocks(c_im),
                enr=enr, eni=eni, epr=epr, epi=epi,
                lamr=lb_re.reshape(1, g * p), lami=lb_im.reshape(1, g * p))


def _ssm(s_in, tabs, d_skip):
    l, w = s_in.shape
    tc = SSM_CHUNK
    tri = jnp.tril(jnp.ones((tc, tc), F32)).astype(BF16)
    full = lambda shape: pl.BlockSpec(shape, lambda c: (0,) * len(shape))
    return pl.pallas_call(
        _ssm_kernel,
        out_shape=jax.ShapeDtypeStruct((l, w), BF16),
        grid=(l // tc,),
        in_specs=[pl.BlockSpec((tc, w), lambda c: (c, 0)),
                  full((SSM_NBLK, SSM_BLK_CH, SSM_BLK_ST)), full((SSM_NBLK, SSM_BLK_CH, SSM_BLK_ST)),
                  full((SSM_NBLK, SSM_BLK_ST, SSM_BLK_CH)), full((SSM_NBLK, SSM_BLK_ST, SSM_BLK_CH)),
                  full((tc, SSM_STATES)), full((tc, SSM_STATES)),
                  full((tc, SSM_STATES)), full((tc, SSM_STATES)),
                  full((1, SSM_STATES)), full((1, SSM_STATES)),
                  full((1, w)), full((tc, tc))],
        out_specs=pl.BlockSpec((tc, w), lambda c: (c, 0)),
        scratch_shapes=[pltpu.VMEM((1, SSM_STATES), F32), pltpu.VMEM((1, SSM_STATES), F32)],
        compiler_params=_params(("arbitrary",)),
        name="s5_ssm",
    )(s_in, tabs['bre'], tabs['bim'], tabs['cre'], tabs['cim'], tabs['enr'], tabs['eni'],
      tabs['epr'], tabs['epi'], tabs['lamr'], tabs['lami'], d_skip.reshape(1, w).astype(F32), tri)


def _merge_kernel(attn_ref, y_ref, u_ref, wa_ref, wg1_ref, wg2_ref, wia_ref, wis_ref, o_ref):
    a = jnp.dot(attn_ref[...], wa_ref[...], preferred_element_type=F32)
    y = y_ref[...]
    s = (jnp.dot(y, wg1_ref[...], preferred_element_type=F32)
         * jax.nn.sigmoid(jnp.dot(y, wg2_ref[...], preferred_element_type=F32)))
    u = u_ref[...]
    ga = jax.nn.sigmoid(jnp.dot(u, wia_ref[...], preferred_element_type=F32))
    gs = jax.nn.sigmoid(jnp.dot(u, wis_ref[...], preferred_element_type=F32))
    o_ref[...] = (ga * a + gs * s).astype(o_ref.dtype)


def _merge(attn, y_ssm, u, w_attn_out, w_ssm_glu, w_in):
    l, d = u.shape
    tm, tn = 512, 512
    nd = d // tn
    go = GATE_OFF // tn
    return pl.pallas_call(
        _merge_kernel,
        out_shape=jax.ShapeDtypeStruct((l, d), BF16),
        grid=(nd, l // tm),
        in_specs=[pl.BlockSpec((tm, ATTN_OUT_WIDTH), lambda j, i: (i, 0)),
                  pl.BlockSpec((tm, SSM_WIDTH), lambda j, i: (i, 0)),
                  pl.BlockSpec((tm, d), lambda j, i: (i, 0)),
                  pl.BlockSpec((ATTN_OUT_WIDTH, tn), lambda j, i: (0, j)),
                  pl.BlockSpec((SSM_WIDTH, tn), lambda j, i: (0, j)),
                  pl.BlockSpec((SSM_WIDTH, tn), lambda j, i: (0, j + nd)),
                  pl.BlockSpec((d, tn), lambda j, i: (0, j + go)),
                  pl.BlockSpec((d, tn), lambda j, i: (0, j + go + nd))],
        out_specs=pl.BlockSpec((tm, tn), lambda j, i: (i, j)),
        compiler_params=_params(("arbitrary", "arbitrary")),
        name="branch_merge",
    )(attn, y_ssm, u, w_attn_out, w_ssm_glu, w_ssm_glu, w_in, w_in)


def _layer_norm(h, g, b):
    mu = jnp.mean(h, axis=-1, keepdims=True)
    c = h - mu
    var = jnp.mean(c * c, axis=-1, keepdims=True)
    return c * lax.rsqrt(var + LN_EPS) * g + b


def _outproj_kernel(mg_ref, x_ref, wo_ref, wr_ref, g1_ref, lg_ref, lb_ref, sc_ref, sh_ref,
                    x1_ref, u2_ref, u2p_ref, sco_ref):
    mix = jnp.dot(mg_ref[...], wo_ref[...], preferred_element_type=F32)
    x1 = _layer_norm(DEEPNORM_ALPHA * x_ref[...] + g1_ref[...] * mix, lg_ref[...], lb_ref[...])
    x1_ref[...] = x1
    u2 = (x1 * (1.0 + sc_ref[...]) + sh_ref[...]).astype(BF16)
    u2_ref[...] = u2
    bits = lax.bitcast_convert_type(u2.astype(F32), jnp.uint32)
    half = bits.shape[1] // 2
    u2p_ref[...] = (bits[:, :half] >> 16) | (bits[:, half:] & jnp.uint32(0xFFFF0000))
    logits = lax.dot_general(wr_ref[...], u2, (((1,), (1,)), ((), ())), preferred_element_type=F32)
    sco_ref[...] = jax.nn.sigmoid(logits)


def _outproj(merged, x, w_o, w_router_t, gate1, ln_g, ln_b, scale2, shift2):
    l, d = x.shape
    e = w_router_t.shape[0]
    tm = 512
    row = lambda w: pl.BlockSpec((tm, w), lambda i: (i, 0))
    vec = pl.BlockSpec((1, d), lambda i: (0, 0))
    return pl.pallas_call(
        _outproj_kernel,
        out_shape=(jax.ShapeDtypeStruct((l, d), F32), jax.ShapeDtypeStruct((l, d), BF16),
                   jax.ShapeDtypeStruct((l, d // 2), jnp.uint32),
                   jax.ShapeDtypeStruct((e, l), F32)),
        grid=(l // tm,),
        in_specs=[row(d), row(d), pl.BlockSpec((d, d), lambda i: (0, 0)),
                  pl.BlockSpec((e, d), lambda i: (0, 0)), vec, vec, vec, vec, vec],
        out_specs=(row(d), row(d), row(d // 2), pl.BlockSpec((e, tm), lambda i: (0, i))),
        compiler_params=_params(("arbitrary",)),
        name="outproj_ln1_router",
    )(merged, x, w_o, w_router_t, gate1, ln_g, ln_b, scale2, shift2)


ROUTE_TILE = 512
EXPERTS_PER_GROUP = N_EXPERTS // N_EXPERT_GROUPS


def _route_kernel(s_ref, b_ref, tri_ref, idx_ref, w_ref, rank_ref, cnt_ref, carry_ref):
    @pl.when(pl.program_id(0) == 0)
    def _():
        carry_ref[...] = jnp.zeros_like(carry_ref)

    ne, tm = s_ref.shape
    neg_inf = -jnp.inf
    s = s_ref[...]
    sel = s + b_ref[...]
    e_iota = lax.broadcasted_iota(jnp.int32, (ne, tm), 0)

    gs_rows = []
    for g in range(N_EXPERT_GROUPS):
        xg = sel[g * EXPERTS_PER_GROUP:(g + 1) * EXPERTS_PER_GROUP]
        m1 = jnp.max(xg, axis=0, keepdims=True)
        n1 = jnp.sum((xg == m1).astype(F32), axis=0, keepdims=True)
        m2 = jnp.max(jnp.where(xg < m1, xg, neg_inf), axis=0, keepdims=True)
        gs_rows.append(m1 + jnp.where(n1 >= 2.0, m1, m2))
    gs = jnp.concatenate(gs_rows, axis=0)

    g_iota = lax.broadcasted_iota(jnp.int32, gs.shape, 0)
    beaten = jnp.zeros(gs.shape, jnp.int32)
    for g2 in range(N_EXPERT_GROUPS):
        row = gs[g2:g2 + 1]
        beats = (row > gs) | ((row == gs) & (g2 < g_iota))
        beaten = beaten + beats.astype(jnp.int32)
    g_ok = beaten < TOPK_GROUPS
    work = jnp.concatenate(
        [jnp.where(g_ok[g:g + 1], sel[g * EXPERTS_PER_GROUP:(g + 1) * EXPERTS_PER_GROUP], neg_inf)
         for g in range(N_EXPERT_GROUPS)], axis=0)

    idxs, vals = [], []
    chosen = jnp.zeros((ne, tm), F32)
    for _ in range(TOP_K):
        m = jnp.max(work, axis=0, keepdims=True)
        i = jnp.min(jnp.where(work == m, e_iota, ne), axis=0, keepdims=True)
        onehot = e_iota == i
        idxs.append(i)
        vals.append(jnp.sum(jnp.where(onehot, s, 0.0), axis=0, keepdims=True))
        chosen = jnp.where(onehot, 1.0, chosen)
        work = jnp.where(onehot, neg_inf, work)
    wsum = vals[0]
    for v in vals[1:]:
        wsum = wsum + v

    before = jnp.dot(chosen.astype(BF16), tri_ref[...], preferred_element_type=F32) + carry_ref[...]
    ranks = [jnp.sum(jnp.where(e_iota == i, before, 0.0), axis=0, keepdims=True) for i in idxs]
    carry_ref[...] = carry_ref[...] + jnp.sum(chosen, axis=1, keepdims=True)

    idx_ref[...] = jnp.concatenate(idxs, axis=0)
    w_ref[...] = jnp.concatenate([v / wsum * ROUTED_SCALE for v in vals], axis=0)
    rank_ref[...] = jnp.concatenate(ranks, axis=0).astype(jnp.int32)
    cnt_ref[...] = carry_ref[...]


def _route(scores_t, router_bias):
    ne, t = scores_t.shape
    tm = ROUTE_TILE
    tri = jnp.triu(jnp.ones((tm, tm), F32), k=1).astype(BF16)
    tok = pl.BlockSpec((TOP_K, tm), lambda i: (0, i))
    return pl.pallas_call(
        _route_kernel,
        out_shape=(jax.ShapeDtypeStruct((TOP_K, t), jnp.int32), jax.ShapeDtypeStruct((TOP_K, t), F32),
                   jax.ShapeDtypeStruct((TOP_K, t), jnp.int32), jax.ShapeDtypeStruct((ne, 1), F32)),
        grid=(t // tm,),
        in_specs=[pl.BlockSpec((ne, tm), lambda i: (0, i)),
                  pl.BlockSpec((ne, 1), lambda i: (0, 0)),
                  pl.BlockSpec((tm, tm), lambda i: (0, 0))],
        out_specs=(tok, tok, tok, pl.BlockSpec((ne, 1), lambda i: (0, 0))),
        scratch_shapes=[pltpu.VMEM((ne, 1), F32)],
        compiler_params=_params(("arbitrary",)),
        name="moe_route",
    )(scores_t, router_bias.astype(F32).reshape(ne, 1), tri)


DISPATCH_TILE = 256


def _dispatch_kernel(idx_ref, rank_ref, ps_ref, pe_ref, u_ref, xs_ref, zero_ref, sem, zsem):
    tm = u_ref.shape[0]
    ne = ps_ref.shape[0]

    @pl.when(pl.program_id(0) == 0)
    def _():
        zero_ref[...] = jnp.zeros_like(zero_ref)

        def block_copy(start):
            start = pl.multiple_of(start, MOE_BLOCK)
            return pltpu.make_async_copy(zero_ref, xs_ref.at[pl.ds(start, MOE_BLOCK)], zsem)

        def fill(e, carry):
            @pl.when(pe_ref[e] > ps_ref[e])
            def _():
                block_copy(pe_ref[e] - MOE_BLOCK).start()
            return carry

        def fill_done(e, carry):
            @pl.when(pe_ref[e] > ps_ref[e])
            def _():
                block_copy(pe_ref[e] - MOE_BLOCK).wait()
            return carry

        lax.fori_loop(0, ne, fill, 0)
        lax.fori_loop(0, ne, fill_done, 0)

        first_unused = pe_ref[ne - 1] // MOE_BLOCK
        n_blocks = xs_ref.shape[0] // MOE_BLOCK

        def fill_unused(b, carry):
            block_copy(b * MOE_BLOCK).start()
            return carry

        def fill_unused_done(b, carry):
            block_copy(b * MOE_BLOCK).wait()
            return carry

        lax.fori_loop(first_unused, n_blocks, fill_unused, 0)
        lax.fori_loop(first_unused, n_blocks, fill_unused_done, 0)

    def issue(t, carry):
        for k in range(TOP_K):
            p = ps_ref[idx_ref[k, t]] + rank_ref[k, t]
            pltpu.make_async_copy(u_ref.at[pl.ds(t, 1)], xs_ref.at[pl.ds(p, 1)], sem).start()
        return carry

    lax.fori_loop(0, tm, issue, 0)
    for _ in range(TOP_K):
        pltpu.make_async_copy(u_ref, xs_ref.at[pl.ds(0, tm)], sem).wait()


def _dispatch(idx, rank, pad_start, pad_end, u2p, n_rows):
    t, dw = u2p.shape
    tm = DISPATCH_TILE
    smem_tok = pl.BlockSpec((TOP_K, tm), lambda i: (0, i), memory_space=pltpu.SMEM)
    smem_all = pl.BlockSpec(memory_space=pltpu.SMEM)
    return pl.pallas_call(
        _dispatch_kernel,
        out_shape=jax.ShapeDtypeStruct((n_rows, dw), jnp.uint32),
        grid=(t // tm,),
        in_specs=[smem_tok, smem_tok, smem_all, smem_all,
                  pl.BlockSpec((tm, dw), lambda i: (i, 0))],
        out_specs=pl.BlockSpec(memory_space=pl.ANY),
        scratch_shapes=[pltpu.VMEM((MOE_BLOCK, dw), jnp.uint32),
                        pltpu.SemaphoreType.DMA, pltpu.SemaphoreType.DMA],
        compiler_params=_params(("arbitrary",)),
        name="moe_dispatch",
    )(idx, rank, pad_start, pad_end, u2p)


def _unpack_rows(xp):
    lo = lax.bitcast_convert_type(xp << 16, F32).astype(BF16)
    hi = lax.bitcast_convert_type(xp & jnp.uint32(0xFFFF0000), F32).astype(BF16)
    return lo, hi


def _expert_kernel(be_ref, bv_ref, nx_ref, sl_ref, x_ref, win_ref, wout_ref, y_ref,
                   wfi_ref, wfo_ref, wbi_ref, wbo_ref, sem):
    b = pl.program_id(0)
    valid = bv_ref[b] > 0
    e = be_ref[b]
    slot = sl_ref[b]
    new_expert = (b == 0) | (e != be_ref[jnp.maximum(b - 1, 0)])

    def weight_copies(ex, s):
        return (pltpu.make_async_copy(win_ref.at[ex], wfi_ref.at[s], sem.at[s, 0]),
                pltpu.make_async_copy(wout_ref.at[ex], wfo_ref.at[s], sem.at[s, 1]))

    @pl.when(b == 0)
    def _():
        for cp in weight_copies(e, slot):
            cp.start()

    @pl.when(valid & new_expert)
    def _():
        for cp in weight_copies(e, slot):
            cp.wait()

        @pl.when(nx_ref[b] >= 0)
        def _():
            for cp in weight_copies(nx_ref[b], 1 - slot):
                cp.start()

        wbi_ref[...] = wfi_ref[slot].astype(BF16)
        wbo_ref[...] = wfo_ref[slot].astype(BF16)

    @pl.when(valid)
    def _():
        lo, hi = _unpack_rows(x_ref[...])
        half = lo.shape[1]
        h = (jnp.dot(lo, wbi_ref[:half], preferred_element_type=F32)
             + jnp.dot(hi, wbi_ref[half:], preferred_element_type=F32))
        hg, hu = h[:, :EXPERT_FF], h[:, EXPERT_FF:]
        act = (hg * jax.nn.sigmoid(hg) * hu).astype(BF16)
        y_ref[...] = jnp.dot(act, wbo_ref[...], preferred_element_type=F32)

    @pl.when(jnp.logical_not(valid))
    def _():
        y_ref[...] = jnp.zeros_like(y_ref)


def _experts(block_expert, block_valid, block_next, block_slot, x_rows, e_w_in, e_w_out):
    n_rows, dw = x_rows.shape
    nb = n_rows // MOE_BLOCK
    _, d, ff2 = e_w_in.shape
    ff = ff2 // 2
    grid_spec = pltpu.PrefetchScalarGridSpec(
        num_scalar_prefetch=4,
        grid=(nb,),
        in_specs=[pl.BlockSpec((MOE_BLOCK, dw), lambda b, *_: (b, 0)),
                  pl.BlockSpec(memory_space=pl.ANY),
                  pl.BlockSpec(memory_space=pl.ANY)],
        out_specs=pl.BlockSpec((MOE_BLOCK, d), lambda b, *_: (b, 0)),
        scratch_shapes=[pltpu.VMEM((2, d, ff2), F32), pltpu.VMEM((2, ff, d), F32),
                        pltpu.VMEM((d, ff2), BF16), pltpu.VMEM((ff, d), BF16),
                        pltpu.SemaphoreType.DMA((2, 2))],
    )
    return pl.pallas_call(
        _expert_kernel,
        out_shape=jax.ShapeDtypeStruct((n_rows, d), F32),
        grid_spec=grid_spec,
        compiler_params=_params(("arbitrary",)),
        name="routed_experts",
    )(block_expert, block_valid, block_next, block_slot, x_rows, e_w_in, e_w_out)


COMBINE_TILE = 128


def _final_kernel(idx_ref, rank_ref, ps_ref, u2_ref, x1_ref, w_ref, ys_ref, win_ref, wout_ref,
                  g2_ref, lg_ref, lb_ref, o_ref, buf_ref, sem):
    tm = u2_ref.shape[0]

    def issue(t, carry):
        for k in range(TOP_K):
            p = ps_ref[idx_ref[k, t]] + rank_ref[k, t]
            pltpu.make_async_copy(ys_ref.at[pl.ds(p, 1)], buf_ref.at[k, pl.ds(t, 1)], sem).start()
        return carry

    lax.fori_loop(0, tm, issue, 0)

    h = jnp.dot(u2_ref[...], win_ref[...], preferred_element_type=F32)
    hg, hu = h[:, :SHARED_FF], h[:, SHARED_FF:]
    act = (hg * jax.nn.sigmoid(hg) * hu).astype(BF16)
    ffn = jnp.dot(act, wout_ref[...], preferred_element_type=F32)

    for k in range(TOP_K):
        pltpu.make_async_copy(ys_ref.at[pl.ds(0, tm)], buf_ref.at[k], sem).wait()

    w = w_ref[...]
    for k in range(TOP_K):
        ffn = ffn + w[:, k:k + 1] * buf_ref[k]
    o_ref[...] = _layer_norm(DEEPNORM_ALPHA * x1_ref[...] + g2_ref[...] * ffn, lg_ref[...], lb_ref[...])


def _final(idx, rank, pad_start, u2, x1, w_tok, y_rows, s_w_in, s_w_out, gate2, ln_g, ln_b):
    l, d = x1.shape
    tm = COMBINE_TILE
    row = pl.BlockSpec((tm, d), lambda i: (i, 0))
    vec = pl.BlockSpec((1, d), lambda i: (0, 0))
    smem_tok = pl.BlockSpec((TOP_K, tm), lambda i: (0, i), memory_space=pltpu.SMEM)
    return pl.pallas_call(
        _final_kernel,
        out_shape=jax.ShapeDtypeStruct((l, d), F32),
        grid=(l // tm,),
        in_specs=[smem_tok, smem_tok, pl.BlockSpec(memory_space=pltpu.SMEM),
                  row, row, pl.BlockSpec((tm, TOP_K), lambda i: (i, 0)),
                  pl.BlockSpec(memory_space=pl.ANY),
                  pl.BlockSpec(s_w_in.shape, lambda i: (0, 0)),
                  pl.BlockSpec(s_w_out.shape, lambda i: (0, 0)), vec, vec, vec],
        out_specs=row,
        scratch_shapes=[pltpu.VMEM((TOP_K, tm, d), F32), pltpu.SemaphoreType.DMA],
        compiler_params=_params(("arbitrary",)),
        name="combine_shared_ln2",
    )(idx, rank, pad_start, u2, x1, w_tok, y_rows, s_w_in, s_w_out, gate2, ln_g, ln_b)


def _block_layout(counts, n_tokens):
    padded = (counts + MOE_BLOCK - 1) // MOE_BLOCK * MOE_BLOCK
    pad_ends = jnp.cumsum(padded)
    pad_starts = (pad_ends - padded).astype(jnp.int32)
    n_rows = -(-(n_tokens * TOP_K + N_EXPERTS * (MOE_BLOCK - 1)) // MOE_BLOCK) * MOE_BLOCK
    block_start = jnp.arange(n_rows // MOE_BLOCK, dtype=jnp.int32) * MOE_BLOCK
    block_expert = jnp.minimum(jnp.sum((block_start[:, None] >= pad_ends[None, :]).astype(jnp.int32), axis=1),
                               N_EXPERTS - 1).astype(jnp.int32)
    block_valid = (block_start < pad_ends[-1]).astype(jnp.int32)
    ar = jnp.arange(N_EXPERTS, dtype=jnp.int32)
    has = counts > 0
    later = (ar[None, :] > ar[:, None]) & has[None, :]
    next_has = jnp.min(jnp.where(later, ar[None, :], N_EXPERTS), axis=1)
    next_has = jnp.where(next_has >= N_EXPERTS, -1, next_has)
    ordinal = jnp.cumsum(has.astype(jnp.int32)) - 1
    mine = block_expert[:, None] == ar[None, :]
    block_next = jnp.sum(jnp.where(mine, next_has[None, :], 0), axis=1).astype(jnp.int32)
    block_slot = (jnp.sum(jnp.where(mine, ordinal[None, :], 0), axis=1) % 2).astype(jnp.int32)
    return (pad_starts, pad_ends.astype(jnp.int32), block_expert, block_valid, block_next, block_slot,
            n_rows)


def kernel(x, c, w_ada, b_ada, w_in, rel_bias, ssm_a_re, ssm_a_im, ssm_log_dt, ssm_b_re, ssm_b_im, ssm_c_re, ssm_c_im, ssm_d, w_attn_out, w_ssm_glu, w_o, ln1_g, ln1_b, w_router, router_bias, e_w_in, e_w_out, s_w_in, s_w_out, ln2_g, ln2_b):
    bsz, l, d = x.shape
    assert bsz == 1
    xf = x.reshape(l, d)
    i = 0
    mod = _modulation(c, w_ada[i], b_ada[i])
    shift1, scale1, gate1, shift2, scale2, gate2 = [mod[:, k * d:(k + 1) * d] for k in range(6)]

    w_in_b = w_in[i].astype(BF16)
    u = _modulate(xf, scale1, shift1)
    s_in = _matmul(u, w_in_b, QKV_WIDTH, SSM_WIDTH, 512, 512, F32, "in_proj_ssm")

    outs, lses = [], []
    for gi, (window, dilation) in enumerate(DILATED_GROUPS):
        bias_prev, bias_cur = _attn_bias_tables(rel_bias, gi, window, dilation)
        qkv = _inproj_qkv_group(u, w_in_b, gi, dilation)
        o, s = _attention_group(qkv, bias_prev, bias_cur, gi)
        outs.append(o)
        lses.append(s)
    attn = _attn_combine(outs, lses)

    tabs = _ssm_tables(ssm_a_re[i], ssm_a_im[i], ssm_log_dt[i], ssm_b_re[i], ssm_b_im[i],
                       ssm_c_re[i], ssm_c_im[i])
    y_ssm = _ssm(s_in, tabs, ssm_d[i])

    merged = _merge(attn, y_ssm, u, w_attn_out[i].astype(BF16), w_ssm_glu[i].astype(BF16), w_in_b)
    x1, u2, u2p, scores_t = _outproj(merged, xf, w_o[i].astype(BF16), w_router[i].T.astype(BF16), gate1,
                                     ln1_g[i].reshape(1, d), ln1_b[i].reshape(1, d), scale2, shift2)

    idx, w, rank, counts = _route(scores_t, router_bias[i])
    (pad_start, pad_end, block_expert, block_valid, block_next, block_slot,
     n_rows) = _block_layout(counts[:, 0].astype(jnp.int32), l)
    x_rows = _dispatch(idx, rank, pad_start, pad_end, u2p, n_rows)
    y_rows = _experts(block_expert, block_valid, block_next, block_slot, x_rows, e_w_in[i], e_w_out[i])
    out = _final(idx, rank, pad_start, u2, x1, w.T, y_rows, s_w_in[i].astype(BF16),
                 s_w_out[i].astype(BF16), gate2, ln2_g[i].reshape(1, d), ln2_b[i].reshape(1, d))
    return out.reshape(bsz, l, d)
```

```python
import functools
import math

import jax
import jax.numpy as jnp
from jax import lax
from jax.experimental import pallas as pl
from jax.experimental.pallas import tpu as pltpu

F32 = jnp.float32
BF16 = jnp.bfloat16

D_MODEL = 2048
HEAD_DIM = 128
HEADS_PER_GROUP = 4
DILATED_GROUPS = ((128, 1), (512, 4), (2048, 16))
N_ATTN_GROUPS = len(DILATED_GROUPS)
N_ATTN_HEADS = N_ATTN_GROUPS * HEADS_PER_GROUP
ATTN_WIDTH = N_ATTN_HEADS * HEAD_DIM
ATTN_OUT_WIDTH = HEADS_PER_GROUP * HEAD_DIM
ATTN_BLOCK = 128
REL_BUCKETS = 32
REL_MAX_DISTANCE = 2048
SSM_GROUP_CH = 16
SSM_STATE = 64
SSM_WIDTH = 1024
SSM_GROUPS = SSM_WIDTH // SSM_GROUP_CH
A_RE_MAX = -1e-4
QKV_WIDTH = 3 * ATTN_WIDTH
GATE_OFF = QKV_WIDTH + SSM_WIDTH
N_EXPERTS = 256
TOP_K = 8
N_EXPERT_GROUPS = 8
TOPK_GROUPS = 4
EXPERT_FF = 512
SHARED_FF = 512
ROUTED_SCALE = 2.5
MOE_BLOCK = 128
DEPTH = 1
DEEPNORM_ALPHA = (2 * DEPTH) ** 0.25
LN_EPS = 1e-5
NEG_BIG = -1e30
LANES = 128

SSM_CHUNK = 128
SSM_BLK_GROUPS = 16
SSM_NBLK = SSM_GROUPS // SSM_BLK_GROUPS
SSM_BLK_CH = SSM_BLK_GROUPS * SSM_GROUP_CH
SSM_BLK_ST = SSM_BLK_GROUPS * SSM_STATE
SSM_STATES = SSM_GROUPS * SSM_STATE

VMEM_LIMIT = 56 * 1024 * 1024


def _params(sem, vmem=VMEM_LIMIT):
    return pltpu.CompilerParams(dimension_semantics=sem, vmem_limit_bytes=vmem)


def _mod_kernel(c_ref, w_ref, b_ref, o_ref):
    c = c_ref[...]
    cond = c * jax.nn.sigmoid(c)
    o_ref[...] = jnp.dot(cond, w_ref[...], preferred_element_type=F32,
                         precision=lax.Precision.HIGHEST) + b_ref[...]


def _modulation(c, w_ada, b_ada):
    d, n = w_ada.shape
    tn = 1024
    c8 = jnp.broadcast_to(c, (8, d))
    out = pl.pallas_call(
        _mod_kernel,
        out_shape=jax.ShapeDtypeStruct((8, n), F32),
        grid=(n // tn,),
        in_specs=[pl.BlockSpec((8, d), lambda j: (0, 0)),
                  pl.BlockSpec((d, tn), lambda j: (0, j)),
                  pl.BlockSpec((1, tn), lambda j: (0, j))],
        out_specs=pl.BlockSpec((8, tn), lambda j: (0, j)),
        compiler_params=_params(("arbitrary",)),
        name="adaln_mod",
    )(c8, w_ada, b_ada.reshape(1, n))
    return out[0:1]


def _modulate_kernel(x_ref, sc_ref, sh_ref, o_ref):
    o_ref[...] = (x_ref[...] * (1.0 + sc_ref[...]) + sh_ref[...]).astype(o_ref.dtype)


def _modulate(x, scale, shift):
    m, d = x.shape
    tm = 512
    return pl.pallas_call(
        _modulate_kernel,
        out_shape=jax.ShapeDtypeStruct((m, d), BF16),
        grid=(m // tm,),
        in_specs=[pl.BlockSpec((tm, d), lambda i: (i, 0)),
                  pl.BlockSpec((1, d), lambda i: (0, 0)),
                  pl.BlockSpec((1, d), lambda i: (0, 0))],
        out_specs=pl.BlockSpec((tm, d), lambda i: (i, 0)),
        compiler_params=_params(("arbitrary",)),
        name="modulate",
    )(x, scale, shift)


def _mm_kernel(a_ref, w_ref, o_ref):
    o_ref[...] = jnp.dot(a_ref[...], w_ref[...], preferred_element_type=F32).astype(o_ref.dtype)


def _matmul(a, w, col_off, n, tm, tn, out_dtype, name):
    m, k = a.shape
    assert col_off % tn == 0 and n % tn == 0 and m % tm == 0
    off_blocks = col_off // tn
    return pl.pallas_call(
        _mm_kernel,
        out_shape=jax.ShapeDtypeStruct((m, n), out_dtype),
        grid=(n // tn, m // tm),
        in_specs=[pl.BlockSpec((tm, k), lambda j, i: (i, 0)),
                  pl.BlockSpec((k, tn), lambda j, i: (0, j + off_blocks))],
        out_specs=pl.BlockSpec((tm, tn), lambda j, i: (i, j)),
        compiler_params=_params(("arbitrary", "arbitrary")),
        name=name,
    )(a, w)


def _attn_kernel(bp_ref, bc_ref, q_ref, kp_ref, kc_ref, vp_ref, vc_ref, o_ref, lse_ref):
    blk = pl.program_id(1)
    scale = HEAD_DIM ** -0.5
    nt = (((1,), (1,)), ((), ()))
    first = blk == 0
    for h in range(HEADS_PER_GROUP):
        hs = slice(h * HEAD_DIM, (h + 1) * HEAD_DIM)
        q = q_ref[:, hs]
        s_p = lax.dot_general(q, kp_ref[:, hs], nt, preferred_element_type=F32) * scale + bp_ref[h]
        s_c = lax.dot_general(q, kc_ref[:, hs], nt, preferred_element_type=F32) * scale + bc_ref[h]
        s_p = jnp.where(first, NEG_BIG, s_p)
        m = jnp.maximum(jnp.max(s_p, axis=-1, keepdims=True), jnp.max(s_c, axis=-1, keepdims=True))
        p_p = jnp.exp(s_p - m)
        p_c = jnp.exp(s_c - m)
        l = jnp.sum(p_p, axis=-1, keepdims=True) + jnp.sum(p_c, axis=-1, keepdims=True)
        o = (jnp.dot(p_p.astype(BF16), vp_ref[:, hs], preferred_element_type=F32)
             + jnp.dot(p_c.astype(BF16), vc_ref[:, hs], preferred_element_type=F32))
        o_ref[:, hs] = o / l
        lse_ref[:, hs] = jnp.broadcast_to(m + jnp.log(l), (ATTN_BLOCK, HEAD_DIM))


def _inproj_dilated_kernel(a_ref, w_ref, o_ref, acc_ref):
    res = jnp.dot(a_ref[...], w_ref[...], preferred_element_type=F32)
    dilation, rows, tn = o_ref.shape
    if dilation == 1:
        o_ref[0] = res.astype(o_ref.dtype)
        return
    for c in range(tn // LANES):
        acc_ref[c] = res[:, c * LANES:(c + 1) * LANES]
    for r in range(dilation):
        for c in range(tn // LANES):
            o_ref[r, :, c * LANES:(c + 1) * LANES] = (
                acc_ref[c, pl.ds(r, rows, stride=dilation), :].astype(o_ref.dtype))


def _inproj_qkv_group(u, w_in, gi, dilation):
    l, k = u.shape
    tm, tn = 512, ATTN_OUT_WIDTH
    return pl.pallas_call(
        _inproj_dilated_kernel,
        out_shape=jax.ShapeDtypeStruct((dilation, l // dilation, 3 * tn), BF16),
        grid=(3, l // tm),
        in_specs=[pl.BlockSpec((tm, k), lambda j, i: (i, 0)),
                  pl.BlockSpec((k, tn), lambda j, i: (0, j * N_ATTN_GROUPS + gi))],
        out_specs=pl.BlockSpec((dilation, tm // dilation, tn), lambda j, i: (0, i, j)),
        scratch_shapes=[pltpu.VMEM((tn // LANES, tm, LANES), F32)],
        compiler_params=_params(("arbitrary", "arbitrary")),
        name=f"in_proj_qkv_g{gi}",
    )(u, w_in)


def _attention_group(qkv, bias_prev, bias_cur, gi):
    dilation, m, _ = qkv.shape
    nb = m // ATTN_BLOCK

    def cur(cb):
        return pl.BlockSpec((None, ATTN_BLOCK, ATTN_OUT_WIDTH), lambda r, b: (r, b, cb))

    def prev(cb):
        return pl.BlockSpec((None, ATTN_BLOCK, ATTN_OUT_WIDTH),
                            lambda r, b: (r, jnp.maximum(b - 1, 0), cb))

    bias_spec = pl.BlockSpec((HEADS_PER_GROUP, ATTN_BLOCK, ATTN_BLOCK), lambda r, b: (0, 0, 0))
    out_spec = pl.BlockSpec((None, ATTN_BLOCK, ATTN_OUT_WIDTH), lambda r, b: (r, b, 0))
    return pl.pallas_call(
        _attn_kernel,
        out_shape=(jax.ShapeDtypeStruct((dilation, m, ATTN_OUT_WIDTH), F32),
                   jax.ShapeDtypeStruct((dilation, m, ATTN_OUT_WIDTH), F32)),
        grid=(dilation, nb),
        in_specs=[bias_spec, bias_spec, cur(0), prev(1), cur(1), prev(2), cur(2)],
        out_specs=(out_spec, out_spec),
        compiler_params=_params(("arbitrary", "arbitrary")),
        name=f"dilated_attn_g{gi}",
    )(bias_prev, bias_cur, qkv, qkv, qkv, qkv, qkv)


def _t5_bucket(dist):
    max_exact = REL_BUCKETS // 2
    d_f = jnp.maximum(dist, 1).astype(F32)
    large = max_exact + (jnp.log(d_f / max_exact) / math.log(REL_MAX_DISTANCE / max_exact)
                         * (REL_BUCKETS - max_exact)).astype(jnp.int32)
    large = jnp.minimum(large, REL_BUCKETS - 1)
    return jnp.where(dist < max_exact, dist, large)


def _attn_bias_tables(rel_bias, gi, window, dilation):
    n_keys = window // dilation
    bucket = _t5_bucket(jnp.arange(n_keys + 1, dtype=jnp.int32) * dilation)
    hs = slice(gi * HEADS_PER_GROUP, (gi + 1) * HEADS_PER_GROUP)
    hi = lax.Precision.HIGHEST
    pick_bucket = (bucket[:, None] == jnp.arange(REL_BUCKETS)[None, :]).astype(F32)
    by_dist = jnp.dot(pick_bucket, rel_bias[:, hs].astype(F32), precision=hi)
    qi = jnp.arange(ATTN_BLOCK)[:, None]
    kj = jnp.arange(ATTN_BLOCK)[None, :]
    d_prev = qi + ATTN_BLOCK - kj
    d_cur = qi - kj
    tabs = []
    for dist in (d_prev, d_cur):
        ok = (dist >= 0) & (dist <= n_keys)
        pick_dist = (dist[:, :, None] == jnp.arange(n_keys + 1)[None, None, :]).astype(F32)
        vals = jnp.einsum('ijd,dh->hij', pick_dist, by_dist, precision=hi)
        tabs.append(jnp.where(ok[None], vals, NEG_BIG))
    return tabs


ATTN_COMBINE_TILE = 512


def _attn_combine_kernel(o1, o2, o3, l1, l2, l3, out_ref, *scratch):
    def token_order(ref, buf, c):
        dilation, rows, _ = ref.shape
        cs = slice(c * LANES, (c + 1) * LANES)
        if dilation == 1:
            return ref[0, :, cs]
        for r in range(dilation):
            buf[c, pl.ds(r, rows, stride=dilation), :] = ref[r, :, cs]
        return buf[c]

    for c in range(out_ref.shape[1] // LANES):
        v1, v2, v3, a1, a2, a3 = [token_order(ref, buf, c)
                                  for ref, buf in zip((o1, o2, o3, l1, l2, l3), scratch)]
        m = jnp.maximum(jnp.maximum(a1, a2), a3)
        e1, e2, e3 = jnp.exp(a1 - m), jnp.exp(a2 - m), jnp.exp(a3 - m)
        num = e1 * v1 + e2 * v2 + e3 * v3
        out_ref[:, c * LANES:(c + 1) * LANES] = (num / (e1 + e2 + e3)).astype(out_ref.dtype)


def _attn_combine(outs, lses):
    w = outs[0].shape[-1]
    l = outs[0].shape[0] * outs[0].shape[1]
    tm = ATTN_COMBINE_TILE

    def spec(a):
        dilation = a.shape[0]
        return pl.BlockSpec((dilation, tm // dilation, w), lambda i: (0, i, 0))

    return pl.pallas_call(
        _attn_combine_kernel,
        out_shape=jax.ShapeDtypeStruct((l, w), BF16),
        grid=(l // tm,),
        in_specs=[spec(a) for a in (*outs, *lses)],
        out_specs=pl.BlockSpec((tm, w), lambda i: (i, 0)),
        scratch_shapes=[pltpu.VMEM((w // LANES, tm, LANES), F32)] * 6,
        compiler_params=_params(("arbitrary",)),
        name="attn_combine",
    )(*outs, *lses)


def _ssm_kernel(u_ref, bre_ref, bim_ref, cre_ref, cim_ref, enr_ref, eni_ref, epr_ref, epi_ref,
                lamr_ref, lami_ref, d_ref, tri_ref, y_ref, car_ref, cai_ref):
    @pl.when(pl.program_id(0) == 0)
    def _():
        car_ref[...] = jnp.zeros_like(car_ref)
        cai_ref[...] = jnp.zeros_like(cai_ref)

    tc = u_ref.shape[0]
    tri = tri_ref[...]
    for j in range(SSM_NBLK):
        ch = slice(j * SSM_BLK_CH, (j + 1) * SSM_BLK_CH)
        st = slice(j * SSM_BLK_ST, (j + 1) * SSM_BLK_ST)
        u = u_ref[:, ch]
        ub = u.astype(BF16)
        bur = jnp.dot(ub, bre_ref[j], preferred_element_type=F32)
        bui = jnp.dot(ub, bim_ref[j], preferred_element_type=F32)
        enr, eni = enr_ref[:, st], eni_ref[:, st]
        xr = bur * enr - bui * eni
        xi = bur * eni + bui * enr
        pr = jnp.dot(tri, xr.astype(BF16), preferred_element_type=F32)
        pi = jnp.dot(tri, xi.astype(BF16), preferred_element_type=F32)
        cr, ci = car_ref[:, st], cai_ref[:, st]
        lr, li = lamr_ref[:, st], lami_ref[:, st]
        tr = pr + (lr * cr - li * ci)
        ti = pi + (lr * ci + li * cr)
        epr, epi = epr_ref[:, st], epi_ref[:, st]
        sr = epr * tr - epi * ti
        si = epr * ti + epi * tr
        car_ref[:, st] = sr[tc - 1:tc]
        cai_ref[:, st] = si[tc - 1:tc]
        y = (jnp.dot(sr.astype(BF16), cre_ref[j], preferred_element_type=F32)
             - jnp.dot(si.astype(BF16), cim_ref[j], preferred_element_type=F32))
        y_ref[:, ch] = (y + d_ref[:, ch] * u).astype(y_ref.dtype)


def _ssm_tables(a_re, a_im, log_dt, b_re, b_im, c_re, c_im):
    g, p, hc = b_re.shape
    lam_re = jnp.minimum(a_re.astype(F32), A_RE_MAX)
    lam_im = a_im.astype(F32)
    dt = jnp.exp(log_dt.astype(F32))[:, None]
    mag = jnp.exp(lam_re * dt)
    lb_re, lb_im = mag * jnp.cos(lam_im * dt), mag * jnp.sin(lam_im * dt)
    imag = jnp.exp(-lam_re * dt)
    li_re, li_im = imag * jnp.cos(lam_im * dt), -imag * jnp.sin(lam_im * dt)
    den = lam_re * lam_re + lam_im * lam_im
    nr, ni = lb_re - 1.0, lb_im
    f_re = (nr * lam_re + ni * lam_im) / den
    f_im = (ni * lam_re - nr * lam_im) / den
    bb_re = f_re[:, :, None] * b_re - f_im[:, :, None] * b_im
    bb_im = f_re[:, :, None] * b_im + f_im[:, :, None] * b_re

    def powers(pr, pi):
        er, ei = jnp.ones((1, g * p), F32), jnp.zeros((1, g * p), F32)
        pr, pi = pr.reshape(1, g * p), pi.reshape(1, g * p)
        while er.shape[0] < SSM_CHUNK:
            er, ei = (jnp.concatenate([er, er * pr - ei * pi], axis=0),
                      jnp.concatenate([ei, er * pi + ei * pr], axis=0))
            pr, pi = pr * pr - pi * pi, 2.0 * pr * pi
        return er, ei

    epr, epi = powers(lb_re, lb_im)
    enr, eni = powers(li_re, li_im)
    eye = jnp.eye(SSM_BLK_GROUPS, dtype=F32)

    def b_blocks(b):
        b = b.reshape(SSM_NBLK, SSM_BLK_GROUPS, p, hc)
        return jnp.einsum('jgph,gk->jghkp', b, eye).reshape(SSM_NBLK, SSM_BLK_CH, SSM_BLK_ST).astype(BF16)

    def c_blocks(c):
        c = c.astype(F32).reshape(SSM_NBLK, SSM_BLK_GROUPS, hc, p)
        return jnp.einsum('jghp,gk->jkpgh', c, eye).reshape(SSM_NBLK, SSM_BLK_ST, SSM_BLK_CH).astype(BF16)

    return dict(bre=b_blocks(bb_re), bim=b_blocks(bb_im), cre=c_blocks(c_re), cim=c_blocks(c_im),
                enr=enr, eni=eni, epr=epr, epi=epi,
                lamr=lb_re.reshape(1, g * p), lami=lb_im.reshape(1, g * p))


def _ssm(s_in, tabs, d_skip):
    l, w = s_in.shape
    tc = SSM_CHUNK
    tri = jnp.tril(jnp.ones((tc, tc), F32)).astype(BF16)
    full = lambda shape: pl.BlockSpec(shape, lambda c: (0,) * len(shape))
    return pl.pallas_call(
        _ssm_kernel,
        out_shape=jax.ShapeDtypeStruct((l, w), BF16),
        grid=(l // tc,),
        in_specs=[pl.BlockSpec((tc, w), lambda c: (c, 0)),
                  full((SSM_NBLK, SSM_BLK_CH, SSM_BLK_ST)), full((SSM_NBLK, SSM_BLK_CH, SSM_BLK_ST)),
                  full((SSM_NBLK, SSM_BLK_ST, SSM_BLK_CH)), full((SSM_NBLK, SSM_BLK_ST, SSM_BLK_CH)),
                  full((tc, SSM_STATES)), full((tc, SSM_STATES)),
                  full((tc, SSM_STATES)), full((tc, SSM_STATES)),
                  full((1, SSM_STATES)), full((1, SSM_STATES)),
                  full((1, w)), full((tc, tc))],
        out_specs=pl.BlockSpec((tc, w), lambda c: (c, 0)),
        scratch_shapes=[pltpu.VMEM((1, SSM_STATES), F32), pltpu.VMEM((1, SSM_STATES), F32)],
        compiler_params=_params(("arbitrary",)),
        name="s5_ssm",
    )(s_in, tabs['bre'], tabs['bim'], tabs['cre'], tabs['cim'], tabs['enr'], tabs['eni'],
      tabs['epr'], tabs['epi'], tabs['lamr'], tabs['lami'], d_skip.reshape(1, w).astype(F32), tri)


def _merge_kernel(attn_ref, y_ref, u_ref, wa_ref, wg1_ref, wg2_ref, wia_ref, wis_ref, o_ref):
    a = jnp.dot(attn_ref[...], wa_ref[...], preferred_element_type=F32)
    y = y_ref[...]
    s = (jnp.dot(y, wg1_ref[...], preferred_element_type=F32)
         * jax.nn.sigmoid(jnp.dot(y, wg2_ref[...], preferred_element_type=F32)))
    u = u_ref[...]
    ga = jax.nn.sigmoid(jnp.dot(u, wia_ref[...], preferred_element_type=F32))
    gs = jax.nn.sigmoid(jnp.dot(u, wis_ref[...], preferred_element_type=F32))
    o_ref[...] = (ga * a + gs * s).astype(o_ref.dtype)


def _merge(attn, y_ssm, u, w_attn_out, w_ssm_glu, w_in):
    l, d = u.shape
    tm, tn = 512, 512
    nd = d // tn
    go = GATE_OFF // tn
    return pl.pallas_call(
        _merge_kernel,
        out_shape=jax.ShapeDtypeStruct((l, d), BF16),
        grid=(nd, l // tm),
        in_specs=[pl.BlockSpec((tm, ATTN_OUT_WIDTH), lambda j, i: (i, 0)),
                  pl.BlockSpec((tm, SSM_WIDTH), lambda j, i: (i, 0)),
                  pl.BlockSpec((tm, d), lambda j, i: (i, 0)),
                  pl.BlockSpec((ATTN_OUT_WIDTH, tn), lambda j, i: (0, j)),
                  pl.BlockSpec((SSM_WIDTH, tn), lambda j, i: (0, j)),
                  pl.BlockSpec((SSM_WIDTH, tn), lambda j, i: (0, j + nd)),
                  pl.BlockSpec((d, tn), lambda j, i: (0, j + go)),
                  pl.BlockSpec((d, tn), lambda j, i: (0, j + go + nd))],
        out_specs=pl.BlockSpec((tm, tn), lambda j, i: (i, j)),
        compiler_params=_params(("arbitrary", "arbitrary")),
        name="branch_merge",
    )(attn, y_ssm, u, w_attn_out, w_ssm_glu, w_ssm_glu, w_in, w_in)


def _layer_norm(h, g, b):
    mu = jnp.mean(h, axis=-1, keepdims=True)
    c = h - mu
    var = jnp.mean(c * c, axis=-1, keepdims=True)
    return c * lax.rsqrt(var + LN_EPS) * g + b


def _outproj_kernel(mg_ref, x_ref, wo_ref, wr_ref, g1_ref, lg_ref, lb_ref, sc_ref, sh_ref,
                    x1_ref, u2_ref, u2p_ref, sco_ref):
    mix = jnp.dot(mg_ref[...], wo_ref[...], preferred_element_type=F32)
    x1 = _layer_norm(DEEPNORM_ALPHA * x_ref[...] + g1_ref[...] * mix, lg_ref[...], lb_ref[...])
    x1_ref[...] = x1
    u2 = (x1 * (1.0 + sc_ref[...]) + sh_ref[...]).astype(BF16)
    u2_ref[...] = u2
    bits = lax.bitcast_convert_type(u2.astype(F32), jnp.uint32)
    half = bits.shape[1] // 2
    u2p_ref[...] = (bits[:, :half] >> 16) | (bits[:, half:] & jnp.uint32(0xFFFF0000))
    logits = lax.dot_general(wr_ref[...], u2, (((1,), (1,)), ((), ())), preferred_element_type=F32)
    sco_ref[...] = jax.nn.sigmoid(logits)


def _outproj(merged, x, w_o, w_router_t, gate1, ln_g, ln_b, scale2, shift2):
    l, d = x.shape
    e = w_router_t.shape[0]
    tm = 512
    row = lambda w: pl.BlockSpec((tm, w), lambda i: (i, 0))
    vec = pl.BlockSpec((1, d), lambda i: (0, 0))
    return pl.pallas_call(
        _outproj_kernel,
        out_shape=(jax.ShapeDtypeStruct((l, d), F32), jax.ShapeDtypeStruct((l, d), BF16),
                   jax.ShapeDtypeStruct((l, d // 2), jnp.uint32),
                   jax.ShapeDtypeStruct((e, l), F32)),
        grid=(l // tm,),
        in_specs=[row(d), row(d), pl.BlockSpec((d, d), lambda i: (0, 0)),
                  pl.BlockSpec((e, d), lambda i: (0, 0)), vec, vec, vec, vec, vec],
        out_specs=(row(d), row(d), row(d // 2), pl.BlockSpec((e, tm), lambda i: (0, i))),
        compiler_params=_params(("arbitrary",)),
        name="outproj_ln1_router",
    )(merged, x, w_o, w_router_t, gate1, ln_g, ln_b, scale2, shift2)


ROUTE_TILE = 512
EXPERTS_PER_GROUP = N_EXPERTS // N_EXPERT_GROUPS


def _route_kernel(s_ref, b_ref, tri_ref, idx_ref, w_ref, rank_ref, cnt_ref, carry_ref):
    @pl.when(pl.program_id(0) == 0)
    def _():
        carry_ref[...] = jnp.zeros_like(carry_ref)

    ne, tm = s_ref.shape
    neg_inf = -jnp.inf
    s = s_ref[...]
    sel = s + b_ref[...]
    e_iota = lax.broadcasted_iota(jnp.int32, (ne, tm), 0)

    gs_rows = []
    for g in range(N_EXPERT_GROUPS):
        xg = sel[g * EXPERTS_PER_GROUP:(g + 1) * EXPERTS_PER_GROUP]
        m1 = jnp.max(xg, axis=0, keepdims=True)
        n1 = jnp.sum((xg == m1).astype(F32), axis=0, keepdims=True)
        m2 = jnp.max(jnp.where(xg < m1, xg, neg_inf), axis=0, keepdims=True)
        gs_rows.append(m1 + jnp.where(n1 >= 2.0, m1, m2))
    gs = jnp.concatenate(gs_rows, axis=0)

    g_iota = lax.broadcasted_iota(jnp.int32, gs.shape, 0)
    beaten = jnp.zeros(gs.shape, jnp.int32)
    for g2 in range(N_EXPERT_GROUPS):
        row = gs[g2:g2 + 1]
        beats = (row > gs) | ((row == gs) & (g2 < g_iota))
        beaten = beaten + beats.astype(jnp.int32)
    g_ok = beaten < TOPK_GROUPS
    work = jnp.concatenate(
        [jnp.where(g_ok[g:g + 1], sel[g * EXPERTS_PER_GROUP:(g + 1) * EXPERTS_PER_GROUP], neg_inf)
         for g in range(N_EXPERT_GROUPS)], axis=0)

    idxs, vals = [], []
    chosen = jnp.zeros((ne, tm), F32)
    for _ in range(TOP_K):
        m = jnp.max(work, axis=0, keepdims=True)
        i = jnp.min(jnp.where(work == m, e_iota, ne), axis=0, keepdims=True)
        onehot = e_iota == i
        idxs.append(i)
        vals.append(jnp.sum(jnp.where(onehot, s, 0.0), axis=0, keepdims=True))
        chosen = jnp.where(onehot, 1.0, chosen)
        work = jnp.where(onehot, neg_inf, work)
    wsum = vals[0]
    for v in vals[1:]:
        wsum = wsum + v

    before = jnp.dot(chosen.astype(BF16), tri_ref[...], preferred_element_type=F32) + carry_ref[...]
    ranks = [jnp.sum(jnp.where(e_iota == i, before, 0.0), axis=0, keepdims=True) for i in idxs]
    carry_ref[...] = carry_ref[...] + jnp.sum(chosen, axis=1, keepdims=True)

    idx_ref[...] = jnp.concatenate(idxs, axis=0)
    w_ref[...] = jnp.concatenate([v / wsum * ROUTED_SCALE for v in vals], axis=0)
    rank_ref[...] = jnp.concatenate(ranks, axis=0).astype(jnp.int32)
    cnt_ref[...] = carry_ref[...]


def _route(scores_t, router_bias):
    ne, t = scores_t.shape
    tm = ROUTE_TILE
    tri = jnp.triu(jnp.ones((tm, tm), F32), k=1).astype(BF16)
    tok = pl.BlockSpec((TOP_K, tm), lambda i: (0, i))
    return pl.pallas_call(
        _route_kernel,
        out_shape=(jax.ShapeDtypeStruct((TOP_K, t), jnp.int32), jax.ShapeDtypeStruct((TOP_K, t), F32),
                   jax.ShapeDtypeStruct((TOP_K, t), jnp.int32), jax.ShapeDtypeStruct((ne, 1), F32)),
        grid=(t // tm,),
        in_specs=[pl.BlockSpec((ne, tm), lambda i: (0, i)),
                  pl.BlockSpec((ne, 1), lambda i: (0, 0)),
                  pl.BlockSpec((tm, tm), lambda i: (0, 0))],
        out_specs=(tok, tok, tok, pl.BlockSpec((ne, 1), lambda i: (0, 0))),
        scratch_shapes=[pltpu.VMEM((ne, 1), F32)],
        compiler_params=_params(("arbitrary",)),
        name="moe_route",
    )(scores_t, router_bias.astype(F32).reshape(ne, 1), tri)


DISPATCH_TILE = 256


def _dispatch_kernel(idx_ref, rank_ref, ps_ref, pe_ref, u_ref, xs_ref, zero_ref, sem, zsem):
    tm = u_ref.shape[0]
    ne = ps_ref.shape[0]

    @pl.when(pl.program_id(0) == 0)
    def _():
        zero_ref[...] = jnp.zeros_like(zero_ref)

        def block_copy(start):
            start = pl.multiple_of(start, MOE_BLOCK)
            return pltpu.make_async_copy(zero_ref, xs_ref.at[pl.ds(start, MOE_BLOCK)], zsem)

        def fill(e, carry):
            @pl.when(pe_ref[e] > ps_ref[e])
            def _():
                block_copy(pe_ref[e] - MOE_BLOCK).start()
            return carry

        def fill_done(e, carry):
            @pl.when(pe_ref[e] > ps_ref[e])
            def _():
                block_copy(pe_ref[e] - MOE_BLOCK).wait()
            return carry

        lax.fori_loop(0, ne, fill, 0)
        lax.fori_loop(0, ne, fill_done, 0)

        first_unused = pe_ref[ne - 1] // MOE_BLOCK
        n_blocks = xs_ref.shape[0] // MOE_BLOCK

        def fill_unused(b, carry):
            block_copy(b * MOE_BLOCK).start()
            return carry

        def fill_unused_done(b, carry):
            block_copy(b * MOE_BLOCK).wait()
            return carry

        lax.fori_loop(first_unused, n_blocks, fill_unused, 0)
        lax.fori_loop(first_unused, n_blocks, fill_unused_done, 0)

    def issue(t, carry):
        for k in range(TOP_K):
            p = ps_ref[idx_ref[k, t]] + rank_ref[k, t]
            pltpu.make_async_copy(u_ref.at[pl.ds(t, 1)], xs_ref.at[pl.ds(p, 1)], sem).start()
        return carry

    lax.fori_loop(0, tm, issue, 0)
    for _ in range(TOP_K):
        pltpu.make_async_copy(u_ref, xs_ref.at[pl.ds(0, tm)], sem).wait()


def _dispatch(idx, rank, pad_start, pad_end, u2p, n_rows):
    t, dw = u2p.shape
    tm = DISPATCH_TILE
    smem_tok = pl.BlockSpec((TOP_K, tm), lambda i: (0, i), memory_space=pltpu.SMEM)
    smem_all = pl.BlockSpec(memory_space=pltpu.SMEM)
    return pl.pallas_call(
        _dispatch_kernel,
        out_shape=jax.ShapeDtypeStruct((n_rows, dw), jnp.uint32),
        grid=(t // tm,),
        in_specs=[smem_tok, smem_tok, smem_all, smem_all,
                  pl.BlockSpec((tm, dw), lambda i: (i, 0))],
        out_specs=pl.BlockSpec(memory_space=pl.ANY),
        scratch_shapes=[pltpu.VMEM((MOE_BLOCK, dw), jnp.uint32),
                        pltpu.SemaphoreType.DMA, pltpu.SemaphoreType.DMA],
        compiler_params=_params(("arbitrary",)),
        name="moe_dispatch",
    )(idx, rank, pad_start, pad_end, u2p)


def _unpack_rows(xp):
    lo = lax.bitcast_convert_type(xp << 16, F32).astype(BF16)
    hi = lax.bitcast_convert_type(xp & jnp.uint32(0xFFFF0000), F32).astype(BF16)
    return lo, hi


def _pack_rows(v):
    bits = lax.bitcast_convert_type(v.astype(BF16).astype(F32), jnp.uint32)
    half = bits.shape[1] // 2
    return (bits[:, :half] >> 16) | (bits[:, half:] & jnp.uint32(0xFFFF0000))


def _expert_kernel(be_ref, bv_ref, nx_ref, nx2_ref, sl_ref, x_ref, win_ref, wout_ref, y_ref,
                   wfi_ref, wfo_ref, wbi_ref, wbo_ref, sem):
    b = pl.program_id(0)
    valid = bv_ref[b] > 0
    e = be_ref[b]
    slot = sl_ref[b]
    new_expert = (b == 0) | (e != be_ref[jnp.maximum(b - 1, 0)])

    def weight_copies(ex, s):
        return (pltpu.make_async_copy(win_ref.at[ex], wfi_ref.at[s], sem.at[s, 0]),
                pltpu.make_async_copy(wout_ref.at[ex], wfo_ref.at[s], sem.at[s, 1]))

    @pl.when(b == 0)
    def _():
        for cp in weight_copies(e, slot):
            cp.start()

        @pl.when(nx_ref[b] >= 0)
        def _():
            for cp in weight_copies(nx_ref[b], 1 - slot):
                cp.start()

    @pl.when(valid & new_expert)
    def _():
        for cp in weight_copies(e, slot):
            cp.wait()
        wbi_ref[...] = wfi_ref[slot].astype(BF16)
        wbo_ref[...] = wfo_ref[slot].astype(BF16)

        @pl.when(nx2_ref[b] >= 0)
        def _():
            for cp in weight_copies(nx2_ref[b], slot):
                cp.start()

    @pl.when(valid)
    def _():
        lo, hi = _unpack_rows(x_ref[...])
        half = lo.shape[1]
        h = (jnp.dot(lo, wbi_ref[:half], preferred_element_type=F32)
             + jnp.dot(hi, wbi_ref[half:], preferred_element_type=F32))
        hg, hu = h[:, :EXPERT_FF], h[:, EXPERT_FF:]
        act = (hg * jax.nn.sigmoid(hg) * hu).astype(BF16)
        y_ref[...] = _pack_rows(jnp.dot(act, wbo_ref[...], preferred_element_type=F32))

    @pl.when(jnp.logical_not(valid))
    def _():
        y_ref[...] = jnp.zeros_like(y_ref)


def _experts(block_expert, block_valid, block_next, block_next2, block_slot, x_rows, e_w_in, e_w_out):
    n_rows, dw = x_rows.shape
    nb = n_rows // MOE_BLOCK
    _, d, ff2 = e_w_in.shape
    ff = ff2 // 2
    grid_spec = pltpu.PrefetchScalarGridSpec(
        num_scalar_prefetch=5,
        grid=(nb,),
        in_specs=[pl.BlockSpec((MOE_BLOCK, dw), lambda b, *_: (b, 0)),
                  pl.BlockSpec(memory_space=pl.ANY),
                  pl.BlockSpec(memory_space=pl.ANY)],
        out_specs=pl.BlockSpec((MOE_BLOCK, d // 2), lambda b, *_: (b, 0)),
        scratch_shapes=[pltpu.VMEM((2, d, ff2), F32), pltpu.VMEM((2, ff, d), F32),
                        pltpu.VMEM((d, ff2), BF16), pltpu.VMEM((ff, d), BF16),
                        pltpu.SemaphoreType.DMA((2, 2))],
    )
    return pl.pallas_call(
        _expert_kernel,
        out_shape=jax.ShapeDtypeStruct((n_rows, d // 2), jnp.uint32),
        grid_spec=grid_spec,
        compiler_params=_params(("arbitrary",)),
        name="routed_experts",
    )(block_expert, block_valid, block_next, block_next2, block_slot, x_rows, e_w_in, e_w_out)


COMBINE_TILE = 256


def _final_kernel(idx_ref, rank_ref, ps_ref, u2_ref, x1_ref, w_ref, ys_ref, win_ref, wout_ref,
                  g2_ref, lg_ref, lb_ref, o_ref, buf_ref, sem):
    tm = u2_ref.shape[0]

    def issue(t, carry):
        for k in range(TOP_K):
            p = ps_ref[idx_ref[k, t]] + rank_ref[k, t]
            pltpu.make_async_copy(ys_ref.at[pl.ds(p, 1)], buf_ref.at[k, pl.ds(t, 1)], sem).start()
        return carry

    lax.fori_loop(0, tm, issue, 0)

    h = jnp.dot(u2_ref[...], win_ref[...], preferred_element_type=F32)
    hg, hu = h[:, :SHARED_FF], h[:, SHARED_FF:]
    act = (hg * jax.nn.sigmoid(hg) * hu).astype(BF16)
    ffn = jnp.dot(act, wout_ref[...], preferred_element_type=F32)

    for k in range(TOP_K):
        pltpu.make_async_copy(ys_ref.at[pl.ds(0, tm)], buf_ref.at[k], sem).wait()

    w = w_ref[...]
    half = buf_ref.shape[2]
    routed_lo = jnp.zeros((tm, half), F32)
    routed_hi = jnp.zeros((tm, half), F32)
    for k in range(TOP_K):
        packed = buf_ref[k]
        wk = w[:, k:k + 1]
        routed_lo = routed_lo + wk * lax.bitcast_convert_type(packed << 16, F32)
        routed_hi = routed_hi + wk * lax.bitcast_convert_type(packed & jnp.uint32(0xFFFF0000), F32)
    ffn = ffn + jnp.concatenate([routed_lo, routed_hi], axis=1)
    o_ref[...] = _layer_norm(DEEPNORM_ALPHA * x1_ref[...] + g2_ref[...] * ffn, lg_ref[...], lb_ref[...])


def _final(idx, rank, pad_start, u2, x1, w_tok, y_rows, s_w_in, s_w_out, gate2, ln_g, ln_b):
    l, d = x1.shape
    tm = COMBINE_TILE
    row = pl.BlockSpec((tm, d), lambda i: (i, 0))
    vec = pl.BlockSpec((1, d), lambda i: (0, 0))
    smem_tok = pl.BlockSpec((TOP_K, tm), lambda i: (0, i), memory_space=pltpu.SMEM)
    return pl.pallas_call(
        _final_kernel,
        out_shape=jax.ShapeDtypeStruct((l, d), F32),
        grid=(l // tm,),
        in_specs=[smem_tok, smem_tok, pl.BlockSpec(memory_space=pltpu.SMEM),
                  row, row, pl.BlockSpec((tm, TOP_K), lambda i: (i, 0)),
                  pl.BlockSpec(memory_space=pl.ANY),
                  pl.BlockSpec(s_w_in.shape, lambda i: (0, 0)),
                  pl.BlockSpec(s_w_out.shape, lambda i: (0, 0)), vec, vec, vec],
        out_specs=row,
        scratch_shapes=[pltpu.VMEM((TOP_K, tm, d // 2), jnp.uint32), pltpu.SemaphoreType.DMA],
        compiler_params=_params(("arbitrary",)),
        name="combine_shared_ln2",
    )(idx, rank, pad_start, u2, x1, w_tok, y_rows, s_w_in, s_w_out, gate2, ln_g, ln_b)


def _block_layout(counts, n_tokens):
    padded = (counts + MOE_BLOCK - 1) // MOE_BLOCK * MOE_BLOCK
    pad_ends = jnp.cumsum(padded)
    pad_starts = (pad_ends - padded).astype(jnp.int32)
    n_rows = -(-(n_tokens * TOP_K + N_EXPERTS * (MOE_BLOCK - 1)) // MOE_BLOCK) * MOE_BLOCK
    block_start = jnp.arange(n_rows // MOE_BLOCK, dtype=jnp.int32) * MOE_BLOCK
    block_expert = jnp.minimum(jnp.sum((block_start[:, None] >= pad_ends[None, :]).astype(jnp.int32), axis=1),
                               N_EXPERTS - 1).astype(jnp.int32)
    block_valid = (block_start < pad_ends[-1]).astype(jnp.int32)
    ar = jnp.arange(N_EXPERTS, dtype=jnp.int32)
    has = counts > 0
    later = (ar[None, :] > ar[:, None]) & has[None, :]
    next_has = jnp.min(jnp.where(later, ar[None, :], N_EXPERTS), axis=1)
    next_has = jnp.where(next_has >= N_EXPERTS, -1, next_has)
    after = (ar[None, :] == next_has[:, None])
    next2_has = jnp.sum(jnp.where(after, next_has[None, :], 0), axis=1)
    next2_has = jnp.where(next_has < 0, -1, next2_has)
    ordinal = jnp.cumsum(has.astype(jnp.int32)) - 1
    mine = block_expert[:, None] == ar[None, :]
    pick = lambda v: jnp.sum(jnp.where(mine, v[None, :], 0), axis=1).astype(jnp.int32)
    return (pad_starts, pad_ends.astype(jnp.int32), block_expert, block_valid, pick(next_has),
            pick(next2_has), pick(ordinal) % 2, n_rows)


def kernel(x, c, w_ada, b_ada, w_in, rel_bias, ssm_a_re, ssm_a_im, ssm_log_dt, ssm_b_re, ssm_b_im, ssm_c_re, ssm_c_im, ssm_d, w_attn_out, w_ssm_glu, w_o, ln1_g, ln1_b, w_router, router_bias, e_w_in, e_w_out, s_w_in, s_w_out, ln2_g, ln2_b):
    bsz, l, d = x.shape
    assert bsz == 1
    xf = x.reshape(l, d)
    i = 0
    mod = _modulation(c, w_ada[i], b_ada[i])
    shift1, scale1, gate1, shift2, scale2, gate2 = [mod[:, k * d:(k + 1) * d] for k in range(6)]

    w_in_b = w_in[i].astype(BF16)
    u = _modulate(xf, scale1, shift1)
    s_in = _matmul(u, w_in_b, QKV_WIDTH, SSM_WIDTH, 512, 512, F32, "in_proj_ssm")

    outs, lses = [], []
    for gi, (window, dilation) in enumerate(DILATED_GROUPS):
        bias_prev, bias_cur = _attn_bias_tables(rel_bias, gi, window, dilation)
        qkv = _inproj_qkv_group(u, w_in_b, gi, dilation)
        o, s = _attention_group(qkv, bias_prev, bias_cur, gi)
        outs.append(o)
        lses.append(s)
    attn = _attn_combine(outs, lses)

    tabs = _ssm_tables(ssm_a_re[i], ssm_a_im[i], ssm_log_dt[i], ssm_b_re[i], ssm_b_im[i],
                       ssm_c_re[i], ssm_c_im[i])
    y_ssm = _ssm(s_in, tabs, ssm_d[i])

    merged = _merge(attn, y_ssm, u, w_attn_out[i].astype(BF16), w_ssm_glu[i].astype(BF16), w_in_b)
    x1, u2, u2p, scores_t = _outproj(merged, xf, w_o[i].astype(BF16), w_router[i].T.astype(BF16), gate1,
                                     ln1_g[i].reshape(1, d), ln1_b[i].reshape(1, d), scale2, shift2)

    idx, w, rank, counts = _route(scores_t, router_bias[i])
    (pad_start, pad_end, block_expert, block_valid, block_next, block_next2, block_slot,
     n_rows) = _block_layout(counts[:, 0].astype(jnp.int32), l)
    x_rows = _dispatch(idx, rank, pad_start, pad_end, u2p, n_rows)
    y_rows = _experts(block_expert, block_valid, block_next, block_next2, block_slot, x_rows,
                      e_w_in[i], e_w_out[i])
    out = _final(idx, rank, pad_start, u2, x1, w.T, y_rows, s_w_in[i].astype(BF16),
                 s_w_out[i].astype(BF16), gate2, ln2_g[i].reshape(1, d), ln2_b[i].reshape(1, d))
    return out.reshape(bsz, l, d)
```

```python
import functools
import math

import jax
import jax.numpy as jnp
from jax import lax
from jax.experimental import pallas as pl
from jax.experimental.pallas import tpu as pltpu

F32 = jnp.float32
BF16 = jnp.bfloat16

D_MODEL = 2048
HEAD_DIM = 128
HEADS_PER_GROUP = 4
DILATED_GROUPS = ((128, 1), (512, 4), (2048, 16))
N_ATTN_GROUPS = len(DILATED_GROUPS)
N_ATTN_HEADS = N_ATTN_GROUPS * HEADS_PER_GROUP
ATTN_WIDTH = N_ATTN_HEADS * HEAD_DIM
ATTN_OUT_WIDTH = HEADS_PER_GROUP * HEAD_DIM
ATTN_BLOCK = 128
REL_BUCKETS = 32
REL_MAX_DISTANCE = 2048
SSM_GROUP_CH = 16
SSM_STATE = 64
SSM_WIDTH = 1024
SSM_GROUPS = SSM_WIDTH // SSM_GROUP_CH
A_RE_MAX = -1e-4
QKV_WIDTH = 3 * ATTN_WIDTH
GATE_OFF = QKV_WIDTH + SSM_WIDTH
N_EXPERTS = 256
TOP_K = 8
N_EXPERT_GROUPS = 8
TOPK_GROUPS = 4
EXPERT_FF = 512
SHARED_FF = 512
ROUTED_SCALE = 2.5
MOE_BLOCK = 128
DEPTH = 1
DEEPNORM_ALPHA = (2 * DEPTH) ** 0.25
LN_EPS = 1e-5
NEG_BIG = -1e30
LANES = 128

SSM_CHUNK = 128
SSM_BLK_GROUPS = 16
SSM_NBLK = SSM_GROUPS // SSM_BLK_GROUPS
SSM_BLK_CH = SSM_BLK_GROUPS * SSM_GROUP_CH
SSM_BLK_ST = SSM_BLK_GROUPS * SSM_STATE
SSM_STATES = SSM_GROUPS * SSM_STATE

VMEM_LIMIT = 56 * 1024 * 1024


def _params(sem, vmem=VMEM_LIMIT):
    return pltpu.CompilerParams(dimension_semantics=sem, vmem_limit_bytes=vmem)


def _mod_kernel(c_ref, w_ref, b_ref, o_ref):
    c = c_ref[...]
    cond = c * jax.nn.sigmoid(c)
    o_ref[...] = jnp.dot(cond, w_ref[...], preferred_element_type=F32,
                         precision=lax.Precision.HIGHEST) + b_ref[...]


def _modulation(c, w_ada, b_ada):
    d, n = w_ada.shape
    tn = 1024
    c8 = jnp.broadcast_to(c, (8, d))
    out = pl.pallas_call(
        _mod_kernel,
        out_shape=jax.ShapeDtypeStruct((8, n), F32),
        grid=(n // tn,),
        in_specs=[pl.BlockSpec((8, d), lambda j: (0, 0)),
                  pl.BlockSpec((d, tn), lambda j: (0, j)),
                  pl.BlockSpec((1, tn), lambda j: (0, j))],
        out_specs=pl.BlockSpec((8, tn), lambda j: (0, j)),
        compiler_params=_params(("arbitrary",)),
        name="adaln_mod",
    )(c8, w_ada, b_ada.reshape(1, n))
    return out[0:1]


def _modulate_kernel(x_ref, sc_ref, sh_ref, o_ref):
    o_ref[...] = (x_ref[...] * (1.0 + sc_ref[...]) + sh_ref[...]).astype(o_ref.dtype)


def _modulate(x, scale, shift):
    m, d = x.shape
    tm = 512
    return pl.pallas_call(
        _modulate_kernel,
        out_shape=jax.ShapeDtypeStruct((m, d), BF16),
        grid=(m // tm,),
        in_specs=[pl.BlockSpec((tm, d), lambda i: (i, 0)),
                  pl.BlockSpec((1, d), lambda i: (0, 0)),
                  pl.BlockSpec((1, d), lambda i: (0, 0))],
        out_specs=pl.BlockSpec((tm, d), lambda i: (i, 0)),
        compiler_params=_params(("arbitrary",)),
        name="modulate",
    )(x, scale, shift)


def _mm_kernel(a_ref, w_ref, o_ref):
    o_ref[...] = jnp.dot(a_ref[...], w_ref[...], preferred_element_type=F32).astype(o_ref.dtype)


def _matmul(a, w, col_off, n, tm, tn, out_dtype, name):
    m, k = a.shape
    assert col_off % tn == 0 and n % tn == 0 and m % tm == 0
    off_blocks = col_off // tn
    return pl.pallas_call(
        _mm_kernel,
        out_shape=jax.ShapeDtypeStruct((m, n), out_dtype),
        grid=(n // tn, m // tm),
        in_specs=[pl.BlockSpec((tm, k), lambda j, i: (i, 0)),
                  pl.BlockSpec((k, tn), lambda j, i: (0, j + off_blocks))],
        out_specs=pl.BlockSpec((tm, tn), lambda j, i: (i, j)),
        compiler_params=_params(("arbitrary", "arbitrary")),
        name=name,
    )(a, w)


def _attn_kernel(bp_ref, bc_ref, q_ref, kp_ref, kc_ref, vp_ref, vc_ref, o_ref, lse_ref):
    blk = pl.program_id(1)
    scale = HEAD_DIM ** -0.5
    nt = (((1,), (1,)), ((), ()))
    first = blk == 0
    for h in range(HEADS_PER_GROUP):
        hs = slice(h * HEAD_DIM, (h + 1) * HEAD_DIM)
        q = q_ref[:, hs]
        s_p = lax.dot_general(q, kp_ref[:, hs], nt, preferred_element_type=F32) * scale + bp_ref[h]
        s_c = lax.dot_general(q, kc_ref[:, hs], nt, preferred_element_type=F32) * scale + bc_ref[h]
        s_p = jnp.where(first, NEG_BIG, s_p)
        m = jnp.maximum(jnp.max(s_p, axis=-1, keepdims=True), jnp.max(s_c, axis=-1, keepdims=True))
        p_p = jnp.exp(s_p - m)
        p_c = jnp.exp(s_c - m)
        l = jnp.sum(p_p, axis=-1, keepdims=True) + jnp.sum(p_c, axis=-1, keepdims=True)
        o = (jnp.dot(p_p.astype(BF16), vp_ref[:, hs], preferred_element_type=F32)
             + jnp.dot(p_c.astype(BF16), vc_ref[:, hs], preferred_element_type=F32))
        o_ref[:, hs] = o / l
        lse_ref[:, hs] = jnp.broadcast_to(m + jnp.log(l), (ATTN_BLOCK, HEAD_DIM))


def _inproj_dilated_kernel(a_ref, w_ref, o_ref, acc_ref):
    res = jnp.dot(a_ref[...], w_ref[...], preferred_element_type=F32)
    dilation, rows, tn = o_ref.shape
    if dilation == 1:
        o_ref[0] = res.astype(o_ref.dtype)
        return
    for c in range(tn // LANES):
        acc_ref[c] = res[:, c * LANES:(c + 1) * LANES]
    for r in range(dilation):
        for c in range(tn // LANES):
            o_ref[r, :, c * LANES:(c + 1) * LANES] = (
                acc_ref[c, pl.ds(r, rows, stride=dilation), :].astype(o_ref.dtype))


def _inproj_qkv_group(u, w_in, gi, dilation):
    l, k = u.shape
    tm, tn = 512, ATTN_OUT_WIDTH
    return pl.pallas_call(
        _inproj_dilated_kernel,
        out_shape=jax.ShapeDtypeStruct((dilation, l // dilation, 3 * tn), BF16),
        grid=(3, l // tm),
        in_specs=[pl.BlockSpec((tm, k), lambda j, i: (i, 0)),
                  pl.BlockSpec((k, tn), lambda j, i: (0, j * N_ATTN_GROUPS + gi))],
        out_specs=pl.BlockSpec((dilation, tm // dilation, tn), lambda j, i: (0, i, j)),
        scratch_shapes=[pltpu.VMEM((tn // LANES, tm, LANES), F32)],
        compiler_params=_params(("arbitrary", "arbitrary")),
        name=f"in_proj_qkv_g{gi}",
    )(u, w_in)


def _attention_group(qkv, bias_prev, bias_cur, gi):
    dilation, m, _ = qkv.shape
    nb = m // ATTN_BLOCK

    def cur(cb):
        return pl.BlockSpec((None, ATTN_BLOCK, ATTN_OUT_WIDTH), lambda r, b: (r, b, cb))

    def prev(cb):
        return pl.BlockSpec((None, ATTN_BLOCK, ATTN_OUT_WIDTH),
                            lambda r, b: (r, jnp.maximum(b - 1, 0), cb))

    bias_spec = pl.BlockSpec((HEADS_PER_GROUP, ATTN_BLOCK, ATTN_BLOCK), lambda r, b: (0, 0, 0))
    out_spec = pl.BlockSpec((None, ATTN_BLOCK, ATTN_OUT_WIDTH), lambda r, b: (r, b, 0))
    return pl.pallas_call(
        _attn_kernel,
        out_shape=(jax.ShapeDtypeStruct((dilation, m, ATTN_OUT_WIDTH), F32),
                   jax.ShapeDtypeStruct((dilation, m, ATTN_OUT_WIDTH), F32)),
        grid=(dilation, nb),
        in_specs=[bias_spec, bias_spec, cur(0), prev(1), cur(1), prev(2), cur(2)],
        out_specs=(out_spec, out_spec),
        compiler_params=_params(("arbitrary", "arbitrary")),
        name=f"dilated_attn_g{gi}",
    )(bias_prev, bias_cur, qkv, qkv, qkv, qkv, qkv)


def _t5_bucket(dist):
    max_exact = REL_BUCKETS // 2
    d_f = jnp.maximum(dist, 1).astype(F32)
    large = max_exact + (jnp.log(d_f / max_exact) / math.log(REL_MAX_DISTANCE / max_exact)
                         * (REL_BUCKETS - max_exact)).astype(jnp.int32)
    large = jnp.minimum(large, REL_BUCKETS - 1)
    return jnp.where(dist < max_exact, dist, large)


def _attn_bias_tables(rel_bias, gi, window, dilation):
    n_keys = window // dilation
    bucket = _t5_bucket(jnp.arange(n_keys + 1, dtype=jnp.int32) * dilation)
    hs = slice(gi * HEADS_PER_GROUP, (gi + 1) * HEADS_PER_GROUP)
    hi = lax.Precision.HIGHEST
    pick_bucket = (bucket[:, None] == jnp.arange(REL_BUCKETS)[None, :]).astype(F32)
    by_dist = jnp.dot(pick_bucket, rel_bias[:, hs].astype(F32), precision=hi)
    qi = jnp.arange(ATTN_BLOCK)[:, None]
    kj = jnp.arange(ATTN_BLOCK)[None, :]
    d_prev = qi + ATTN_BLOCK - kj
    d_cur = qi - kj
    tabs = []
    for dist in (d_prev, d_cur):
        ok = (dist >= 0) & (dist <= n_keys)
        pick_dist = (dist[:, :, None] == jnp.arange(n_keys + 1)[None, None, :]).astype(F32)
        vals = jnp.einsum('ijd,dh->hij', pick_dist, by_dist, precision=hi)
        tabs.append(jnp.where(ok[None], vals, NEG_BIG))
    return tabs


ATTN_COMBINE_TILE = 512


def _attn_combine_kernel(o1, o2, o3, l1, l2, l3, out_ref, *scratch):
    def token_order(ref, buf, c):
        dilation, rows, _ = ref.shape
        cs = slice(c * LANES, (c + 1) * LANES)
        if dilation == 1:
            return ref[0, :, cs]
        for r in range(dilation):
            buf[c, pl.ds(r, rows, stride=dilation), :] = ref[r, :, cs]
        return buf[c]

    for c in range(out_ref.shape[1] // LANES):
        v1, v2, v3, a1, a2, a3 = [token_order(ref, buf, c)
                                  for ref, buf in zip((o1, o2, o3, l1, l2, l3), scratch)]
        m = jnp.maximum(jnp.maximum(a1, a2), a3)
        e1, e2, e3 = jnp.exp(a1 - m), jnp.exp(a2 - m), jnp.exp(a3 - m)
        num = e1 * v1 + e2 * v2 + e3 * v3
        out_ref[:, c * LANES:(c + 1) * LANES] = (num / (e1 + e2 + e3)).astype(out_ref.dtype)


def _attn_combine(outs, lses):
    w = outs[0].shape[-1]
    l = outs[0].shape[0] * outs[0].shape[1]
    tm = ATTN_COMBINE_TILE

    def spec(a):
        dilation = a.shape[0]
        return pl.BlockSpec((dilation, tm // dilation, w), lambda i: (0, i, 0))

    return pl.pallas_call(
        _attn_combine_kernel,
        out_shape=jax.ShapeDtypeStruct((l, w), BF16),
        grid=(l // tm,),
        in_specs=[spec(a) for a in (*outs, *lses)],
        out_specs=pl.BlockSpec((tm, w), lambda i: (i, 0)),
        scratch_shapes=[pltpu.VMEM((w // LANES, tm, LANES), F32)] * 6,
        compiler_params=_params(("arbitrary",)),
        name="attn_combine",
    )(*outs, *lses)


def _ssm_kernel(u_ref, bre_ref, bim_ref, cre_ref, cim_ref, enr_ref, eni_ref, epr_ref, epi_ref,
                lamr_ref, lami_ref, d_ref, tri_ref, y_ref, car_ref, cai_ref):
    @pl.when(pl.program_id(0) == 0)
    def _():
        car_ref[...] = jnp.zeros_like(car_ref)
        cai_ref[...] = jnp.zeros_like(cai_ref)

    tc = u_ref.shape[0]
    tri = tri_ref[...]
    for j in range(SSM_NBLK):
        ch = slice(j * SSM_BLK_CH, (j + 1) * SSM_BLK_CH)
        st = slice(j * SSM_BLK_ST, (j + 1) * SSM_BLK_ST)
        u = u_ref[:, ch]
        ub = u.astype(BF16)
        bur = jnp.dot(ub, bre_ref[j], preferred_element_type=F32)
        bui = jnp.dot(ub, bim_ref[j], preferred_element_type=F32)
        enr, eni = enr_ref[:, st], eni_ref[:, st]
        xr = bur * enr - bui * eni
        xi = bur * eni + bui * enr
        pr = jnp.dot(tri, xr.astype(BF16), preferred_element_type=F32)
        pi = jnp.dot(tri, xi.astype(BF16), preferred_element_type=F32)
        cr, ci = car_ref[:, st], cai_ref[:, st]
        lr, li = lamr_ref[:, st], lami_ref[:, st]
        tr = pr + (lr * cr - li * ci)
        ti = pi + (lr * ci + li * cr)
        epr, epi = epr_ref[:, st], epi_ref[:, st]
        sr = epr * tr - epi * ti
        si = epr * ti + epi * tr
        car_ref[:, st] = sr[tc - 1:tc]
        cai_ref[:, st] = si[tc - 1:tc]
        y = (jnp.dot(sr.astype(BF16), cre_ref[j], preferred_element_type=F32)
             - jnp.dot(si.astype(BF16), cim_ref[j], preferred_element_type=F32))
        y_ref[:, ch] = (y + d_ref[:, ch] * u).astype(y_ref.dtype)


def _ssm_tables(a_re, a_im, log_dt, b_re, b_im, c_re, c_im):
    g, p, hc = b_re.shape
    lam_re = jnp.minimum(a_re.astype(F32), A_RE_MAX)
    lam_im = a_im.astype(F32)
    dt = jnp.exp(log_dt.astype(F32))[:, None]
    mag = jnp.exp(lam_re * dt)
    lb_re, lb_im = mag * jnp.cos(lam_im * dt), mag * jnp.sin(lam_im * dt)
    imag = jnp.exp(-lam_re * dt)
    li_re, li_im = imag * jnp.cos(lam_im * dt), -imag * jnp.sin(lam_im * dt)
    den = lam_re * lam_re + lam_im * lam_im
    nr, ni = lb_re - 1.0, lb_im
    f_re = (nr * lam_re + ni * lam_im) / den
    f_im = (ni * lam_re - nr * lam_im) / den
    bb_re = f_re[:, :, None] * b_re - f_im[:, :, None] * b_im
    bb_im = f_re[:, :, None] * b_im + f_im[:, :, None] * b_re

    def powers(pr, pi):
        er, ei = jnp.ones((1, g * p), F32), jnp.zeros((1, g * p), F32)
        pr, pi = pr.reshape(1, g * p), pi.reshape(1, g * p)
        while er.shape[0] < SSM_CHUNK:
            er, ei = (jnp.concatenate([er, er * pr - ei * pi], axis=0),
                      jnp.concatenate([ei, er * pi + ei * pr], axis=0))
            pr, pi = pr * pr - pi * pi, 2.0 * pr * pi
        return er, ei

    epr, epi = powers(lb_re, lb_im)
    enr, eni = powers(li_re, li_im)
    eye = jnp.eye(SSM_BLK_GROUPS, dtype=F32)

    def b_blocks(b):
        b = b.reshape(SSM_NBLK, SSM_BLK_GROUPS, p, hc)
        return jnp.einsum('jgph,gk->jghkp', b, eye).reshape(SSM_NBLK, SSM_BLK_CH, SSM_BLK_ST).astype(BF16)

    def c_blocks(c):
        c = c.astype(F32).reshape(SSM_NBLK, SSM_BLK_GROUPS, hc, p)
        return jnp.einsum('jghp,gk->jkpgh', c, eye).reshape(SSM_NBLK, SSM_BLK_ST, SSM_BLK_CH).astype(BF16)

    return dict(bre=b_blocks(bb_re), bim=b_blocks(bb_im), cre=c_blocks(c_re), cim=c_blocks(c_im),
                enr=enr, eni=eni, epr=epr, epi=epi,
                lamr=lb_re.reshape(1, g * p), lami=lb_im.reshape(1, g * p))


def _ssm(s_in, tabs, d_skip):
    l, w = s_in.shape
    tc = SSM_CHUNK
    tri = jnp.tril(jnp.ones((tc, tc), F32)).astype(BF16)
    full = lambda shape: pl.BlockSpec(shape, lambda c: (0,) * len(shape))
    return pl.pallas_call(
        _ssm_kernel,
        out_shape=jax.ShapeDtypeStruct((l, w), BF16),
        grid=(l // tc,),
        in_specs=[pl.BlockSpec((tc, w), lambda c: (c, 0)),
                  full((SSM_NBLK, SSM_BLK_CH, SSM_BLK_ST)), full((SSM_NBLK, SSM_BLK_CH, SSM_BLK_ST)),
                  full((SSM_NBLK, SSM_BLK_ST, SSM_BLK_CH)), full((SSM_NBLK, SSM_BLK_ST, SSM_BLK_CH)),
                  full((tc, SSM_STATES)), full((tc, SSM_STATES)),
                  full((tc, SSM_STATES)), full((tc, SSM_STATES)),
                  full((1, SSM_STATES)), full((1, SSM_STATES)),
                  full((1, w)), full((tc, tc))],
        out_specs=pl.BlockSpec((tc, w), lambda c: (c, 0)),
        scratch_shapes=[pltpu.VMEM((1, SSM_STATES), F32), pltpu.VMEM((1, SSM_STATES), F32)],
        compiler_params=_params(("arbitrary",)),
        name="s5_ssm",
    )(s_in, tabs['bre'], tabs['bim'], tabs['cre'], tabs['cim'], tabs['enr'], tabs['eni'],
      tabs['epr'], tabs['epi'], tabs['lamr'], tabs['lami'], d_skip.reshape(1, w).astype(F32), tri)


def _merge_kernel(attn_ref, y_ref, u_ref, wa_ref, wg1_ref, wg2_ref, wia_ref, wis_ref, o_ref):
    a = jnp.dot(attn_ref[...], wa_ref[...], preferred_element_type=F32)
    y = y_ref[...]
    s = (jnp.dot(y, wg1_ref[...], preferred_element_type=F32)
         * jax.nn.sigmoid(jnp.dot(y, wg2_ref[...], preferred_element_type=F32)))
    u = u_ref[...]
    ga = jax.nn.sigmoid(jnp.dot(u, wia_ref[...], preferred_element_type=F32))
    gs = jax.nn.sigmoid(jnp.dot(u, wis_ref[...], preferred_element_type=F32))
    o_ref[...] = (ga * a + gs * s).astype(o_ref.dtype)


def _merge(attn, y_ssm, u, w_attn_out, w_ssm_glu, w_in):
    l, d = u.shape
    tm, tn = 512, 512
    nd = d // tn
    go = GATE_OFF // tn
    return pl.pallas_call(
        _merge_kernel,
        out_shape=jax.ShapeDtypeStruct((l, d), BF16),
        grid=(nd, l // tm),
        in_specs=[pl.BlockSpec((tm, ATTN_OUT_WIDTH), lambda j, i: (i, 0)),
                  pl.BlockSpec((tm, SSM_WIDTH), lambda j, i: (i, 0)),
                  pl.BlockSpec((tm, d), lambda j, i: (i, 0)),
                  pl.BlockSpec((ATTN_OUT_WIDTH, tn), lambda j, i: (0, j)),
                  pl.BlockSpec((SSM_WIDTH, tn), lambda j, i: (0, j)),
                  pl.BlockSpec((SSM_WIDTH, tn), lambda j, i: (0, j + nd)),
                  pl.BlockSpec((d, tn), lambda j, i: (0, j + go)),
                  pl.BlockSpec((d, tn), lambda j, i: (0, j + go + nd))],
        out_specs=pl.BlockSpec((tm, tn), lambda j, i: (i, j)),
        compiler_params=_params(("arbitrary", "arbitrary")),
        name="branch_merge",
    )(attn, y_ssm, u, w_attn_out, w_ssm_glu, w_ssm_glu, w_in, w_in)


def _layer_norm(h, g, b):
    mu = jnp.mean(h, axis=-1, keepdims=True)
    c = h - mu
    var = jnp.mean(c * c, axis=-1, keepdims=True)
    return c * lax.rsqrt(var + LN_EPS) * g + b


def _outproj_kernel(mg_ref, x_ref, wo_ref, wr_ref, g1_ref, lg_ref, lb_ref, sc_ref, sh_ref,
                    x1_ref, u2_ref, u2p_ref, sco_ref):
    mix = jnp.dot(mg_ref[...], wo_ref[...], preferred_element_type=F32)
    x1 = _layer_norm(DEEPNORM_ALPHA * x_ref[...] + g1_ref[...] * mix, lg_ref[...], lb_ref[...])
    x1_ref[...] = x1
    u2 = (x1 * (1.0 + sc_ref[...]) + sh_ref[...]).astype(BF16)
    u2_ref[...] = u2
    bits = lax.bitcast_convert_type(u2.astype(F32), jnp.uint32)
    half = bits.shape[1] // 2
    u2p_ref[...] = (bits[:, :half] >> 16) | (bits[:, half:] & jnp.uint32(0xFFFF0000))
    logits = lax.dot_general(wr_ref[...], u2, (((1,), (1,)), ((), ())), preferred_element_type=F32)
    sco_ref[...] = jax.nn.sigmoid(logits)


def _outproj(merged, x, w_o, w_router_t, gate1, ln_g, ln_b, scale2, shift2):
    l, d = x.shape
    e = w_router_t.shape[0]
    tm = 512
    row = lambda w: pl.BlockSpec((tm, w), lambda i: (i, 0))
    vec = pl.BlockSpec((1, d), lambda i: (0, 0))
    return pl.pallas_call(
        _outproj_kernel,
        out_shape=(jax.ShapeDtypeStruct((l, d), F32), jax.ShapeDtypeStruct((l, d), BF16),
                   jax.ShapeDtypeStruct((l, d // 2), jnp.uint32),
                   jax.ShapeDtypeStruct((e, l), F32)),
        grid=(l // tm,),
        in_specs=[row(d), row(d), pl.BlockSpec((d, d), lambda i: (0, 0)),
                  pl.BlockSpec((e, d), lambda i: (0, 0)), vec, vec, vec, vec, vec],
        out_specs=(row(d), row(d), row(d // 2), pl.BlockSpec((e, tm), lambda i: (0, i))),
        compiler_params=_params(("arbitrary",)),
        name="outproj_ln1_router",
    )(merged, x, w_o, w_router_t, gate1, ln_g, ln_b, scale2, shift2)


ROUTE_TILE = 512
EXPERTS_PER_GROUP = N_EXPERTS // N_EXPERT_GROUPS


def _route_kernel(s_ref, b_ref, tri_ref, idx_ref, w_ref, rank_ref, cnt_ref, carry_ref):
    @pl.when(pl.program_id(0) == 0)
    def _():
        carry_ref[...] = jnp.zeros_like(carry_ref)

    ne, tm = s_ref.shape
    neg_inf = -jnp.inf
    s = s_ref[...]
    sel = s + b_ref[...]
    e_iota = lax.broadcasted_iota(jnp.int32, (ne, tm), 0)

    gs_rows = []
    for g in range(N_EXPERT_GROUPS):
        xg = sel[g * EXPERTS_PER_GROUP:(g + 1) * EXPERTS_PER_GROUP]
        m1 = jnp.max(xg, axis=0, keepdims=True)
        n1 = jnp.sum((xg == m1).astype(F32), axis=0, keepdims=True)
        m2 = jnp.max(jnp.where(xg < m1, xg, neg_inf), axis=0, keepdims=True)
        gs_rows.append(m1 + jnp.where(n1 >= 2.0, m1, m2))
    gs = jnp.concatenate(gs_rows, axis=0)

    g_iota = lax.broadcasted_iota(jnp.int32, gs.shape, 0)
    beaten = jnp.zeros(gs.shape, jnp.int32)
    for g2 in range(N_EXPERT_GROUPS):
        row = gs[g2:g2 + 1]
        beats = (row > gs) | ((row == gs) & (g2 < g_iota))
        beaten = beaten + beats.astype(jnp.int32)
    g_ok = beaten < TOPK_GROUPS
    work = jnp.concatenate(
        [jnp.where(g_ok[g:g + 1], sel[g * EXPERTS_PER_GROUP:(g + 1) * EXPERTS_PER_GROUP], neg_inf)
         for g in range(N_EXPERT_GROUPS)], axis=0)

    idxs, vals = [], []
    chosen = jnp.zeros((ne, tm), F32)
    for _ in range(TOP_K):
        m = jnp.max(work, axis=0, keepdims=True)
        i = jnp.min(jnp.where(work == m, e_iota, ne), axis=0, keepdims=True)
        onehot = e_iota == i
        idxs.append(i)
        vals.append(jnp.sum(jnp.where(onehot, s, 0.0), axis=0, keepdims=True))
        chosen = jnp.where(onehot, 1.0, chosen)
        work = jnp.where(onehot, neg_inf, work)
    wsum = vals[0]
    for v in vals[1:]:
        wsum = wsum + v

    before = jnp.dot(chosen.astype(BF16), tri_ref[...], preferred_element_type=F32) + carry_ref[...]
    ranks = [jnp.sum(jnp.where(e_iota == i, before, 0.0), axis=0, keepdims=True) for i in idxs]
    carry_ref[...] = carry_ref[...] + jnp.sum(chosen, axis=1, keepdims=True)

    idx_ref[...] = jnp.concatenate(idxs, axis=0)
    w_ref[...] = jnp.concatenate([v / wsum * ROUTED_SCALE for v in vals], axis=0)
    rank_ref[...] = jnp.concatenate(ranks, axis=0).astype(jnp.int32)
    cnt_ref[...] = carry_ref[...]


def _route(scores_t, router_bias):
    ne, t = scores_t.shape
    tm = ROUTE_TILE
    tri = jnp.triu(jnp.ones((tm, tm), F32), k=1).astype(BF16)
    tok = pl.BlockSpec((TOP_K, tm), lambda i: (0, i))
    return pl.pallas_call(
        _route_kernel,
        out_shape=(jax.ShapeDtypeStruct((TOP_K, t), jnp.int32), jax.ShapeDtypeStruct((TOP_K, t), F32),
                   jax.ShapeDtypeStruct((TOP_K, t), jnp.int32), jax.ShapeDtypeStruct((ne, 1), F32)),
        grid=(t // tm,),
        in_specs=[pl.BlockSpec((ne, tm), lambda i: (0, i)),
                  pl.BlockSpec((ne, 1), lambda i: (0, 0)),
                  pl.BlockSpec((tm, tm), lambda i: (0, 0))],
        out_specs=(tok, tok, tok, pl.BlockSpec((ne, 1), lambda i: (0, 0))),
        scratch_shapes=[pltpu.VMEM((ne, 1), F32)],
        compiler_params=_params(("arbitrary",)),
        name="moe_route",
    )(scores_t, router_bias.astype(F32).reshape(ne, 1), tri)


DISPATCH_TILE = 256


def _dispatch_kernel(idx_ref, rank_ref, ps_ref, pe_ref, u_ref, xs_ref, zero_ref, sem, zsem):
    tm = u_ref.shape[0]
    ne = ps_ref.shape[0]

    @pl.when(pl.program_id(0) == 0)
    def _():
        zero_ref[...] = jnp.zeros_like(zero_ref)

        def block_copy(start):
            start = pl.multiple_of(start, MOE_BLOCK)
            return pltpu.make_async_copy(zero_ref, xs_ref.at[pl.ds(start, MOE_BLOCK)], zsem)

        def fill(e, carry):
            @pl.when(pe_ref[e] > ps_ref[e])
            def _():
                block_copy(pe_ref[e] - MOE_BLOCK).start()
            return carry

        def fill_done(e, carry):
            @pl.when(pe_ref[e] > ps_ref[e])
            def _():
                block_copy(pe_ref[e] - MOE_BLOCK).wait()
            return carry

        lax.fori_loop(0, ne, fill, 0)
        lax.fori_loop(0, ne, fill_done, 0)

        first_unused = pe_ref[ne - 1] // MOE_BLOCK
        n_blocks = xs_ref.shape[0] // MOE_BLOCK

        def fill_unused(b, carry):
            block_copy(b * MOE_BLOCK).start()
            return carry

        def fill_unused_done(b, carry):
            block_copy(b * MOE_BLOCK).wait()
            return carry

        lax.fori_loop(first_unused, n_blocks, fill_unused, 0)
        lax.fori_loop(first_unused, n_blocks, fill_unused_done, 0)

    def issue(t, carry):
        for k in range(TOP_K):
            p = ps_ref[idx_ref[k, t]] + rank_ref[k, t]
            pltpu.make_async_copy(u_ref.at[pl.ds(t, 1)], xs_ref.at[pl.ds(p, 1)], sem).start(
                priority=k % 2)
        return carry

    lax.fori_loop(0, tm, issue, 0)
    for _ in range(TOP_K):
        pltpu.make_async_copy(u_ref, xs_ref.at[pl.ds(0, tm)], sem).wait()


def _dispatch(idx, rank, pad_start, pad_end, u2p, n_rows):
    t, dw = u2p.shape
    tm = DISPATCH_TILE
    smem_tok = pl.BlockSpec((TOP_K, tm), lambda i: (0, i), memory_space=pltpu.SMEM)
    smem_all = pl.BlockSpec(memory_space=pltpu.SMEM)
    return pl.pallas_call(
        _dispatch_kernel,
        out_shape=jax.ShapeDtypeStruct((n_rows, dw), jnp.uint32),
        grid=(t // tm,),
        in_specs=[smem_tok, smem_tok, smem_all, smem_all,
                  pl.BlockSpec((tm, dw), lambda i: (i, 0))],
        out_specs=pl.BlockSpec(memory_space=pl.ANY),
        scratch_shapes=[pltpu.VMEM((MOE_BLOCK, dw), jnp.uint32),
                        pltpu.SemaphoreType.DMA, pltpu.SemaphoreType.DMA],
        compiler_params=_params(("arbitrary",)),
        name="moe_dispatch",
    )(idx, rank, pad_start, pad_end, u2p)


def _unpack_rows(xp):
    lo = lax.bitcast_convert_type(xp << 16, F32).astype(BF16)
    hi = lax.bitcast_convert_type(xp & jnp.uint32(0xFFFF0000), F32).astype(BF16)
    return lo, hi


WEIGHT_CHUNK_BYTES = 2 * 1024 * 1024


def _pack_rows(v):
    bits = lax.bitcast_convert_type(v.astype(BF16).astype(F32), jnp.uint32)
    half = bits.shape[1] // 2
    return (bits[:, :half] >> 16) | (bits[:, half:] & jnp.uint32(0xFFFF0000))


def _expert_kernel(be_ref, bv_ref, nx_ref, nx2_ref, sl_ref, x_ref, win_ref, wout_ref, y_ref,
                   wfi_ref, wfo_ref, wbi_ref, wbo_ref, sem):
    b = pl.program_id(0)
    valid = bv_ref[b] > 0
    e = be_ref[b]
    slot = sl_ref[b]
    new_expert = (b == 0) | (e != be_ref[jnp.maximum(b - 1, 0)])

    def weight_copies(ex, s):
        copies = []
        for hbm, land, j in ((win_ref, wfi_ref, 0), (wout_ref, wfo_ref, 1)):
            rows = WEIGHT_CHUNK_BYTES // (hbm.shape[2] * 4)
            for c in range(hbm.shape[1] // rows):
                rs = pl.ds(c * rows, rows)
                copies.append(pltpu.make_async_copy(hbm.at[ex, rs], land.at[s, rs], sem.at[s, j]))
        return copies

    def start_weights(ex, s):
        for n, cp in enumerate(weight_copies(ex, s)):
            cp.start(priority=n % 2)

    @pl.when(b == 0)
    def _():
        start_weights(e, slot)

        @pl.when(nx_ref[b] >= 0)
        def _():
            start_weights(nx_ref[b], 1 - slot)

    @pl.when(valid & new_expert)
    def _():
        for cp in weight_copies(e, slot):
            cp.wait()
        wbi_ref[...] = wfi_ref[slot].astype(BF16)
        wbo_ref[...] = wfo_ref[slot].astype(BF16)

        @pl.when(nx2_ref[b] >= 0)
        def _():
            start_weights(nx2_ref[b], slot)

    @pl.when(valid)
    def _():
        lo, hi = _unpack_rows(x_ref[...])
        half = lo.shape[1]
        h = (jnp.dot(lo, wbi_ref[:half], preferred_element_type=F32)
             + jnp.dot(hi, wbi_ref[half:], preferred_element_type=F32))
        hg, hu = h[:, :EXPERT_FF], h[:, EXPERT_FF:]
        act = (hg * jax.nn.sigmoid(hg) * hu).astype(BF16)
        y_ref[...] = _pack_rows(jnp.dot(act, wbo_ref[...], preferred_element_type=F32))

    @pl.when(jnp.logical_not(valid))
    def _():
        y_ref[...] = jnp.zeros_like(y_ref)


def _experts(block_expert, block_valid, block_next, block_next2, block_slot, x_rows, e_w_in, e_w_out):
    n_rows, dw = x_rows.shape
    nb = n_rows // MOE_BLOCK
    _, d, ff2 = e_w_in.shape
    ff = ff2 // 2
    grid_spec = pltpu.PrefetchScalarGridSpec(
        num_scalar_prefetch=5,
        grid=(nb,),
        in_specs=[pl.BlockSpec((MOE_BLOCK, dw), lambda b, *_: (b, 0)),
                  pl.BlockSpec(memory_space=pl.ANY),
                  pl.BlockSpec(memory_space=pl.ANY)],
        out_specs=pl.BlockSpec((MOE_BLOCK, d // 2), lambda b, *_: (b, 0)),
        scratch_shapes=[pltpu.VMEM((2, d, ff2), F32), pltpu.VMEM((2, ff, d), F32),
                        pltpu.VMEM((d, ff2), BF16), pltpu.VMEM((ff, d), BF16),
                        pltpu.SemaphoreType.DMA((2, 2))],
    )
    return pl.pallas_call(
        _expert_kernel,
        out_shape=jax.ShapeDtypeStruct((n_rows, d // 2), jnp.uint32),
        grid_spec=grid_spec,
        compiler_params=_params(("arbitrary",)),
        name="routed_experts",
    )(block_expert, block_valid, block_next, block_next2, block_slot, x_rows, e_w_in, e_w_out)


COMBINE_TILE = 256


def _final_kernel(idx_ref, rank_ref, idxn_ref, rankn_ref, ps_ref, u2_ref, x1_ref, w_ref, ys_ref,
                  win_ref, wout_ref, g2_ref, lg_ref, lb_ref, o_ref, buf_ref, sem):
    tm = u2_ref.shape[0]
    i = pl.program_id(0)
    cur = i % 2

    def gather(idx_r, rank_r, slot):
        def issue(t, carry):
            for k in range(TOP_K):
                p = ps_ref[idx_r[k, t]] + rank_r[k, t]
                pltpu.make_async_copy(ys_ref.at[pl.ds(p, 1)], buf_ref.at[slot, k, pl.ds(t, 1)],
                                      sem.at[slot]).start(priority=k % 2)
            return carry

        lax.fori_loop(0, tm, issue, 0)

    @pl.when(i == 0)
    def _():
        gather(idx_ref, rank_ref, 0)

    @pl.when(i + 1 < pl.num_programs(0))
    def _():
        gather(idxn_ref, rankn_ref, 1 - cur)

    h = jnp.dot(u2_ref[...], win_ref[...], preferred_element_type=F32)
    hg, hu = h[:, :SHARED_FF], h[:, SHARED_FF:]
    act = (hg * jax.nn.sigmoid(hg) * hu).astype(BF16)
    ffn = jnp.dot(act, wout_ref[...], preferred_element_type=F32)

    for k in range(TOP_K):
        pltpu.make_async_copy(ys_ref.at[pl.ds(0, tm)], buf_ref.at[cur, k], sem.at[cur]).wait()

    w = w_ref[...]
    half = buf_ref.shape[3]
    routed_lo = jnp.zeros((tm, half), F32)
    routed_hi = jnp.zeros((tm, half), F32)
    for k in range(TOP_K):
        packed = buf_ref[cur, k]
        wk = w[:, k:k + 1]
        routed_lo = routed_lo + wk * lax.bitcast_convert_type(packed << 16, F32)
        routed_hi = routed_hi + wk * lax.bitcast_convert_type(packed & jnp.uint32(0xFFFF0000), F32)
    ffn = ffn + jnp.concatenate([routed_lo, routed_hi], axis=1)
    o_ref[...] = _layer_norm(DEEPNORM_ALPHA * x1_ref[...] + g2_ref[...] * ffn, lg_ref[...], lb_ref[...])


def _final(idx, rank, pad_start, u2, x1, w_tok, y_rows, s_w_in, s_w_out, gate2, ln_g, ln_b):
    l, d = x1.shape
    tm = COMBINE_TILE
    row = pl.BlockSpec((tm, d), lambda i: (i, 0))
    vec = pl.BlockSpec((1, d), lambda i: (0, 0))
    n_tiles = l // tm
    smem_tok = pl.BlockSpec((TOP_K, tm), lambda i: (0, i), memory_space=pltpu.SMEM)
    smem_next = pl.BlockSpec((TOP_K, tm), lambda i: (0, jnp.minimum(i + 1, n_tiles - 1)),
                             memory_space=pltpu.SMEM)
    return pl.pallas_call(
        _final_kernel,
        out_shape=jax.ShapeDtypeStruct((l, d), F32),
        grid=(n_tiles,),
        in_specs=[smem_tok, smem_tok, smem_next, smem_next, pl.BlockSpec(memory_space=pltpu.SMEM),
                  row, row, pl.BlockSpec((tm, TOP_K), lambda i: (i, 0)),
                  pl.BlockSpec(memory_space=pl.ANY),
                  pl.BlockSpec(s_w_in.shape, lambda i: (0, 0)),
                  pl.BlockSpec(s_w_out.shape, lambda i: (0, 0)), vec, vec, vec],
        out_specs=row,
        scratch_shapes=[pltpu.VMEM((2, TOP_K, tm, d // 2), jnp.uint32), pltpu.SemaphoreType.DMA((2,))],
        compiler_params=_params(("arbitrary",)),
        name="combine_shared_ln2",
    )(idx, rank, idx, rank, pad_start, u2, x1, w_tok, y_rows, s_w_in, s_w_out, gate2, ln_g, ln_b)


def _block_layout(counts, n_tokens):
    padded = (counts + MOE_BLOCK - 1) // MOE_BLOCK * MOE_BLOCK
    pad_ends = jnp.cumsum(padded)
    pad_starts = (pad_ends - padded).astype(jnp.int32)
    n_rows = -(-(n_tokens * TOP_K + N_EXPERTS * (MOE_BLOCK - 1)) // MOE_BLOCK) * MOE_BLOCK
    block_start = jnp.arange(n_rows // MOE_BLOCK, dtype=jnp.int32) * MOE_BLOCK
    block_expert = jnp.minimum(jnp.sum((block_start[:, None] >= pad_ends[None, :]).astype(jnp.int32), axis=1),
                               N_EXPERTS - 1).astype(jnp.int32)
    block_valid = (block_start < pad_ends[-1]).astype(jnp.int32)
    ar = jnp.arange(N_EXPERTS, dtype=jnp.int32)
    has = counts > 0
    later = (ar[None, :] > ar[:, None]) & has[None, :]
    next_has = jnp.min(jnp.where(later, ar[None, :], N_EXPERTS), axis=1)
    next_has = jnp.where(next_has >= N_EXPERTS, -1, next_has)
    after = (ar[None, :] == next_has[:, None])
    next2_has = jnp.sum(jnp.where(after, next_has[None, :], 0), axis=1)
    next2_has = jnp.where(next_has < 0, -1, next2_has)
    ordinal = jnp.cumsum(has.astype(jnp.int32)) - 1
    mine = block_expert[:, None] == ar[None, :]
    pick = lambda v: jnp.sum(jnp.where(mine, v[None, :], 0), axis=1).astype(jnp.int32)
    return (pad_starts, pad_ends.astype(jnp.int32), block_expert, block_valid, pick(next_has),
            pick(next2_has), pick(ordinal) % 2, n_rows)


def kernel(x, c, w_ada, b_ada, w_in, rel_bias, ssm_a_re, ssm_a_im, ssm_log_dt, ssm_b_re, ssm_b_im, ssm_c_re, ssm_c_im, ssm_d, w_attn_out, w_ssm_glu, w_o, ln1_g, ln1_b, w_router, router_bias, e_w_in, e_w_out, s_w_in, s_w_out, ln2_g, ln2_b):
    bsz, l, d = x.shape
    assert bsz == 1
    xf = x.reshape(l, d)
    i = 0
    mod = _modulation(c, w_ada[i], b_ada[i])
    shift1, scale1, gate1, shift2, scale2, gate2 = [mod[:, k * d:(k + 1) * d] for k in range(6)]

    w_in_b = w_in[i].astype(BF16)
    u = _modulate(xf, scale1, shift1)
    s_in = _matmul(u, w_in_b, QKV_WIDTH, SSM_WIDTH, 512, 512, F32, "in_proj_ssm")

    outs, lses = [], []
    for gi, (window, dilation) in enumerate(DILATED_GROUPS):
        bias_prev, bias_cur = _attn_bias_tables(rel_bias, gi, window, dilation)
        qkv = _inproj_qkv_group(u, w_in_b, gi, dilation)
        o, s = _attention_group(qkv, bias_prev, bias_cur, gi)
        outs.append(o)
        lses.append(s)
    attn = _attn_combine(outs, lses)

    tabs = _ssm_tables(ssm_a_re[i], ssm_a_im[i], ssm_log_dt[i], ssm_b_re[i], ssm_b_im[i],
                       ssm_c_re[i], ssm_c_im[i])
    y_ssm = _ssm(s_in, tabs, ssm_d[i])

    merged = _merge(attn, y_ssm, u, w_attn_out[i].astype(BF16), w_ssm_glu[i].astype(BF16), w_in_b)
    x1, u2, u2p, scores_t = _outproj(merged, xf, w_o[i].astype(BF16), w_router[i].T.astype(BF16), gate1,
                                     ln1_g[i].reshape(1, d), ln1_b[i].reshape(1, d), scale2, shift2)

    idx, w, rank, counts = _route(scores_t, router_bias[i])
    (pad_start, pad_end, block_expert, block_valid, block_next, block_next2, block_slot,
     n_rows) = _block_layout(counts[:, 0].astype(jnp.int32), l)
    x_rows = _dispatch(idx, rank, pad_start, pad_end, u2p, n_rows)
    y_rows = _experts(block_expert, block_valid, block_next, block_next2, block_slot, x_rows,
                      e_w_in[i], e_w_out[i])
    out = _final(idx, rank, pad_start, u2, x1, w.T, y_rows, s_w_in[i].astype(BF16),
                 s_w_out[i].astype(BF16), gate2, ln2_g[i].reshape(1, d), ln2_b[i].reshape(1, d))
    return out.reshape(bsz, l, d)
```

```python
import functools
import math

import jax
import jax.numpy as jnp
from jax import lax
from jax.experimental import pallas as pl
from jax.experimental.pallas import tpu as pltpu

F32 = jnp.float32
BF16 = jnp.bfloat16

D_MODEL = 2048
HEAD_DIM = 128
HEADS_PER_GROUP = 4
DILATED_GROUPS = ((128, 1), (512, 4), (2048, 16))
N_ATTN_GROUPS = len(DILATED_GROUPS)
N_ATTN_HEADS = N_ATTN_GROUPS * HEADS_PER_GROUP
ATTN_WIDTH = N_ATTN_HEADS * HEAD_DIM
ATTN_OUT_WIDTH = HEADS_PER_GROUP * HEAD_DIM
ATTN_BLOCK = 128
REL_BUCKETS = 32
REL_MAX_DISTANCE = 2048
SSM_GROUP_CH = 16
SSM_STATE = 64
SSM_WIDTH = 1024
SSM_GROUPS = SSM_WIDTH // SSM_GROUP_CH
A_RE_MAX = -1e-4
QKV_WIDTH = 3 * ATTN_WIDTH
GATE_OFF = QKV_WIDTH + SSM_WIDTH
N_EXPERTS = 256
TOP_K = 8
N_EXPERT_GROUPS = 8
TOPK_GROUPS = 4
EXPERT_FF = 512
SHARED_FF = 512
ROUTED_SCALE = 2.5
MOE_BLOCK = 128
DEPTH = 1
DEEPNORM_ALPHA = (2 * DEPTH) ** 0.25
LN_EPS = 1e-5
NEG_BIG = -1e30
LANES = 128
SUBLANES = 8

SSM_CHUNK = 128
SSM_BLK_GROUPS = 16
SSM_NBLK = SSM_GROUPS // SSM_BLK_GROUPS
SSM_BLK_CH = SSM_BLK_GROUPS * SSM_GROUP_CH
SSM_BLK_ST = SSM_BLK_GROUPS * SSM_STATE
SSM_STATES = SSM_GROUPS * SSM_STATE

VMEM_LIMIT = 56 * 1024 * 1024


def _params(sem, vmem=VMEM_LIMIT):
    return pltpu.CompilerParams(dimension_semantics=sem, vmem_limit_bytes=vmem)


def _mod_kernel(c_ref, w_ref, b_ref, o_ref):
    c = c_ref[...]
    cond = c * jax.nn.sigmoid(c)
    o_ref[...] = jnp.dot(cond, w_ref[...], preferred_element_type=F32,
                         precision=lax.Precision.HIGHEST) + b_ref[...]


def _modulation(c, w_ada, b_ada):
    d, n = w_ada.shape
    tn = 1024
    c8 = jnp.broadcast_to(c, (8, d))
    out = pl.pallas_call(
        _mod_kernel,
        out_shape=jax.ShapeDtypeStruct((8, n), F32),
        grid=(n // tn,),
        in_specs=[pl.BlockSpec((8, d), lambda j: (0, 0)),
                  pl.BlockSpec((d, tn), lambda j: (0, j)),
                  pl.BlockSpec((1, tn), lambda j: (0, j))],
        out_specs=pl.BlockSpec((8, tn), lambda j: (0, j)),
        compiler_params=_params(("arbitrary",)),
        name="adaln_mod",
    )(c8, w_ada, b_ada.reshape(1, n))
    return out[0:1]


def _modulate_kernel(x_ref, sc_ref, sh_ref, o_ref):
    o_ref[...] = (x_ref[...] * (1.0 + sc_ref[...]) + sh_ref[...]).astype(o_ref.dtype)


def _modulate(x, scale, shift):
    m, d = x.shape
    tm = 512
    return pl.pallas_call(
        _modulate_kernel,
        out_shape=jax.ShapeDtypeStruct((m, d), BF16),
        grid=(m // tm,),
        in_specs=[pl.BlockSpec((tm, d), lambda i: (i, 0)),
                  pl.BlockSpec((1, d), lambda i: (0, 0)),
                  pl.BlockSpec((1, d), lambda i: (0, 0))],
        out_specs=pl.BlockSpec((tm, d), lambda i: (i, 0)),
        compiler_params=_params(("arbitrary",)),
        name="modulate",
    )(x, scale, shift)


def _mm_kernel(a_ref, w_ref, o_ref):
    o_ref[...] = jnp.dot(a_ref[...], w_ref[...], preferred_element_type=F32).astype(o_ref.dtype)


def _matmul(a, w, col_off, n, tm, tn, out_dtype, name):
    m, k = a.shape
    assert col_off % tn == 0 and n % tn == 0 and m % tm == 0
    off_blocks = col_off // tn
    return pl.pallas_call(
        _mm_kernel,
        out_shape=jax.ShapeDtypeStruct((m, n), out_dtype),
        grid=(n // tn, m // tm),
        in_specs=[pl.BlockSpec((tm, k), lambda j, i: (i, 0)),
                  pl.BlockSpec((k, tn), lambda j, i: (0, j + off_blocks))],
        out_specs=pl.BlockSpec((tm, tn), lambda j, i: (i, j)),
        compiler_params=_params(("arbitrary", "arbitrary")),
        name=name,
    )(a, w)


def _attn_kernel(bp_ref, bc_ref, q_ref, kp_ref, kc_ref, vp_ref, vc_ref, o_ref, lse_ref):
    blk = pl.program_id(1)
    scale = HEAD_DIM ** -0.5
    nt = (((1,), (1,)), ((), ()))
    first = blk == 0
    for h in range(HEADS_PER_GROUP):
        hs = slice(h * HEAD_DIM, (h + 1) * HEAD_DIM)
        q = q_ref[:, hs]
        s_p = lax.dot_general(q, kp_ref[:, hs], nt, preferred_element_type=F32) * scale + bp_ref[h]
        s_c = lax.dot_general(q, kc_ref[:, hs], nt, preferred_element_type=F32) * scale + bc_ref[h]
        s_p = jnp.where(first, NEG_BIG, s_p)
        m = jnp.maximum(jnp.max(s_p, axis=-1, keepdims=True), jnp.max(s_c, axis=-1, keepdims=True))
        p_p = jnp.exp(s_p - m)
        p_c = jnp.exp(s_c - m)
        l = jnp.sum(p_p, axis=-1, keepdims=True) + jnp.sum(p_c, axis=-1, keepdims=True)
        o = (jnp.dot(p_p.astype(BF16), vp_ref[:, hs], preferred_element_type=F32)
             + jnp.dot(p_c.astype(BF16), vc_ref[:, hs], preferred_element_type=F32))
        o_ref[:, hs] = o / l
        lse_ref[:, hs] = jnp.broadcast_to(m + jnp.log(l), (ATTN_BLOCK, HEAD_DIM))


def _inproj_dilated_kernel(a_ref, w_ref, o_ref, acc_ref):
    res = jnp.dot(a_ref[...], w_ref[...], preferred_element_type=F32)
    dilation, rows, tn = o_ref.shape
    if dilation == 1:
        o_ref[0] = res.astype(o_ref.dtype)
        return
    for c in range(tn // LANES):
        acc_ref[c] = res[:, c * LANES:(c + 1) * LANES]
    for r in range(dilation):
        for c in range(tn // LANES):
            o_ref[r, :, c * LANES:(c + 1) * LANES] = (
                acc_ref[c, pl.ds(r, rows, stride=dilation), :].astype(o_ref.dtype))


def _inproj_qkv_group(u, w_in, gi, dilation):
    l, k = u.shape
    tm, tn = 512, ATTN_OUT_WIDTH
    return pl.pallas_call(
        _inproj_dilated_kernel,
        out_shape=jax.ShapeDtypeStruct((dilation, l // dilation, 3 * tn), BF16),
        grid=(3, l // tm),
        in_specs=[pl.BlockSpec((tm, k), lambda j, i: (i, 0)),
                  pl.BlockSpec((k, tn), lambda j, i: (0, j * N_ATTN_GROUPS + gi))],
        out_specs=pl.BlockSpec((dilation, tm // dilation, tn), lambda j, i: (0, i, j)),
        scratch_shapes=[pltpu.VMEM((tn // LANES, tm, LANES), F32)],
        compiler_params=_params(("arbitrary", "arbitrary")),
        name=f"in_proj_qkv_g{gi}",
    )(u, w_in)


def _attention_group(qkv, bias_prev, bias_cur, gi):
    dilation, m, _ = qkv.shape
    nb = m // ATTN_BLOCK

    def cur(cb):
        return pl.BlockSpec((None, ATTN_BLOCK, ATTN_OUT_WIDTH), lambda r, b: (r, b, cb))

    def prev(cb):
        return pl.BlockSpec((None, ATTN_BLOCK, ATTN_OUT_WIDTH),
                            lambda r, b: (r, jnp.maximum(b - 1, 0), cb))

    bias_spec = pl.BlockSpec((HEADS_PER_GROUP, ATTN_BLOCK, ATTN_BLOCK), lambda r, b: (0, 0, 0))
    out_spec = pl.BlockSpec((None, ATTN_BLOCK, ATTN_OUT_WIDTH), lambda r, b: (r, b, 0))
    return pl.pallas_call(
        _attn_kernel,
        out_shape=(jax.ShapeDtypeStruct((dilation, m, ATTN_OUT_WIDTH), F32),
                   jax.ShapeDtypeStruct((dilation, m, ATTN_OUT_WIDTH), F32)),
        grid=(dilation, nb),
        in_specs=[bias_spec, bias_spec, cur(0), prev(1), cur(1), prev(2), cur(2)],
        out_specs=(out_spec, out_spec),
        compiler_params=_params(("arbitrary", "arbitrary")),
        name=f"dilated_attn_g{gi}",
    )(bias_prev, bias_cur, qkv, qkv, qkv, qkv, qkv)


def _t5_bucket(dist):
    max_exact = REL_BUCKETS // 2
    d_f = jnp.maximum(dist, 1).astype(F32)
    large = max_exact + (jnp.log(d_f / max_exact) / math.log(REL_MAX_DISTANCE / max_exact)
                         * (REL_BUCKETS - max_exact)).astype(jnp.int32)
    large = jnp.minimum(large, REL_BUCKETS - 1)
    return jnp.where(dist < max_exact, dist, large)


def _attn_bias_tables(rel_bias, gi, window, dilation):
    n_keys = window // dilation
    bucket = _t5_bucket(jnp.arange(n_keys + 1, dtype=jnp.int32) * dilation)
    hs = slice(gi * HEADS_PER_GROUP, (gi + 1) * HEADS_PER_GROUP)
    hi = lax.Precision.HIGHEST
    pick_bucket = (bucket[:, None] == jnp.arange(REL_BUCKETS)[None, :]).astype(F32)
    by_dist = jnp.dot(pick_bucket, rel_bias[:, hs].astype(F32), precision=hi)
    qi = jnp.arange(ATTN_BLOCK)[:, None]
    kj = jnp.arange(ATTN_BLOCK)[None, :]
    d_prev = qi + ATTN_BLOCK - kj
    d_cur = qi - kj
    tabs = []
    for dist in (d_prev, d_cur):
        ok = (dist >= 0) & (dist <= n_keys)
        pick_dist = (dist[:, :, None] == jnp.arange(n_keys + 1)[None, None, :]).astype(F32)
        vals = jnp.einsum('ijd,dh->hij', pick_dist, by_dist, precision=hi)
        tabs.append(jnp.where(ok[None], vals, NEG_BIG))
    return tabs


ATTN_COMBINE_TILE = 512


def _attn_combine_kernel(o1, o2, o3, l1, l2, l3, out_ref, *scratch):
    def token_order(ref, buf, c):
        dilation, rows, _ = ref.shape
        cs = slice(c * LANES, (c + 1) * LANES)
        if dilation == 1:
            return ref[0, :, cs]
        for r in range(dilation):
            buf[c, pl.ds(r, rows, stride=dilation), :] = ref[r, :, cs]
        return buf[c]

    for c in range(out_ref.shape[1] // LANES):
        v1, v2, v3, a1, a2, a3 = [token_order(ref, buf, c)
                                  for ref, buf in zip((o1, o2, o3, l1, l2, l3), scratch)]
        m = jnp.maximum(jnp.maximum(a1, a2), a3)
        e1, e2, e3 = jnp.exp(a1 - m), jnp.exp(a2 - m), jnp.exp(a3 - m)
        num = e1 * v1 + e2 * v2 + e3 * v3
        out_ref[:, c * LANES:(c + 1) * LANES] = (num / (e1 + e2 + e3)).astype(out_ref.dtype)


def _attn_combine(outs, lses):
    w = outs[0].shape[-1]
    l = outs[0].shape[0] * outs[0].shape[1]
    tm = ATTN_COMBINE_TILE

    def spec(a):
        dilation = a.shape[0]
        return pl.BlockSpec((dilation, tm // dilation, w), lambda i: (0, i, 0))

    return pl.pallas_call(
        _attn_combine_kernel,
        out_shape=jax.ShapeDtypeStruct((l, w), BF16),
        grid=(l // tm,),
        in_specs=[spec(a) for a in (*outs, *lses)],
        out_specs=pl.BlockSpec((tm, w), lambda i: (i, 0)),
        scratch_shapes=[pltpu.VMEM((w // LANES, tm, LANES), F32)] * 6,
        compiler_params=_params(("arbitrary",)),
        name="attn_combine",
    )(*outs, *lses)


def _ssm_kernel(u_ref, bre_ref, bim_ref, cre_ref, cim_ref, enr_ref, eni_ref, epr_ref, epi_ref,
                lamr_ref, lami_ref, d_ref, tri_ref, y_ref, car_ref, cai_ref):
    @pl.when(pl.program_id(0) == 0)
    def _():
        car_ref[...] = jnp.zeros_like(car_ref)
        cai_ref[...] = jnp.zeros_like(cai_ref)

    tc = u_ref.shape[0]
    tri = tri_ref[...]
    for j in range(SSM_NBLK):
        ch = slice(j * SSM_BLK_CH, (j + 1) * SSM_BLK_CH)
        st = slice(j * SSM_BLK_ST, (j + 1) * SSM_BLK_ST)
        u = u_ref[:, ch]
        ub = u.astype(BF16)
        bur = jnp.dot(ub, bre_ref[j], preferred_element_type=F32)
        bui = jnp.dot(ub, bim_ref[j], preferred_element_type=F32)
        enr, eni = enr_ref[:, st], eni_ref[:, st]
        xr = bur * enr - bui * eni
        xi = bur * eni + bui * enr
        pr = jnp.dot(tri, xr.astype(BF16), preferred_element_type=F32)
        pi = jnp.dot(tri, xi.astype(BF16), preferred_element_type=F32)
        cr, ci = car_ref[:, st], cai_ref[:, st]
        lr, li = lamr_ref[:, st], lami_ref[:, st]
        tr = pr + (lr * cr - li * ci)
        ti = pi + (lr * ci + li * cr)
        epr, epi = epr_ref[:, st], epi_ref[:, st]
        sr = epr * tr - epi * ti
        si = epr * ti + epi * tr
        car_ref[:, st] = sr[tc - 1:tc]
        cai_ref[:, st] = si[tc - 1:tc]
        y = (jnp.dot(sr.astype(BF16), cre_ref[j], preferred_element_type=F32)
             - jnp.dot(si.astype(BF16), cim_ref[j], preferred_element_type=F32))
        y_ref[:, ch] = (y + d_ref[:, ch] * u).astype(y_ref.dtype)


def _ssm_tables(a_re, a_im, log_dt, b_re, b_im, c_re, c_im):
    g, p, hc = b_re.shape
    lam_re = jnp.minimum(a_re.astype(F32), A_RE_MAX)
    lam_im = a_im.astype(F32)
    dt = jnp.exp(log_dt.astype(F32))[:, None]
    mag = jnp.exp(lam_re * dt)
    lb_re, lb_im = mag * jnp.cos(lam_im * dt), mag * jnp.sin(lam_im * dt)
    imag = jnp.exp(-lam_re * dt)
    li_re, li_im = imag * jnp.cos(lam_im * dt), -imag * jnp.sin(lam_im * dt)
    den = lam_re * lam_re + lam_im * lam_im
    nr, ni = lb_re - 1.0, lb_im
    f_re = (nr * lam_re + ni * lam_im) / den
    f_im = (ni * lam_re - nr * lam_im) / den
    bb_re = f_re[:, :, None] * b_re - f_im[:, :, None] * b_im
    bb_im = f_re[:, :, None] * b_im + f_im[:, :, None] * b_re

    def powers(pr, pi):
        er, ei = jnp.ones((1, g * p), F32), jnp.zeros((1, g * p), F32)
        pr, pi = pr.reshape(1, g * p), pi.reshape(1, g * p)
        while er.shape[0] < SSM_CHUNK:
            er, ei = (jnp.concatenate([er, er * pr - ei * pi], axis=0),
                      jnp.concatenate([ei, er * pi + ei * pr], axis=0))
            pr, pi = pr * pr - pi * pi, 2.0 * pr * pi
        return er, ei

    epr, epi = powers(lb_re, lb_im)
    enr, eni = powers(li_re, li_im)
    eye = jnp.eye(SSM_BLK_GROUPS, dtype=F32)

    def b_blocks(b):
        b = b.reshape(SSM_NBLK, SSM_BLK_GROUPS, p, hc)
        return jnp.einsum('jgph,gk->jghkp', b, eye).reshape(SSM_NBLK, SSM_BLK_CH, SSM_BLK_ST).astype(BF16)

    def c_blocks(c):
        c = c.astype(F32).reshape(SSM_NBLK, SSM_BLK_GROUPS, hc, p)
        return jnp.einsum('jghp,gk->jkpgh', c, eye).reshape(SSM_NBLK, SSM_BLK_ST, SSM_BLK_CH).astype(BF16)

    return dict(bre=b_blocks(bb_re), bim=b_blocks(bb_im), cre=c_blocks(c_re), cim=c_blocks(c_im),
                enr=enr, eni=eni, epr=epr, epi=epi,
                lamr=lb_re.reshape(1, g * p), lami=lb_im.reshape(1, g * p))


def _ssm(s_in, tabs, d_skip):
    l, w = s_in.shape
    tc = SSM_CHUNK
    tri = jnp.tril(jnp.ones((tc, tc), F32)).astype(BF16)
    full = lambda shape: pl.BlockSpec(shape, lambda c: (0,) * len(shape))
    return pl.pallas_call(
        _ssm_kernel,
        out_shape=jax.ShapeDtypeStruct((l, w), BF16),
        grid=(l // tc,),
        in_specs=[pl.BlockSpec((tc, w), lambda c: (c, 0)),
                  full((SSM_NBLK, SSM_BLK_CH, SSM_BLK_ST)), full((SSM_NBLK, SSM_BLK_CH, SSM_BLK_ST)),
                  full((SSM_NBLK, SSM_BLK_ST, SSM_BLK_CH)), full((SSM_NBLK, SSM_BLK_ST, SSM_BLK_CH)),
                  full((tc, SSM_STATES)), full((tc, SSM_STATES)),
                  full((tc, SSM_STATES)), full((tc, SSM_STATES)),
                  full((1, SSM_STATES)), full((1, SSM_STATES)),
                  full((1, w)), full((tc, tc))],
        out_specs=pl.BlockSpec((tc, w), lambda c: (c, 0)),
        scratch_shapes=[pltpu.VMEM((1, SSM_STATES), F32), pltpu.VMEM((1, SSM_STATES), F32)],
        compiler_params=_params(("arbitrary",)),
        name="s5_ssm",
    )(s_in, tabs['bre'], tabs['bim'], tabs['cre'], tabs['cim'], tabs['enr'], tabs['eni'],
      tabs['epr'], tabs['epi'], tabs['lamr'], tabs['lami'], d_skip.reshape(1, w).astype(F32), tri)


def _merge_kernel(attn_ref, y_ref, u_ref, wa_ref, wg1_ref, wg2_ref, wia_ref, wis_ref, o_ref):
    a = jnp.dot(attn_ref[...], wa_ref[...], preferred_element_type=F32)
    y = y_ref[...]
    s = (jnp.dot(y, wg1_ref[...], preferred_element_type=F32)
         * jax.nn.sigmoid(jnp.dot(y, wg2_ref[...], preferred_element_type=F32)))
    u = u_ref[...]
    ga = jax.nn.sigmoid(jnp.dot(u, wia_ref[...], preferred_element_type=F32))
    gs = jax.nn.sigmoid(jnp.dot(u, wis_ref[...], preferred_element_type=F32))
    o_ref[...] = (ga * a + gs * s).astype(o_ref.dtype)


def _merge(attn, y_ssm, u, w_attn_out, w_ssm_glu, w_in):
    l, d = u.shape
    tm, tn = 512, 512
    nd = d // tn
    go = GATE_OFF // tn
    return pl.pallas_call(
        _merge_kernel,
        out_shape=jax.ShapeDtypeStruct((l, d), BF16),
        grid=(nd, l // tm),
        in_specs=[pl.BlockSpec((tm, ATTN_OUT_WIDTH), lambda j, i: (i, 0)),
                  pl.BlockSpec((tm, SSM_WIDTH), lambda j, i: (i, 0)),
                  pl.BlockSpec((tm, d), lambda j, i: (i, 0)),
                  pl.BlockSpec((ATTN_OUT_WIDTH, tn), lambda j, i: (0, j)),
                  pl.BlockSpec((SSM_WIDTH, tn), lambda j, i: (0, j)),
                  pl.BlockSpec((SSM_WIDTH, tn), lambda j, i: (0, j + nd)),
                  pl.BlockSpec((d, tn), lambda j, i: (0, j + go)),
                  pl.BlockSpec((d, tn), lambda j, i: (0, j + go + nd))],
        out_specs=pl.BlockSpec((tm, tn), lambda j, i: (i, j)),
        compiler_params=_params(("arbitrary", "arbitrary")),
        name="branch_merge",
    )(attn, y_ssm, u, w_attn_out, w_ssm_glu, w_ssm_glu, w_in, w_in)


def _layer_norm(h, g, b):
    mu = jnp.mean(h, axis=-1, keepdims=True)
    c = h - mu
    var = jnp.mean(c * c, axis=-1, keepdims=True)
    return c * lax.rsqrt(var + LN_EPS) * g + b


def _pack_rows(v):
    bits = lax.bitcast_convert_type(v.astype(BF16).astype(F32), jnp.uint32)
    half = bits.shape[1] // 2
    return (bits[:, :half] >> 16) | (bits[:, half:] & jnp.uint32(0xFFFF0000))


def _unpack_lo(xp):
    return lax.bitcast_convert_type(xp << 16, F32)


def _unpack_hi(xp):
    return lax.bitcast_convert_type(xp & jnp.uint32(0xFFFF0000), F32)


def _store_tile_rows(ref, packed):
    rows = packed.shape[0]
    for s in range(SUBLANES):
        ref[pl.ds(s, rows, stride=SUBLANES), :] = packed[:, s * LANES:(s + 1) * LANES]


def _load_tile_rows(ref, s):
    return ref[pl.ds(s, ref.shape[0] // SUBLANES, stride=SUBLANES), :]


def _outproj_kernel(mg_ref, x_ref, wo_ref, wr_ref, g1_ref, lg_ref, lb_ref, sc_ref, sh_ref,
                    x1_ref, u2_ref, u2p_ref, sco_ref):
    mix = jnp.dot(mg_ref[...], wo_ref[...], preferred_element_type=F32)
    x1 = _layer_norm(DEEPNORM_ALPHA * x_ref[...] + g1_ref[...] * mix, lg_ref[...], lb_ref[...])
    x1_ref[...] = x1
    u2 = (x1 * (1.0 + sc_ref[...]) + sh_ref[...]).astype(BF16)
    u2_ref[...] = u2
    _store_tile_rows(u2p_ref, _pack_rows(u2))
    logits = lax.dot_general(wr_ref[...], u2, (((1,), (1,)), ((), ())), preferred_element_type=F32)
    sco_ref[...] = jax.nn.sigmoid(logits)


def _outproj(merged, x, w_o, w_router_t, gate1, ln_g, ln_b, scale2, shift2):
    l, d = x.shape
    assert d == 2 * SUBLANES * LANES
    e = w_router_t.shape[0]
    tm = 512
    row = lambda w: pl.BlockSpec((tm, w), lambda i: (i, 0))
    vec = pl.BlockSpec((1, d), lambda i: (0, 0))
    return pl.pallas_call(
        _outproj_kernel,
        out_shape=(jax.ShapeDtypeStruct((l, d), F32), jax.ShapeDtypeStruct((l, d), BF16),
                   jax.ShapeDtypeStruct((l * SUBLANES, LANES), jnp.uint32),
                   jax.ShapeDtypeStruct((e, l), F32)),
        grid=(l // tm,),
        in_specs=[row(d), row(d), pl.BlockSpec((d, d), lambda i: (0, 0)),
                  pl.BlockSpec((e, d), lambda i: (0, 0)), vec, vec, vec, vec, vec],
        out_specs=(row(d), row(d), pl.BlockSpec((tm * SUBLANES, LANES), lambda i: (i, 0)),
                   pl.BlockSpec((e, tm), lambda i: (0, i))),
        compiler_params=_params(("arbitrary",)),
        name="outproj_ln1_router",
    )(merged, x, w_o, w_router_t, gate1, ln_g, ln_b, scale2, shift2)


ROUTE_TILE = 512
EXPERTS_PER_GROUP = N_EXPERTS // N_EXPERT_GROUPS


def _route_kernel(s_ref, b_ref, tri_ref, idx_ref, w_ref, rank_ref, cnt_ref, carry_ref):
    @pl.when(pl.program_id(0) == 0)
    def _():
        carry_ref[...] = jnp.zeros_like(carry_ref)

    ne, tm = s_ref.shape
    neg_inf = -jnp.inf
    s = s_ref[...]
    sel = s + b_ref[...]
    e_iota = lax.broadcasted_iota(jnp.int32, (ne, tm), 0)

    gs_rows = []
    for g in range(N_EXPERT_GROUPS):
        xg = sel[g * EXPERTS_PER_GROUP:(g + 1) * EXPERTS_PER_GROUP]
        m1 = jnp.max(xg, axis=0, keepdims=True)
        n1 = jnp.sum((xg == m1).astype(F32), axis=0, keepdims=True)
        m2 = jnp.max(jnp.where(xg < m1, xg, neg_inf), axis=0, keepdims=True)
        gs_rows.append(m1 + jnp.where(n1 >= 2.0, m1, m2))
    gs = jnp.concatenate(gs_rows, axis=0)

    g_iota = lax.broadcasted_iota(jnp.int32, gs.shape, 0)
    beaten = jnp.zeros(gs.shape, jnp.int32)
    for g2 in range(N_EXPERT_GROUPS):
        row = gs[g2:g2 + 1]
        beats = (row > gs) | ((row == gs) & (g2 < g_iota))
        beaten = beaten + beats.astype(jnp.int32)
    g_ok = beaten < TOPK_GROUPS
    work = jnp.concatenate(
        [jnp.where(g_ok[g:g + 1], sel[g * EXPERTS_PER_GROUP:(g + 1) * EXPERTS_PER_GROUP], neg_inf)
         for g in range(N_EXPERT_GROUPS)], axis=0)

    idxs, vals = [], []
    chosen = jnp.zeros((ne, tm), F32)
    for _ in range(TOP_K):
        m = jnp.max(work, axis=0, keepdims=True)
        i = jnp.min(jnp.where(work == m, e_iota, ne), axis=0, keepdims=True)
        onehot = e_iota == i
        idxs.append(i)
        vals.append(jnp.sum(jnp.where(onehot, s, 0.0), axis=0, keepdims=True))
        chosen = jnp.where(onehot, 1.0, chosen)
        work = jnp.where(onehot, neg_inf, work)
    wsum = vals[0]
    for v in vals[1:]:
        wsum = wsum + v

    before = jnp.dot(chosen.astype(BF16), tri_ref[...], preferred_element_type=F32) + carry_ref[...]
    ranks = [jnp.sum(jnp.where(e_iota == i, before, 0.0), axis=0, keepdims=True) for i in idxs]
    carry_ref[...] = carry_ref[...] + jnp.sum(chosen, axis=1, keepdims=True)

    idx_ref[...] = jnp.concatenate(idxs, axis=0)
    w_ref[...] = jnp.concatenate([v / wsum * ROUTED_SCALE for v in vals], axis=0)
    rank_ref[...] = jnp.concatenate(ranks, axis=0).astype(jnp.int32)
    cnt_ref[...] = carry_ref[...]


def _route(scores_t, router_bias):
    ne, t = scores_t.shape
    tm = ROUTE_TILE
    tri = jnp.triu(jnp.ones((tm, tm), F32), k=1).astype(BF16)
    tok = pl.BlockSpec((TOP_K, tm), lambda i: (0, i))
    return pl.pallas_call(
        _route_kernel,
        out_shape=(jax.ShapeDtypeStruct((TOP_K, t), jnp.int32), jax.ShapeDtypeStruct((TOP_K, t), F32),
                   jax.ShapeDtypeStruct((TOP_K, t), jnp.int32), jax.ShapeDtypeStruct((ne, 1), F32)),
        grid=(t // tm,),
        in_specs=[pl.BlockSpec((ne, tm), lambda i: (0, i)),
                  pl.BlockSpec((ne, 1), lambda i: (0, 0)),
                  pl.BlockSpec((tm, tm), lambda i: (0, 0))],
        out_specs=(tok, tok, tok, pl.BlockSpec((ne, 1), lambda i: (0, 0))),
        scratch_shapes=[pltpu.VMEM((ne, 1), F32)],
        compiler_params=_params(("arbitrary",)),
        name="moe_route",
    )(scores_t, router_bias.astype(F32).reshape(ne, 1), tri)


DISPATCH_TILE = 256


def _dispatch_kernel(idx_ref, rank_ref, ps_ref, pe_ref, u_ref, xs_ref, zero_ref, sem, zsem):
    tm = u_ref.shape[0] // SUBLANES
    ne = ps_ref.shape[0]
    block_words = MOE_BLOCK * SUBLANES

    def tile_row(r):
        return pl.ds(pl.multiple_of(r * SUBLANES, SUBLANES), SUBLANES)

    @pl.when(pl.program_id(0) == 0)
    def _():
        zero_ref[...] = jnp.zeros_like(zero_ref)

        def block_copy(start):
            start = pl.multiple_of(start * SUBLANES, block_words)
            return pltpu.make_async_copy(zero_ref, xs_ref.at[pl.ds(start, block_words)], zsem)

        def fill(e, carry):
            @pl.when(pe_ref[e] > ps_ref[e])
            def _():
                block_copy(pe_ref[e] - MOE_BLOCK).start()
            return carry

        def fill_done(e, carry):
            @pl.when(pe_ref[e] > ps_ref[e])
            def _():
                block_copy(pe_ref[e] - MOE_BLOCK).wait()
            return carry

        lax.fori_loop(0, ne, fill, 0)
        lax.fori_loop(0, ne, fill_done, 0)

        first_unused = pe_ref[ne - 1] // MOE_BLOCK
        n_blocks = xs_ref.shape[0] // block_words

        def fill_unused(b, carry):
            block_copy(b * MOE_BLOCK).start()
            return carry

        def fill_unused_done(b, carry):
            block_copy(b * MOE_BLOCK).wait()
            return carry

        lax.fori_loop(first_unused, n_blocks, fill_unused, 0)
        lax.fori_loop(first_unused, n_blocks, fill_unused_done, 0)

    def issue(t, carry):
        for k in range(TOP_K):
            p = ps_ref[idx_ref[k, t]] + rank_ref[k, t]
            pltpu.make_async_copy(u_ref.at[tile_row(t)], xs_ref.at[tile_row(p)], sem).start(
                priority=k % 2)
        return carry

    lax.fori_loop(0, tm, issue, 0)
    for _ in range(TOP_K):
        pltpu.make_async_copy(u_ref, xs_ref.at[pl.ds(0, tm * SUBLANES)], sem).wait()


def _dispatch(idx, rank, pad_start, pad_end, u2p, n_rows):
    t = u2p.shape[0] // SUBLANES
    tm = DISPATCH_TILE
    smem_tok = pl.BlockSpec((TOP_K, tm), lambda i: (0, i), memory_space=pltpu.SMEM)
    smem_all = pl.BlockSpec(memory_space=pltpu.SMEM)
    return pl.pallas_call(
        _dispatch_kernel,
        out_shape=jax.ShapeDtypeStruct((n_rows * SUBLANES, LANES), jnp.uint32),
        grid=(t // tm,),
        in_specs=[smem_tok, smem_tok, smem_all, smem_all,
                  pl.BlockSpec((tm * SUBLANES, LANES), lambda i: (i, 0))],
        out_specs=pl.BlockSpec(memory_space=pl.ANY),
        scratch_shapes=[pltpu.VMEM((MOE_BLOCK * SUBLANES, LANES), jnp.uint32),
                        pltpu.SemaphoreType.DMA, pltpu.SemaphoreType.DMA],
        compiler_params=_params(("arbitrary",)),
        name="moe_dispatch",
    )(idx, rank, pad_start, pad_end, u2p)


WEIGHT_CHUNK_BYTES = 2 * 1024 * 1024


def _expert_kernel(be_ref, bv_ref, nx_ref, nx2_ref, sl_ref, x_ref, win_ref, wout_ref, y_ref,
                   wfi_ref, wfo_ref, wbi_ref, wbo_ref, sem):
    b = pl.program_id(0)
    valid = bv_ref[b] > 0
    e = be_ref[b]
    slot = sl_ref[b]
    new_expert = (b == 0) | (e != be_ref[jnp.maximum(b - 1, 0)])

    def weight_copies(ex, s):
        copies = []
        for hbm, land, j in ((win_ref, wfi_ref, 0), (wout_ref, wfo_ref, 1)):
            rows = WEIGHT_CHUNK_BYTES // (hbm.shape[2] * 4)
            for c in range(hbm.shape[1] // rows):
                rs = pl.ds(c * rows, rows)
                copies.append(pltpu.make_async_copy(hbm.at[ex, rs], land.at[s, rs], sem.at[s, j]))
        return copies

    def start_weights(ex, s):
        for n, cp in enumerate(weight_copies(ex, s)):
            cp.start(priority=n % 2)

    @pl.when(b == 0)
    def _():
        start_weights(e, slot)

        @pl.when(nx_ref[b] >= 0)
        def _():
            start_weights(nx_ref[b], 1 - slot)

    @pl.when(valid & new_expert)
    def _():
        for cp in weight_copies(e, slot):
            cp.wait()
        for s in range(2):
            @pl.when(slot == s)
            def _():
                wbi_ref[...] = wfi_ref[s].astype(BF16)
                wbo_ref[...] = wfo_ref[s].astype(BF16)

        @pl.when(nx2_ref[b] >= 0)
        def _():
            start_weights(nx2_ref[b], slot)

    @pl.when(valid)
    def _():
        words = [_load_tile_rows(x_ref, s) for s in range(SUBLANES)]
        lo = jnp.concatenate([_unpack_lo(wd).astype(BF16) for wd in words], axis=1)
        hi = jnp.concatenate([_unpack_hi(wd).astype(BF16) for wd in words], axis=1)
        half = lo.shape[1]
        h = (jnp.dot(lo, wbi_ref[:half], preferred_element_type=F32)
             + jnp.dot(hi, wbi_ref[half:], preferred_element_type=F32))
        hg, hu = h[:, :EXPERT_FF], h[:, EXPERT_FF:]
        act = (hg * jax.nn.sigmoid(hg) * hu).astype(BF16)
        _store_tile_rows(y_ref, _pack_rows(jnp.dot(act, wbo_ref[...], preferred_element_type=F32)))

    @pl.when(jnp.logical_not(valid))
    def _():
        y_ref[...] = jnp.zeros_like(y_ref)


def _experts(block_expert, block_valid, block_next, block_next2, block_slot, x_rows, e_w_in, e_w_out):
    n_rows = x_rows.shape[0] // SUBLANES
    nb = n_rows // MOE_BLOCK
    _, d, ff2 = e_w_in.shape
    ff = ff2 // 2
    rows_spec = pl.BlockSpec((MOE_BLOCK * SUBLANES, LANES), lambda b, *_: (b, 0))
    grid_spec = pltpu.PrefetchScalarGridSpec(
        num_scalar_prefetch=5,
        grid=(nb,),
        in_specs=[rows_spec,
                  pl.BlockSpec(memory_space=pl.ANY),
                  pl.BlockSpec(memory_space=pl.ANY)],
        out_specs=rows_spec,
        scratch_shapes=[pltpu.VMEM((2, d, ff2), F32), pltpu.VMEM((2, ff, d), F32),
                        pltpu.VMEM((d, ff2), BF16), pltpu.VMEM((ff, d), BF16),
                        pltpu.SemaphoreType.DMA((2, 2))],
    )
    return pl.pallas_call(
        _expert_kernel,
        out_shape=jax.ShapeDtypeStruct((n_rows * SUBLANES, LANES), jnp.uint32),
        grid_spec=grid_spec,
        compiler_params=_params(("arbitrary",)),
        name="routed_experts",
    )(block_expert, block_valid, block_next, block_next2, block_slot, x_rows, e_w_in, e_w_out)


COMBINE_TILE = 256


def _final_kernel(idx_ref, rank_ref, ps_ref, u2_ref, x1_ref, w_ref, ys_ref,
                  win_ref, wout_ref, g2_ref, lg_ref, lb_ref, o_ref, buf_ref, sem):
    tm = u2_ref.shape[0]

    def tile_row(r):
        return pl.ds(pl.multiple_of(r * SUBLANES, SUBLANES), SUBLANES)

    def issue(t, carry):
        for k in range(TOP_K):
            p = ps_ref[idx_ref[k, t]] + rank_ref[k, t]
            pltpu.make_async_copy(ys_ref.at[tile_row(p)], buf_ref.at[k, tile_row(t)], sem).start(
                priority=k % 2)
        return carry

    lax.fori_loop(0, tm, issue, 0)

    h = jnp.dot(u2_ref[...], win_ref[...], preferred_element_type=F32)
    hg, hu = h[:, :SHARED_FF], h[:, SHARED_FF:]
    act = (hg * jax.nn.sigmoid(hg) * hu).astype(BF16)
    ffn = jnp.dot(act, wout_ref[...], preferred_element_type=F32)

    for k in range(TOP_K):
        pltpu.make_async_copy(ys_ref.at[pl.ds(0, tm * SUBLANES)], buf_ref.at[k], sem).wait()

    w = w_ref[...]
    wk = [w[:, k:k + 1] for k in range(TOP_K)]
    lo_parts, hi_parts = [], []
    for s in range(SUBLANES):
        lo = jnp.zeros((tm, LANES), F32)
        hi = jnp.zeros((tm, LANES), F32)
        for k in range(TOP_K):
            words = _load_tile_rows(buf_ref.at[k], s)
            lo = lo + wk[k] * _unpack_lo(words)
            hi = hi + wk[k] * _unpack_hi(words)
        lo_parts.append(lo)
        hi_parts.append(hi)
    ffn = ffn + jnp.concatenate(lo_parts + hi_parts, axis=1)
    o_ref[...] = _layer_norm(DEEPNORM_ALPHA * x1_ref[...] + g2_ref[...] * ffn, lg_ref[...], lb_ref[...])


def _final(idx, rank, pad_start, u2, x1, w_tok, y_rows, s_w_in, s_w_out, gate2, ln_g, ln_b):
    l, d = x1.shape
    tm = COMBINE_TILE
    row = pl.BlockSpec((tm, d), lambda i: (i, 0))
    vec = pl.BlockSpec((1, d), lambda i: (0, 0))
    smem_tok = pl.BlockSpec((TOP_K, tm), lambda i: (0, i), memory_space=pltpu.SMEM)
    return pl.pallas_call(
        _final_kernel,
        out_shape=jax.ShapeDtypeStruct((l, d), F32),
        grid=(l // tm,),
        in_specs=[smem_tok, smem_tok, pl.BlockSpec(memory_space=pltpu.SMEM),
                  row, row, pl.BlockSpec((tm, TOP_K), lambda i: (i, 0)),
                  pl.BlockSpec(memory_space=pl.ANY),
                  pl.BlockSpec(s_w_in.shape, lambda i: (0, 0)),
                  pl.BlockSpec(s_w_out.shape, lambda i: (0, 0)), vec, vec, vec],
        out_specs=row,
        scratch_shapes=[pltpu.VMEM((TOP_K, tm * SUBLANES, LANES), jnp.uint32),
                        pltpu.SemaphoreType.DMA],
        compiler_params=_params(("arbitrary",)),
        name="combine_shared_ln2",
    )(idx, rank, pad_start, u2, x1, w_tok, y_rows, s_w_in, s_w_out, gate2, ln_g, ln_b)


def _block_layout(counts, n_tokens):
    padded = (counts + MOE_BLOCK - 1) // MOE_BLOCK * MOE_BLOCK
    pad_ends = jnp.cumsum(padded)
    pad_starts = (pad_ends - padded).astype(jnp.int32)
    n_rows = -(-(n_tokens * TOP_K + N_EXPERTS * (MOE_BLOCK - 1)) // MOE_BLOCK) * MOE_BLOCK
    block_start = jnp.arange(n_rows // MOE_BLOCK, dtype=jnp.int32) * MOE_BLOCK
    block_expert = jnp.minimum(jnp.sum((block_start[:, None] >= pad_ends[None, :]).astype(jnp.int32), axis=1),
                               N_EXPERTS - 1).astype(jnp.int32)
    block_valid = (block_start < pad_ends[-1]).astype(jnp.int32)
    ar = jnp.arange(N_EXPERTS, dtype=jnp.int32)
    has = counts > 0
    later = (ar[None, :] > ar[:, None]) & has[None, :]
    next_has = jnp.min(jnp.where(later, ar[None, :], N_EXPERTS), axis=1)
    next_has = jnp.where(next_has >= N_EXPERTS, -1, next_has)
    after = (ar[None, :] == next_has[:, None])
    next2_has = jnp.sum(jnp.where(after, next_has[None, :], 0), axis=1)
    next2_has = jnp.where(next_has < 0, -1, next2_has)
    ordinal = jnp.cumsum(has.astype(jnp.int32)) - 1
    mine = block_expert[:, None] == ar[None, :]
    pick = lambda v: jnp.sum(jnp.where(mine, v[None, :], 0), axis=1).astype(jnp.int32)
    return (pad_starts, pad_ends.astype(jnp.int32), block_expert, block_valid, pick(next_has),
            pick(next2_has), pick(ordinal) % 2, n_rows)


def kernel(x, c, w_ada, b_ada, w_in, rel_bias, ssm_a_re, ssm_a_im, ssm_log_dt, ssm_b_re, ssm_b_im, ssm_c_re, ssm_c_im, ssm_d, w_attn_out, w_ssm_glu, w_o, ln1_g, ln1_b, w_router, router_bias, e_w_in, e_w_out, s_w_in, s_w_out, ln2_g, ln2_b):
    bsz, l, d = x.shape
    assert bsz == 1
    xf = x.reshape(l, d)
    i = 0
    mod = _modulation(c, w_ada[i], b_ada[i])
    shift1, scale1, gate1, shift2, scale2, gate2 = [mod[:, k * d:(k + 1) * d] for k in range(6)]

    w_in_b = w_in[i].astype(BF16)
    u = _modulate(xf, scale1, shift1)
    s_in = _matmul(u, w_in_b, QKV_WIDTH, SSM_WIDTH, 512, 512, F32, "in_proj_ssm")

    outs, lses = [], []
    for gi, (window, dilation) in enumerate(DILATED_GROUPS):
        bias_prev, bias_cur = _attn_bias_tables(rel_bias, gi, window, dilation)
        qkv = _inproj_qkv_group(u, w_in_b, gi, dilation)
        o, s = _attention_group(qkv, bias_prev, bias_cur, gi)
        outs.append(o)
        lses.append(s)
    attn = _attn_combine(outs, lses)

    tabs = _ssm_tables(ssm_a_re[i], ssm_a_im[i], ssm_log_dt[i], ssm_b_re[i], ssm_b_im[i],
                       ssm_c_re[i], ssm_c_im[i])
    y_ssm = _ssm(s_in, tabs, ssm_d[i])

    merged = _merge(attn, y_ssm, u, w_attn_out[i].astype(BF16), w_ssm_glu[i].astype(BF16), w_in_b)
    x1, u2, u2p, scores_t = _outproj(merged, xf, w_o[i].astype(BF16), w_router[i].T.astype(BF16), gate1,
                                     ln1_g[i].reshape(1, d), ln1_b[i].reshape(1, d), scale2, shift2)

    idx, w, rank, counts = _route(scores_t, router_bias[i])
    (pad_start, pad_end, block_expert, block_valid, block_next, block_next2, block_slot,
     n_rows) = _block_layout(counts[:, 0].astype(jnp.int32), l)
    x_rows = _dispatch(idx, rank, pad_start, pad_end, u2p, n_rows)
    y_rows = _experts(block_expert, block_valid, block_next, block_next2, block_slot, x_rows,
                      e_w_in[i], e_w_out[i])
    out = _final(idx, rank, pad_start, u2, x1, w.T, y_rows, s_w_in[i].astype(BF16),
                 s_w_out[i].astype(BF16), gate2, ln2_g[i].reshape(1, d), ln2_b[i].reshape(1, d))
    return out.reshape(bsz, l, d)
```

```python
import functools
import math

import jax
import jax.numpy as jnp
from jax import lax
from jax.experimental import pallas as pl
from jax.experimental.pallas import tpu as pltpu

F32 = jnp.float32
BF16 = jnp.bfloat16

D_MODEL = 2048
HEAD_DIM = 128
HEADS_PER_GROUP = 4
DILATED_GROUPS = ((128, 1), (512, 4), (2048, 16))
N_ATTN_GROUPS = len(DILATED_GROUPS)
N_ATTN_HEADS = N_ATTN_GROUPS * HEADS_PER_GROUP
ATTN_WIDTH = N_ATTN_HEADS * HEAD_DIM
ATTN_OUT_WIDTH = HEADS_PER_GROUP * HEAD_DIM
ATTN_BLOCK = 128
REL_BUCKETS = 32
REL_MAX_DISTANCE = 2048
SSM_GROUP_CH = 16
SSM_STATE = 64
SSM_WIDTH = 1024
SSM_GROUPS = SSM_WIDTH // SSM_GROUP_CH
A_RE_MAX = -1e-4
QKV_WIDTH = 3 * ATTN_WIDTH
GATE_OFF = QKV_WIDTH + SSM_WIDTH
N_EXPERTS = 256
TOP_K = 8
N_EXPERT_GROUPS = 8
TOPK_GROUPS = 4
EXPERT_FF = 512
SHARED_FF = 512
ROUTED_SCALE = 2.5
MOE_BLOCK = 128
DEPTH = 1
DEEPNORM_ALPHA = (2 * DEPTH) ** 0.25
LN_EPS = 1e-5
NEG_BIG = -1e30
LANES = 128
SUBLANES = 8

SSM_CHUNK = 128
SSM_BLK_GROUPS = 16
SSM_NBLK = SSM_GROUPS // SSM_BLK_GROUPS
SSM_BLK_CH = SSM_BLK_GROUPS * SSM_GROUP_CH
SSM_BLK_ST = SSM_BLK_GROUPS * SSM_STATE
SSM_STATES = SSM_GROUPS * SSM_STATE

VMEM_LIMIT = 56 * 1024 * 1024


def _params(sem, vmem=VMEM_LIMIT):
    return pltpu.CompilerParams(dimension_semantics=sem, vmem_limit_bytes=vmem)


def _mod_kernel(c_ref, w_ref, b_ref, o_ref):
    c = c_ref[...]
    cond = c * jax.nn.sigmoid(c)
    o_ref[...] = jnp.dot(cond, w_ref[...], preferred_element_type=F32,
                         precision=lax.Precision.HIGHEST) + b_ref[...]


def _modulation(c, w_ada, b_ada):
    d, n = w_ada.shape
    tn = 1024
    c8 = jnp.broadcast_to(c, (8, d))
    out = pl.pallas_call(
        _mod_kernel,
        out_shape=jax.ShapeDtypeStruct((8, n), F32),
        grid=(n // tn,),
        in_specs=[pl.BlockSpec((8, d), lambda j: (0, 0)),
                  pl.BlockSpec((d, tn), lambda j: (0, j)),
                  pl.BlockSpec((1, tn), lambda j: (0, j))],
        out_specs=pl.BlockSpec((8, tn), lambda j: (0, j)),
        compiler_params=_params(("arbitrary",)),
        name="adaln_mod",
    )(c8, w_ada, b_ada.reshape(1, n))
    return out[0:1]


def _modulate_kernel(x_ref, sc_ref, sh_ref, o_ref):
    o_ref[...] = (x_ref[...] * (1.0 + sc_ref[...]) + sh_ref[...]).astype(o_ref.dtype)


def _modulate(x, scale, shift):
    m, d = x.shape
    tm = 512
    return pl.pallas_call(
        _modulate_kernel,
        out_shape=jax.ShapeDtypeStruct((m, d), BF16),
        grid=(m // tm,),
        in_specs=[pl.BlockSpec((tm, d), lambda i: (i, 0)),
                  pl.BlockSpec((1, d), lambda i: (0, 0)),
                  pl.BlockSpec((1, d), lambda i: (0, 0))],
        out_specs=pl.BlockSpec((tm, d), lambda i: (i, 0)),
        compiler_params=_params(("arbitrary",)),
        name="modulate",
    )(x, scale, shift)


def _mm_kernel(a_ref, w_ref, o_ref):
    o_ref[...] = jnp.dot(a_ref[...], w_ref[...], preferred_element_type=F32).astype(o_ref.dtype)


def _matmul(a, w, col_off, n, tm, tn, out_dtype, name):
    m, k = a.shape
    assert col_off % tn == 0 and n % tn == 0 and m % tm == 0
    off_blocks = col_off // tn
    return pl.pallas_call(
        _mm_kernel,
        out_shape=jax.ShapeDtypeStruct((m, n), out_dtype),
        grid=(n // tn, m // tm),
        in_specs=[pl.BlockSpec((tm, k), lambda j, i: (i, 0)),
                  pl.BlockSpec((k, tn), lambda j, i: (0, j + off_blocks))],
        out_specs=pl.BlockSpec((tm, tn), lambda j, i: (i, j)),
        compiler_params=_params(("arbitrary", "arbitrary")),
        name=name,
    )(a, w)


ATTN_QBLOCKS = 2


def _attn_kernel(bp_ref, bc_ref, q_ref, kp_ref, kc_ref, vp_ref, vc_ref, o_ref, lse_ref):
    scale = HEAD_DIM ** -0.5
    nt = (((1,), (1,)), ((), ()))
    first = pl.program_id(1) == 0
    for h in range(HEADS_PER_GROUP):
        hs = slice(h * HEAD_DIM, (h + 1) * HEAD_DIM)
        for qb in range(ATTN_QBLOCKS):
            rows = slice(qb * ATTN_BLOCK, (qb + 1) * ATTN_BLOCK)
            before = slice((qb - 1) * ATTN_BLOCK, qb * ATTN_BLOCK)
            k_prev = kp_ref[:, hs] if qb == 0 else kc_ref[before, hs]
            v_prev = vp_ref[:, hs] if qb == 0 else vc_ref[before, hs]
            q = q_ref[rows, hs]
            s_p = lax.dot_general(q, k_prev, nt, preferred_element_type=F32) * scale + bp_ref[h]
            s_c = lax.dot_general(q, kc_ref[rows, hs], nt, preferred_element_type=F32) * scale + bc_ref[h]
            if qb == 0:
                s_p = jnp.where(first, NEG_BIG, s_p)
            m = jnp.maximum(jnp.max(s_p, axis=-1, keepdims=True), jnp.max(s_c, axis=-1, keepdims=True))
            p_p = jnp.exp(s_p - m)
            p_c = jnp.exp(s_c - m)
            l = jnp.sum(p_p, axis=-1, keepdims=True) + jnp.sum(p_c, axis=-1, keepdims=True)
            o = (jnp.dot(p_p.astype(BF16), v_prev, preferred_element_type=F32)
                 + jnp.dot(p_c.astype(BF16), vc_ref[rows, hs], preferred_element_type=F32))
            o_ref[rows, hs] = o / l
            lse_ref[rows, hs] = jnp.broadcast_to(m + jnp.log(l), (ATTN_BLOCK, HEAD_DIM))


def _inproj_dilated_kernel(a_ref, wq_ref, wk_ref, wv_ref, o_ref, acc_ref):
    dilation, rows, _ = o_ref.shape
    a = a_ref[...]
    for j, w_ref in enumerate((wq_ref, wk_ref, wv_ref)):
        tn = w_ref.shape[1]
        res = jnp.dot(a, w_ref[...], preferred_element_type=F32)
        if dilation == 1:
            o_ref[0, :, j * tn:(j + 1) * tn] = res.astype(o_ref.dtype)
            continue
        for c in range(tn // LANES):
            acc_ref[c] = res[:, c * LANES:(c + 1) * LANES]
        for r in range(dilation):
            for c in range(tn // LANES):
                o_ref[r, :, j * tn + c * LANES:j * tn + (c + 1) * LANES] = (
                    acc_ref[c, pl.ds(r, rows, stride=dilation), :].astype(o_ref.dtype))


def _inproj_qkv_group(u, w_in, gi, dilation):
    l, k = u.shape
    tm, tn = 512, ATTN_OUT_WIDTH
    w_spec = lambda j: pl.BlockSpec((k, tn), lambda i: (0, j * N_ATTN_GROUPS + gi))
    return pl.pallas_call(
        _inproj_dilated_kernel,
        out_shape=jax.ShapeDtypeStruct((dilation, l // dilation, 3 * tn), BF16),
        grid=(l // tm,),
        in_specs=[pl.BlockSpec((tm, k), lambda i: (i, 0)), w_spec(0), w_spec(1), w_spec(2)],
        out_specs=pl.BlockSpec((dilation, tm // dilation, 3 * tn), lambda i: (0, i, 0)),
        scratch_shapes=[pltpu.VMEM((tn // LANES, tm, LANES), F32)],
        compiler_params=_params(("arbitrary",)),
        name=f"in_proj_qkv_g{gi}",
    )(u, w_in, w_in, w_in)


def _attention_group(qkv, bias_prev, bias_cur, gi):
    dilation, m, _ = qkv.shape
    tile = ATTN_QBLOCKS * ATTN_BLOCK

    def cur(cb):
        return pl.BlockSpec((None, tile, ATTN_OUT_WIDTH), lambda r, b: (r, b, cb))

    def prev(cb):
        return pl.BlockSpec((None, ATTN_BLOCK, ATTN_OUT_WIDTH),
                            lambda r, b: (r, jnp.maximum(b * ATTN_QBLOCKS - 1, 0), cb))

    bias_spec = pl.BlockSpec((HEADS_PER_GROUP, ATTN_BLOCK, ATTN_BLOCK), lambda r, b: (0, 0, 0))
    out_spec = pl.BlockSpec((None, tile, ATTN_OUT_WIDTH), lambda r, b: (r, b, 0))
    return pl.pallas_call(
        _attn_kernel,
        out_shape=(jax.ShapeDtypeStruct((dilation, m, ATTN_OUT_WIDTH), F32),
                   jax.ShapeDtypeStruct((dilation, m, ATTN_OUT_WIDTH), F32)),
        grid=(dilation, m // tile),
        in_specs=[bias_spec, bias_spec, cur(0), prev(1), cur(1), prev(2), cur(2)],
        out_specs=(out_spec, out_spec),
        compiler_params=_params(("arbitrary", "arbitrary")),
        name=f"dilated_attn_g{gi}",
    )(bias_prev, bias_cur, qkv, qkv, qkv, qkv, qkv)


def _t5_bucket(dist):
    max_exact = REL_BUCKETS // 2
    d_f = jnp.maximum(dist, 1).astype(F32)
    large = max_exact + (jnp.log(d_f / max_exact) / math.log(REL_MAX_DISTANCE / max_exact)
                         * (REL_BUCKETS - max_exact)).astype(jnp.int32)
    large = jnp.minimum(large, REL_BUCKETS - 1)
    return jnp.where(dist < max_exact, dist, large)


def _attn_bias_tables(rel_bias, gi, window, dilation):
    n_keys = window // dilation
    bucket = _t5_bucket(jnp.arange(n_keys + 1, dtype=jnp.int32) * dilation)
    hs = slice(gi * HEADS_PER_GROUP, (gi + 1) * HEADS_PER_GROUP)
    hi = lax.Precision.HIGHEST
    pick_bucket = (bucket[:, None] == jnp.arange(REL_BUCKETS)[None, :]).astype(F32)
    by_dist = jnp.dot(pick_bucket, rel_bias[:, hs].astype(F32), precision=hi)
    qi = jnp.arange(ATTN_BLOCK)[:, None]
    kj = jnp.arange(ATTN_BLOCK)[None, :]
    d_prev = qi + ATTN_BLOCK - kj
    d_cur = qi - kj
    tabs = []
    for dist in (d_prev, d_cur):
        ok = (dist >= 0) & (dist <= n_keys)
        pick_dist = (dist[:, :, None] == jnp.arange(n_keys + 1)[None, None, :]).astype(F32)
        vals = jnp.einsum('ijd,dh->hij', pick_dist, by_dist, precision=hi)
        tabs.append(jnp.where(ok[None], vals, NEG_BIG))
    return tabs


ATTN_COMBINE_TILE = 512


def _attn_combine_kernel(o1, o2, o3, l1, l2, l3, out_ref, *scratch):
    def token_order(ref, buf, c):
        dilation, rows, _ = ref.shape
        cs = slice(c * LANES, (c + 1) * LANES)
        if dilation == 1:
            return ref[0, :, cs]
        for r in range(dilation):
            buf[c, pl.ds(r, rows, stride=dilation), :] = ref[r, :, cs]
        return buf[c]

    for c in range(out_ref.shape[1] // LANES):
        v1, v2, v3, a1, a2, a3 = [token_order(ref, buf, c)
                                  for ref, buf in zip((o1, o2, o3, l1, l2, l3), scratch)]
        m = jnp.maximum(jnp.maximum(a1, a2), a3)
        e1, e2, e3 = jnp.exp(a1 - m), jnp.exp(a2 - m), jnp.exp(a3 - m)
        num = e1 * v1 + e2 * v2 + e3 * v3
        out_ref[:, c * LANES:(c + 1) * LANES] = (num / (e1 + e2 + e3)).astype(out_ref.dtype)


def _attn_combine(outs, lses):
    w = outs[0].shape[-1]
    l = outs[0].shape[0] * outs[0].shape[1]
    tm = ATTN_COMBINE_TILE

    def spec(a):
        dilation = a.shape[0]
        return pl.BlockSpec((dilation, tm // dilation, w), lambda i: (0, i, 0))

    return pl.pallas_call(
        _attn_combine_kernel,
        out_shape=jax.ShapeDtypeStruct((l, w), BF16),
        grid=(l // tm,),
        in_specs=[spec(a) for a in (*outs, *lses)],
        out_specs=pl.BlockSpec((tm, w), lambda i: (i, 0)),
        scratch_shapes=[pltpu.VMEM((w // LANES, tm, LANES), F32)] * 6,
        compiler_params=_params(("arbitrary",)),
        name="attn_combine",
    )(*outs, *lses)


def _ssm_kernel(u_ref, bre_ref, bim_ref, cre_ref, cim_ref, enr_ref, eni_ref, epr_ref, epi_ref,
                lamr_ref, lami_ref, d_ref, tri_ref, y_ref, car_ref, cai_ref):
    @pl.when(pl.program_id(0) == 0)
    def _():
        car_ref[...] = jnp.zeros_like(car_ref)
        cai_ref[...] = jnp.zeros_like(cai_ref)

    tc = u_ref.shape[0]
    tri = tri_ref[...]
    for j in range(SSM_NBLK):
        ch = slice(j * SSM_BLK_CH, (j + 1) * SSM_BLK_CH)
        st = slice(j * SSM_BLK_ST, (j + 1) * SSM_BLK_ST)
        u = u_ref[:, ch]
        ub = u.astype(BF16)
        bur = jnp.dot(ub, bre_ref[j], preferred_element_type=F32)
        bui = jnp.dot(ub, bim_ref[j], preferred_element_type=F32)
        enr, eni = enr_ref[:, st], eni_ref[:, st]
        xr = bur * enr - bui * eni
        xi = bur * eni + bui * enr
        pr = jnp.dot(tri, xr.astype(BF16), preferred_element_type=F32)
        pi = jnp.dot(tri, xi.astype(BF16), preferred_element_type=F32)
        cr, ci = car_ref[:, st], cai_ref[:, st]
        lr, li = lamr_ref[:, st], lami_ref[:, st]
        tr = pr + (lr * cr - li * ci)
        ti = pi + (lr * ci + li * cr)
        epr, epi = epr_ref[:, st], epi_ref[:, st]
        sr = epr * tr - epi * ti
        si = epr * ti + epi * tr
        car_ref[:, st] = sr[tc - 1:tc]
        cai_ref[:, st] = si[tc - 1:tc]
        y = (jnp.dot(sr.astype(BF16), cre_ref[j], preferred_element_type=F32)
             - jnp.dot(si.astype(BF16), cim_ref[j], preferred_element_type=F32))
        y_ref[:, ch] = (y + d_ref[:, ch] * u).astype(y_ref.dtype)


def _ssm_tables(a_re, a_im, log_dt, b_re, b_im, c_re, c_im):
    g, p, hc = b_re.shape
    lam_re = jnp.minimum(a_re.astype(F32), A_RE_MAX)
    lam_im = a_im.astype(F32)
    dt = jnp.exp(log_dt.astype(F32))[:, None]
    mag = jnp.exp(lam_re * dt)
    lb_re, lb_im = mag * jnp.cos(lam_im * dt), mag * jnp.sin(lam_im * dt)
    imag = jnp.exp(-lam_re * dt)
    li_re, li_im = imag * jnp.cos(lam_im * dt), -imag * jnp.sin(lam_im * dt)
    den = lam_re * lam_re + lam_im * lam_im
    nr, ni = lb_re - 1.0, lb_im
    f_re = (nr * lam_re + ni * lam_im) / den
    f_im = (ni * lam_re - nr * lam_im) / den
    bb_re = f_re[:, :, None] * b_re - f_im[:, :, None] * b_im
    bb_im = f_re[:, :, None] * b_im + f_im[:, :, None] * b_re

    def powers(pr, pi):
        er, ei = jnp.ones((1, g * p), F32), jnp.zeros((1, g * p), F32)
        pr, pi = pr.reshape(1, g * p), pi.reshape(1, g * p)
        while er.shape[0] < SSM_CHUNK:
            er, ei = (jnp.concatenate([er, er * pr - ei * pi], axis=0),
                      jnp.concatenate([ei, er * pi + ei * pr], axis=0))
            pr, pi = pr * pr - pi * pi, 2.0 * pr * pi
        return er, ei

    epr, epi = powers(lb_re, lb_im)
    enr, eni = powers(li_re, li_im)
    eye = jnp.eye(SSM_BLK_GROUPS, dtype=F32)

    def b_blocks(b):
        b = b.reshape(SSM_NBLK, SSM_BLK_GROUPS, p, hc)
        return jnp.einsum('jgph,gk->jghkp', b, eye).reshape(SSM_NBLK, SSM_BLK_CH, SSM_BLK_ST).astype(BF16)

    def c_blocks(c):
        c = c.astype(F32).reshape(SSM_NBLK, SSM_BLK_GROUPS, hc, p)
        return jnp.einsum('jghp,gk->jkpgh', c, eye).reshape(SSM_NBLK, SSM_BLK_ST, SSM_BLK_CH).astype(BF16)

    return dict(bre=b_blocks(bb_re), bim=b_blocks(bb_im), cre=c_blocks(c_re), cim=c_blocks(c_im),
                enr=enr, eni=eni, epr=epr, epi=epi,
                lamr=lb_re.reshape(1, g * p), lami=lb_im.reshape(1, g * p))


def _ssm(s_in, tabs, d_skip):
    l, w = s_in.shape
    tc = SSM_CHUNK
    tri = jnp.tril(jnp.ones((tc, tc), F32)).astype(BF16)
    full = lambda shape: pl.BlockSpec(shape, lambda c: (0,) * len(shape))
    return pl.pallas_call(
        _ssm_kernel,
        out_shape=jax.ShapeDtypeStruct((l, w), BF16),
        grid=(l // tc,),
        in_specs=[pl.BlockSpec((tc, w), lambda c: (c, 0)),
                  full((SSM_NBLK, SSM_BLK_CH, SSM_BLK_ST)), full((SSM_NBLK, SSM_BLK_CH, SSM_BLK_ST)),
                  full((SSM_NBLK, SSM_BLK_ST, SSM_BLK_CH)), full((SSM_NBLK, SSM_BLK_ST, SSM_BLK_CH)),
                  full((tc, SSM_STATES)), full((tc, SSM_STATES)),
                  full((tc, SSM_STATES)), full((tc, SSM_STATES)),
                  full((1, SSM_STATES)), full((1, SSM_STATES)),
                  full((1, w)), full((tc, tc))],
        out_specs=pl.BlockSpec((tc, w), lambda c: (c, 0)),
        scratch_shapes=[pltpu.VMEM((1, SSM_STATES), F32), pltpu.VMEM((1, SSM_STATES), F32)],
        compiler_params=_params(("arbitrary",)),
        name="s5_ssm",
    )(s_in, tabs['bre'], tabs['bim'], tabs['cre'], tabs['cim'], tabs['enr'], tabs['eni'],
      tabs['epr'], tabs['epi'], tabs['lamr'], tabs['lami'], d_skip.reshape(1, w).astype(F32), tri)


def _merge_kernel(attn_ref, y_ref, u_ref, wa_ref, wg1_ref, wg2_ref, wia_ref, wis_ref, o_ref):
    a = jnp.dot(attn_ref[...], wa_ref[...], preferred_element_type=F32)
    y = y_ref[...]
    s = (jnp.dot(y, wg1_ref[...], preferred_element_type=F32)
         * jax.nn.sigmoid(jnp.dot(y, wg2_ref[...], preferred_element_type=F32)))
    u = u_ref[...]
    ga = jax.nn.sigmoid(jnp.dot(u, wia_ref[...], preferred_element_type=F32))
    gs = jax.nn.sigmoid(jnp.dot(u, wis_ref[...], preferred_element_type=F32))
    o_ref[...] = (ga * a + gs * s).astype(o_ref.dtype)


def _merge(attn, y_ssm, u, w_attn_out, w_ssm_glu, w_in):
    l, d = u.shape
    tm, tn = 512, 512
    nd = d // tn
    go = GATE_OFF // tn
    return pl.pallas_call(
        _merge_kernel,
        out_shape=jax.ShapeDtypeStruct((l, d), BF16),
        grid=(nd, l // tm),
        in_specs=[pl.BlockSpec((tm, ATTN_OUT_WIDTH), lambda j, i: (i, 0)),
                  pl.BlockSpec((tm, SSM_WIDTH), lambda j, i: (i, 0)),
                  pl.BlockSpec((tm, d), lambda j, i: (i, 0)),
                  pl.BlockSpec((ATTN_OUT_WIDTH, tn), lambda j, i: (0, j)),
                  pl.BlockSpec((SSM_WIDTH, tn), lambda j, i: (0, j)),
                  pl.BlockSpec((SSM_WIDTH, tn), lambda j, i: (0, j + nd)),
                  pl.BlockSpec((d, tn), lambda j, i: (0, j + go)),
                  pl.BlockSpec((d, tn), lambda j, i: (0, j + go + nd))],
        out_specs=pl.BlockSpec((tm, tn), lambda j, i: (i, j)),
        compiler_params=_params(("arbitrary", "arbitrary")),
        name="branch_merge",
    )(attn, y_ssm, u, w_attn_out, w_ssm_glu, w_ssm_glu, w_in, w_in)


def _layer_norm(h, g, b):
    mu = jnp.mean(h, axis=-1, keepdims=True)
    c = h - mu
    var = jnp.mean(c * c, axis=-1, keepdims=True)
    return c * lax.rsqrt(var + LN_EPS) * g + b


def _pack_rows(v):
    bits = lax.bitcast_convert_type(v.astype(BF16).astype(F32), jnp.uint32)
    half = bits.shape[1] // 2
    return (bits[:, :half] >> 16) | (bits[:, half:] & jnp.uint32(0xFFFF0000))


def _unpack_lo(xp):
    return lax.bitcast_convert_type(xp << 16, F32)


def _unpack_hi(xp):
    return lax.bitcast_convert_type(xp & jnp.uint32(0xFFFF0000), F32)


def _store_tile_rows(ref, packed):
    rows = packed.shape[0]
    for s in range(SUBLANES):
        ref[pl.ds(s, rows, stride=SUBLANES), :] = packed[:, s * LANES:(s + 1) * LANES]


def _load_tile_rows(ref, s):
    return ref[pl.ds(s, ref.shape[0] // SUBLANES, stride=SUBLANES), :]


def _outproj_kernel(mg_ref, x_ref, wo_ref, wr_ref, g1_ref, lg_ref, lb_ref, sc_ref, sh_ref,
                    x1_ref, u2_ref, u2p_ref, sco_ref):
    mix = jnp.dot(mg_ref[...], wo_ref[...], preferred_element_type=F32)
    x1 = _layer_norm(DEEPNORM_ALPHA * x_ref[...] + g1_ref[...] * mix, lg_ref[...], lb_ref[...])
    x1_ref[...] = x1
    u2 = (x1 * (1.0 + sc_ref[...]) + sh_ref[...]).astype(BF16)
    u2_ref[...] = u2
    _store_tile_rows(u2p_ref, _pack_rows(u2))
    logits = lax.dot_general(wr_ref[...], u2, (((1,), (1,)), ((), ())), preferred_element_type=F32)
    sco_ref[...] = jax.nn.sigmoid(logits)


def _outproj(merged, x, w_o, w_router_t, gate1, ln_g, ln_b, scale2, shift2):
    l, d = x.shape
    assert d == 2 * SUBLANES * LANES
    e = w_router_t.shape[0]
    tm = 512
    row = lambda w: pl.BlockSpec((tm, w), lambda i: (i, 0))
    vec = pl.BlockSpec((1, d), lambda i: (0, 0))
    return pl.pallas_call(
        _outproj_kernel,
        out_shape=(jax.ShapeDtypeStruct((l, d), F32), jax.ShapeDtypeStruct((l, d), BF16),
                   jax.ShapeDtypeStruct((l * SUBLANES, LANES), jnp.uint32),
                   jax.ShapeDtypeStruct((e, l), F32)),
        grid=(l // tm,),
        in_specs=[row(d), row(d), pl.BlockSpec((d, d), lambda i: (0, 0)),
                  pl.BlockSpec((e, d), lambda i: (0, 0)), vec, vec, vec, vec, vec],
        out_specs=(row(d), row(d), pl.BlockSpec((tm * SUBLANES, LANES), lambda i: (i, 0)),
                   pl.BlockSpec((e, tm), lambda i: (0, i))),
        compiler_params=_params(("arbitrary",)),
        name="outproj_ln1_router",
    )(merged, x, w_o, w_router_t, gate1, ln_g, ln_b, scale2, shift2)


ROUTE_TILE = 512
EXPERTS_PER_GROUP = N_EXPERTS // N_EXPERT_GROUPS


def _route_kernel(s_ref, b_ref, tri_ref, idx_ref, w_ref, rank_ref, cnt_ref, carry_ref):
    @pl.when(pl.program_id(0) == 0)
    def _():
        carry_ref[...] = jnp.zeros_like(carry_ref)

    ne, tm = s_ref.shape
    neg_inf = -jnp.inf
    s = s_ref[...]
    sel = s + b_ref[...]
    e_iota = lax.broadcasted_iota(jnp.int32, (ne, tm), 0)

    gs_rows = []
    for g in range(N_EXPERT_GROUPS):
        xg = sel[g * EXPERTS_PER_GROUP:(g + 1) * EXPERTS_PER_GROUP]
        m1 = jnp.max(xg, axis=0, keepdims=True)
        n1 = jnp.sum((xg == m1).astype(F32), axis=0, keepdims=True)
        m2 = jnp.max(jnp.where(xg < m1, xg, neg_inf), axis=0, keepdims=True)
        gs_rows.append(m1 + jnp.where(n1 >= 2.0, m1, m2))
    gs = jnp.concatenate(gs_rows, axis=0)

    g_iota = lax.broadcasted_iota(jnp.int32, gs.shape, 0)
    beaten = jnp.zeros(gs.shape, jnp.int32)
    for g2 in range(N_EXPERT_GROUPS):
        row = gs[g2:g2 + 1]
        beats = (row > gs) | ((row == gs) & (g2 < g_iota))
        beaten = beaten + beats.astype(jnp.int32)
    g_ok = beaten < TOPK_GROUPS
    work = jnp.concatenate(
        [jnp.where(g_ok[g:g + 1], sel[g * EXPERTS_PER_GROUP:(g + 1) * EXPERTS_PER_GROUP], neg_inf)
         for g in range(N_EXPERT_GROUPS)], axis=0)

    idxs, vals = [], []
    chosen = jnp.zeros((ne, tm), F32)
    for _ in range(TOP_K):
        m = jnp.max(work, axis=0, keepdims=True)
        i = jnp.min(jnp.where(work == m, e_iota, ne), axis=0, keepdims=True)
        onehot = e_iota == i
        idxs.append(i)
        vals.append(jnp.sum(jnp.where(onehot, s, 0.0), axis=0, keepdims=True))
        chosen = jnp.where(onehot, 1.0, chosen)
        work = jnp.where(onehot, neg_inf, work)
    wsum = vals[0]
    for v in vals[1:]:
        wsum = wsum + v

    before = jnp.dot(chosen.astype(BF16), tri_ref[...], preferred_element_type=F32) + carry_ref[...]
    ranks = [jnp.sum(jnp.where(e_iota == i, before, 0.0), axis=0, keepdims=True) for i in idxs]
    carry_ref[...] = carry_ref[...] + jnp.sum(chosen, axis=1, keepdims=True)

    idx_ref[...] = jnp.concatenate(idxs, axis=0)
    w_ref[...] = jnp.concatenate([v / wsum * ROUTED_SCALE for v in vals], axis=0)
    rank_ref[...] = jnp.concatenate(ranks, axis=0).astype(jnp.int32)
    cnt_ref[...] = carry_ref[...]


def _route(scores_t, router_bias):
    ne, t = scores_t.shape
    tm = ROUTE_TILE
    tri = jnp.triu(jnp.ones((tm, tm), F32), k=1).astype(BF16)
    tok = pl.BlockSpec((TOP_K, tm), lambda i: (0, i))
    return pl.pallas_call(
        _route_kernel,
        out_shape=(jax.ShapeDtypeStruct((TOP_K, t), jnp.int32), jax.ShapeDtypeStruct((TOP_K, t), F32),
                   jax.ShapeDtypeStruct((TOP_K, t), jnp.int32), jax.ShapeDtypeStruct((ne, 1), F32)),
        grid=(t // tm,),
        in_specs=[pl.BlockSpec((ne, tm), lambda i: (0, i)),
                  pl.BlockSpec((ne, 1), lambda i: (0, 0)),
                  pl.BlockSpec((tm, tm), lambda i: (0, 0))],
        out_specs=(tok, tok, tok, pl.BlockSpec((ne, 1), lambda i: (0, 0))),
        scratch_shapes=[pltpu.VMEM((ne, 1), F32)],
        compiler_params=_params(("arbitrary",)),
        name="moe_route",
    )(scores_t, router_bias.astype(F32).reshape(ne, 1), tri)


DISPATCH_TILE = 256


def _dispatch_kernel(idx_ref, rank_ref, ps_ref, pe_ref, u_ref, xs_ref, zero_ref, sem, zsem):
    tm = u_ref.shape[0] // SUBLANES
    ne = ps_ref.shape[0]
    block_words = MOE_BLOCK * SUBLANES

    def tile_row(r):
        return pl.ds(pl.multiple_of(r * SUBLANES, SUBLANES), SUBLANES)

    @pl.when(pl.program_id(0) == 0)
    def _():
        zero_ref[...] = jnp.zeros_like(zero_ref)

        def block_copy(start):
            start = pl.multiple_of(start * SUBLANES, block_words)
            return pltpu.make_async_copy(zero_ref, xs_ref.at[pl.ds(start, block_words)], zsem)

        def fill(e, carry):
            @pl.when(pe_ref[e] > ps_ref[e])
            def _():
                block_copy(pe_ref[e] - MOE_BLOCK).start()
            return carry

        def fill_done(e, carry):
            @pl.when(pe_ref[e] > ps_ref[e])
            def _():
                block_copy(pe_ref[e] - MOE_BLOCK).wait()
            return carry

        lax.fori_loop(0, ne, fill, 0)
        lax.fori_loop(0, ne, fill_done, 0)

        first_unused = pe_ref[ne - 1] // MOE_BLOCK
        n_blocks = xs_ref.shape[0] // block_words

        def fill_unused(b, carry):
            block_copy(b * MOE_BLOCK).start()
            return carry

        def fill_unused_done(b, carry):
            block_copy(b * MOE_BLOCK).wait()
            return carry

        lax.fori_loop(first_unused, n_blocks, fill_unused, 0)
        lax.fori_loop(first_unused, n_blocks, fill_unused_done, 0)

    def issue(t, carry):
        for k in range(TOP_K):
            p = ps_ref[idx_ref[k, t]] + rank_ref[k, t]
            pltpu.make_async_copy(u_ref.at[tile_row(t)], xs_ref.at[tile_row(p)], sem).start(
                priority=k % 2)
        return carry

    lax.fori_loop(0, tm, issue, 0)
    for _ in range(TOP_K):
        pltpu.make_async_copy(u_ref, xs_ref.at[pl.ds(0, tm * SUBLANES)], sem).wait()


def _dispatch(idx, rank, pad_start, pad_end, u2p, n_rows):
    t = u2p.shape[0] // SUBLANES
    tm = DISPATCH_TILE
    smem_tok = pl.BlockSpec((TOP_K, tm), lambda i: (0, i), memory_space=pltpu.SMEM)
    smem_all = pl.BlockSpec(memory_space=pltpu.SMEM)
    return pl.pallas_call(
        _dispatch_kernel,
        out_shape=jax.ShapeDtypeStruct((n_rows * SUBLANES, LANES), jnp.uint32),
        grid=(t // tm,),
        in_specs=[smem_tok, smem_tok, smem_all, smem_all,
                  pl.BlockSpec((tm * SUBLANES, LANES), lambda i: (i, 0))],
        out_specs=pl.BlockSpec(memory_space=pl.ANY),
        scratch_shapes=[pltpu.VMEM((MOE_BLOCK * SUBLANES, LANES), jnp.uint32),
                        pltpu.SemaphoreType.DMA, pltpu.SemaphoreType.DMA],
        compiler_params=_params(("arbitrary",)),
        name="moe_dispatch",
    )(idx, rank, pad_start, pad_end, u2p)


WEIGHT_CHUNK_BYTES = 2 * 1024 * 1024


def _expert_kernel(be_ref, bv_ref, nx_ref, nx2_ref, sl_ref, x_ref, win_ref, wout_ref, y_ref,
                   wfi_ref, wfo_ref, wbi_ref, wbo_ref, sem):
    b = pl.program_id(0)
    valid = bv_ref[b] > 0
    e = be_ref[b]
    slot = sl_ref[b]
    new_expert = (b == 0) | (e != be_ref[jnp.maximum(b - 1, 0)])

    def weight_chunks():
        chunks = []
        for hbm, land, wb, j in ((win_ref, wfi_ref, wbi_ref, 0), (wout_ref, wfo_ref, wbo_ref, 1)):
            rows = WEIGHT_CHUNK_BYTES // (hbm.shape[2] * 4)
            for c in range(hbm.shape[1] // rows):
                chunks.append((hbm, land, wb, j, pl.ds(c * rows, rows)))
        return chunks

    def chunk_copy(chunk, ex, s):
        hbm, land, _, j, rs = chunk
        return pltpu.make_async_copy(hbm.at[ex, rs], land.at[s, rs], sem.at[s, j])

    def start_weights(ex, s):
        for n, chunk in enumerate(weight_chunks()):
            chunk_copy(chunk, ex, s).start(priority=n % 2)

    @pl.when(b == 0)
    def _():
        start_weights(e, slot)

        @pl.when(nx_ref[b] >= 0)
        def _():
            start_weights(nx_ref[b], 1 - slot)

    @pl.when(valid & new_expert)
    def _():
        for chunk in weight_chunks():
            chunk_copy(chunk, e, slot).wait()
        for n, chunk in enumerate(weight_chunks()):
            _, land, wb, _, rs = chunk
            for s in range(2):
                @pl.when(slot == s)
                def _():
                    wb[rs] = land[s, rs].astype(BF16)

            @pl.when(nx2_ref[b] >= 0)
            def _():
                chunk_copy(chunk, nx2_ref[b], slot).start(priority=n % 2)

    @pl.when(valid)
    def _():
        words = [_load_tile_rows(x_ref, s) for s in range(SUBLANES)]
        lo = jnp.concatenate([_unpack_lo(wd).astype(BF16) for wd in words], axis=1)
        hi = jnp.concatenate([_unpack_hi(wd).astype(BF16) for wd in words], axis=1)
        half = lo.shape[1]
        h = (jnp.dot(lo, wbi_ref[:half], preferred_element_type=F32)
             + jnp.dot(hi, wbi_ref[half:], preferred_element_type=F32))
        hg, hu = h[:, :EXPERT_FF], h[:, EXPERT_FF:]
        act = (hg * jax.nn.sigmoid(hg) * hu).astype(BF16)
        _store_tile_rows(y_ref, _pack_rows(jnp.dot(act, wbo_ref[...], preferred_element_type=F32)))

    @pl.when(jnp.logical_not(valid))
    def _():
        y_ref[...] = jnp.zeros_like(y_ref)


def _experts(block_expert, block_valid, block_next, block_next2, block_slot, x_rows, e_w_in, e_w_out):
    n_rows = x_rows.shape[0] // SUBLANES
    nb = n_rows // MOE_BLOCK
    _, d, ff2 = e_w_in.shape
    ff = ff2 // 2
    rows_spec = pl.BlockSpec((MOE_BLOCK * SUBLANES, LANES), lambda b, *_: (b, 0))
    grid_spec = pltpu.PrefetchScalarGridSpec(
        num_scalar_prefetch=5,
        grid=(nb,),
        in_specs=[rows_spec,
                  pl.BlockSpec(memory_space=pl.ANY),
                  pl.BlockSpec(memory_space=pl.ANY)],
        out_specs=rows_spec,
        scratch_shapes=[pltpu.VMEM((2, d, ff2), F32), pltpu.VMEM((2, ff, d), F32),
                        pltpu.VMEM((d, ff2), BF16), pltpu.VMEM((ff, d), BF16),
                        pltpu.SemaphoreType.DMA((2, 2))],
    )
    return pl.pallas_call(
        _expert_kernel,
        out_shape=jax.ShapeDtypeStruct((n_rows * SUBLANES, LANES), jnp.uint32),
        grid_spec=grid_spec,
        compiler_params=_params(("arbitrary",)),
        name="routed_experts",
    )(block_expert, block_valid, block_next, block_next2, block_slot, x_rows, e_w_in, e_w_out)


COMBINE_TILE = 256


def _final_kernel(idx_ref, rank_ref, idxn_ref, rankn_ref, ps_ref, u2_ref, x1_ref, w_ref, ys_ref,
                  win_ref, wout_ref, g2_ref, lg_ref, lb_ref, o_ref, buf0_ref, buf1_ref, sem):
    tm = u2_ref.shape[0]
    i = pl.program_id(0)
    bufs = (buf0_ref, buf1_ref)

    def tile_row(r):
        return pl.ds(pl.multiple_of(r * SUBLANES, SUBLANES), SUBLANES)

    def gather(idx_r, rank_r, par):
        def issue(t, carry):
            for k in range(TOP_K):
                p = ps_ref[idx_r[k, t]] + rank_r[k, t]
                pltpu.make_async_copy(ys_ref.at[tile_row(p)], bufs[par].at[k, tile_row(t)],
                                      sem.at[par]).start(priority=k % 2)
            return carry

        lax.fori_loop(0, tm, issue, 0)

    @pl.when(i == 0)
    def _():
        gather(idx_ref, rank_ref, 0)

    for par in range(2):
        @pl.when((i + 1 < pl.num_programs(0)) & (i % 2 == par))
        def _():
            gather(idxn_ref, rankn_ref, 1 - par)

    h = jnp.dot(u2_ref[...], win_ref[...], preferred_element_type=F32)
    hg, hu = h[:, :SHARED_FF], h[:, SHARED_FF:]
    act = (hg * jax.nn.sigmoid(hg) * hu).astype(BF16)
    shared = jnp.dot(act, wout_ref[...], preferred_element_type=F32)

    w = w_ref[...]
    wk = [w[:, k:k + 1] for k in range(TOP_K)]

    for par in range(2):
        @pl.when(i % 2 == par)
        def _():
            buf = bufs[par]
            for k in range(TOP_K):
                pltpu.make_async_copy(ys_ref.at[pl.ds(0, tm * SUBLANES)], buf.at[k], sem.at[par]).wait()
            lo_parts, hi_parts = [], []
            for s in range(SUBLANES):
                lo = jnp.zeros((tm, LANES), F32)
                hi = jnp.zeros((tm, LANES), F32)
                for k in range(TOP_K):
                    words = _load_tile_rows(buf.at[k], s)
                    lo = lo + wk[k] * _unpack_lo(words)
                    hi = hi + wk[k] * _unpack_hi(words)
                lo_parts.append(lo)
                hi_parts.append(hi)
            ffn = shared + jnp.concatenate(lo_parts + hi_parts, axis=1)
            o_ref[...] = _layer_norm(DEEPNORM_ALPHA * x1_ref[...] + g2_ref[...] * ffn,
                                     lg_ref[...], lb_ref[...])


def _final(idx, rank, pad_start, u2, x1, w_tok, y_rows, s_w_in, s_w_out, gate2, ln_g, ln_b):
    l, d = x1.shape
    tm = COMBINE_TILE
    row = pl.BlockSpec((tm, d), lambda i: (i, 0))
    vec = pl.BlockSpec((1, d), lambda i: (0, 0))
    n_tiles = l // tm
    smem_tok = pl.BlockSpec((TOP_K, tm), lambda i: (0, i), memory_space=pltpu.SMEM)
    smem_next = pl.BlockSpec((TOP_K, tm), lambda i: (0, jnp.minimum(i + 1, n_tiles - 1)),
                             memory_space=pltpu.SMEM)
    row_buf = pltpu.VMEM((TOP_K, tm * SUBLANES, LANES), jnp.uint32)
    return pl.pallas_call(
        _final_kernel,
        out_shape=jax.ShapeDtypeStruct((l, d), F32),
        grid=(n_tiles,),
        in_specs=[smem_tok, smem_tok, smem_next, smem_next, pl.BlockSpec(memory_space=pltpu.SMEM),
                  row, row, pl.BlockSpec((tm, TOP_K), lambda i: (i, 0)),
                  pl.BlockSpec(memory_space=pl.ANY),
                  pl.BlockSpec(s_w_in.shape, lambda i: (0, 0)),
                  pl.BlockSpec(s_w_out.shape, lambda i: (0, 0)), vec, vec, vec],
        out_specs=row,
        scratch_shapes=[row_buf, row_buf, pltpu.SemaphoreType.DMA((2,))],
        compiler_params=_params(("arbitrary",)),
        name="combine_shared_ln2",
    )(idx, rank, idx, rank, pad_start, u2, x1, w_tok, y_rows, s_w_in, s_w_out, gate2, ln_g, ln_b)


def _block_layout(counts, n_tokens):
    padded = (counts + MOE_BLOCK - 1) // MOE_BLOCK * MOE_BLOCK
    pad_ends = jnp.cumsum(padded)
    pad_starts = (pad_ends - padded).astype(jnp.int32)
    n_rows = -(-(n_tokens * TOP_K + N_EXPERTS * (MOE_BLOCK - 1)) // MOE_BLOCK) * MOE_BLOCK
    block_start = jnp.arange(n_rows // MOE_BLOCK, dtype=jnp.int32) * MOE_BLOCK
    block_expert = jnp.minimum(jnp.sum((block_start[:, None] >= pad_ends[None, :]).astype(jnp.int32), axis=1),
                               N_EXPERTS - 1).astype(jnp.int32)
    block_valid = (block_start < pad_ends[-1]).astype(jnp.int32)
    ar = jnp.arange(N_EXPERTS, dtype=jnp.int32)
    has = counts > 0
    later = (ar[None, :] > ar[:, None]) & has[None, :]
    next_has = jnp.min(jnp.where(later, ar[None, :], N_EXPERTS), axis=1)
    next_has = jnp.where(next_has >= N_EXPERTS, -1, next_has)
    after = (ar[None, :] == next_has[:, None])
    next2_has = jnp.sum(jnp.where(after, next_has[None, :], 0), axis=1)
    next2_has = jnp.where(next_has < 0, -1, next2_has)
    ordinal = jnp.cumsum(has.astype(jnp.int32)) - 1
    mine = block_expert[:, None] == ar[None, :]
    pick = lambda v: jnp.sum(jnp.where(mine, v[None, :], 0), axis=1).astype(jnp.int32)
    return (pad_starts, pad_ends.astype(jnp.int32), block_expert, block_valid, pick(next_has),
            pick(next2_has), pick(ordinal) % 2, n_rows)


def kernel(x, c, w_ada, b_ada, w_in, rel_bias, ssm_a_re, ssm_a_im, ssm_log_dt, ssm_b_re, ssm_b_im, ssm_c_re, ssm_c_im, ssm_d, w_attn_out, w_ssm_glu, w_o, ln1_g, ln1_b, w_router, router_bias, e_w_in, e_w_out, s_w_in, s_w_out, ln2_g, ln2_b):
    bsz, l, d = x.shape
    assert bsz == 1
    xf = x.reshape(l, d)
    i = 0
    mod = _modulation(c, w_ada[i], b_ada[i])
    shift1, scale1, gate1, shift2, scale2, gate2 = [mod[:, k * d:(k + 1) * d] for k in range(6)]

    w_in_b = w_in[i].astype(BF16)
    u = _modulate(xf, scale1, shift1)
    s_in = _matmul(u, w_in_b, QKV_WIDTH, SSM_WIDTH, 512, 512, F32, "in_proj_ssm")

    outs, lses = [], []
    for gi, (window, dilation) in enumerate(DILATED_GROUPS):
        bias_prev, bias_cur = _attn_bias_tables(rel_bias, gi, window, dilation)
        qkv = _inproj_qkv_group(u, w_in_b, gi, dilation)
        o, s = _attention_group(qkv, bias_prev, bias_cur, gi)
        outs.append(o)
        lses.append(s)
    attn = _attn_combine(outs, lses)

    tabs = _ssm_tables(ssm_a_re[i], ssm_a_im[i], ssm_log_dt[i], ssm_b_re[i], ssm_b_im[i],
                       ssm_c_re[i], ssm_c_im[i])
    y_ssm = _ssm(s_in, tabs, ssm_d[i])

    merged = _merge(attn, y_ssm, u, w_attn_out[i].astype(BF16), w_ssm_glu[i].astype(BF16), w_in_b)
    x1, u2, u2p, scores_t = _outproj(merged, xf, w_o[i].astype(BF16), w_router[i].T.astype(BF16), gate1,
                                     ln1_g[i].reshape(1, d), ln1_b[i].reshape(1, d), scale2, shift2)

    idx, w, rank, counts = _route(scores_t, router_bias[i])
    (pad_start, pad_end, block_expert, block_valid, block_next, block_next2, block_slot,
     n_rows) = _block_layout(counts[:, 0].astype(jnp.int32), l)
    x_rows = _dispatch(idx, rank, pad_start, pad_end, u2p, n_rows)
    y_rows = _experts(block_expert, block_valid, block_next, block_next2, block_slot, x_rows,
                      e_w_in[i], e_w_out[i])
    out = _final(idx, rank, pad_start, u2, x1, w.T, y_rows, s_w_in[i].astype(BF16),
                 s_w_out[i].astype(BF16), gate2, ln2_g[i].reshape(1, d), ln2_b[i].reshape(1, d))
    return out.reshape(bsz, l, d)
```

```python
import functools
import math

import jax
import jax.numpy as jnp
from jax import lax
from jax.experimental import pallas as pl
from jax.experimental.pallas import tpu as pltpu

F32 = jnp.float32
BF16 = jnp.bfloat16

D_MODEL = 2048
HEAD_DIM = 128
HEADS_PER_GROUP = 4
DILATED_GROUPS = ((128, 1), (512, 4), (2048, 16))
N_ATTN_GROUPS = len(DILATED_GROUPS)
N_ATTN_HEADS = N_ATTN_GROUPS * HEADS_PER_GROUP
ATTN_WIDTH = N_ATTN_HEADS * HEAD_DIM
ATTN_OUT_WIDTH = HEADS_PER_GROUP * HEAD_DIM
ATTN_BLOCK = 128
REL_BUCKETS = 32
REL_MAX_DISTANCE = 2048
SSM_GROUP_CH = 16
SSM_STATE = 64
SSM_WIDTH = 1024
SSM_GROUPS = SSM_WIDTH // SSM_GROUP_CH
A_RE_MAX = -1e-4
QKV_WIDTH = 3 * ATTN_WIDTH
GATE_OFF = QKV_WIDTH + SSM_WIDTH
N_EXPERTS = 256
TOP_K = 8
N_EXPERT_GROUPS = 8
TOPK_GROUPS = 4
EXPERT_FF = 512
SHARED_FF = 512
ROUTED_SCALE = 2.5
MOE_BLOCK = 128
DEPTH = 1
DEEPNORM_ALPHA = (2 * DEPTH) ** 0.25
LN_EPS = 1e-5
NEG_BIG = -1e30
LANES = 128
SUBLANES = 8

SSM_CHUNK = 128
SSM_CHUNKS_PER_STEP = 2
SSM_BLK_GROUPS = 16
SSM_NBLK = SSM_GROUPS // SSM_BLK_GROUPS
SSM_BLK_CH = SSM_BLK_GROUPS * SSM_GROUP_CH
SSM_BLK_ST = SSM_BLK_GROUPS * SSM_STATE
SSM_STATES = SSM_GROUPS * SSM_STATE

VMEM_LIMIT = 56 * 1024 * 1024


def _params(sem, vmem=VMEM_LIMIT):
    return pltpu.CompilerParams(dimension_semantics=sem, vmem_limit_bytes=vmem)


def _mod_kernel(c_ref, w_ref, b_ref, o_ref):
    c = c_ref[...]
    cond = c * jax.nn.sigmoid(c)
    o_ref[...] = jnp.sum(cond * w_ref[...], axis=0, keepdims=True) + b_ref[...]


def _modulation(c, w_ada, b_ada):
    d, n = w_ada.shape
    assert c.shape == (1, d)
    tn = 1024
    return pl.pallas_call(
        _mod_kernel,
        out_shape=jax.ShapeDtypeStruct((1, n), F32),
        grid=(n // tn,),
        in_specs=[pl.BlockSpec((d, 1), lambda j: (0, 0)),
                  pl.BlockSpec((d, tn), lambda j: (0, j)),
                  pl.BlockSpec((1, tn), lambda j: (0, j))],
        out_specs=pl.BlockSpec((1, tn), lambda j: (0, j)),
        compiler_params=_params(("arbitrary",)),
        name="adaln_mod",
    )(c.reshape(d, 1), w_ada, b_ada.reshape(1, n))


def _modulate_kernel(x_ref, sc_ref, sh_ref, o_ref):
    o_ref[...] = (x_ref[...] * (1.0 + sc_ref[...]) + sh_ref[...]).astype(o_ref.dtype)


def _modulate(x, scale, shift):
    m, d = x.shape
    tm = 512
    return pl.pallas_call(
        _modulate_kernel,
        out_shape=jax.ShapeDtypeStruct((m, d), BF16),
        grid=(m // tm,),
        in_specs=[pl.BlockSpec((tm, d), lambda i: (i, 0)),
                  pl.BlockSpec((1, d), lambda i: (0, 0)),
                  pl.BlockSpec((1, d), lambda i: (0, 0))],
        out_specs=pl.BlockSpec((tm, d), lambda i: (i, 0)),
        compiler_params=_params(("arbitrary",)),
        name="modulate",
    )(x, scale, shift)


def _mm_kernel(a_ref, w_ref, o_ref):
    o_ref[...] = jnp.dot(a_ref[...], w_ref[...], preferred_element_type=F32).astype(o_ref.dtype)


def _matmul(a, w, col_off, n, tm, tn, out_dtype, name):
    m, k = a.shape
    assert col_off % tn == 0 and n % tn == 0 and m % tm == 0
    off_blocks = col_off // tn
    return pl.pallas_call(
        _mm_kernel,
        out_shape=jax.ShapeDtypeStruct((m, n), out_dtype),
        grid=(n // tn, m // tm),
        in_specs=[pl.BlockSpec((tm, k), lambda j, i: (i, 0)),
                  pl.BlockSpec((k, tn), lambda j, i: (0, j + off_blocks))],
        out_specs=pl.BlockSpec((tm, tn), lambda j, i: (i, j)),
        compiler_params=_params(("arbitrary", "arbitrary")),
        name=name,
    )(a, w)


ATTN_QBLOCKS = 2


def _attn_kernel(bp_ref, bc_ref, q_ref, kp_ref, kc_ref, vp_ref, vc_ref, o_ref, lse_ref):
    scale = HEAD_DIM ** -0.5
    nt = (((1,), (1,)), ((), ()))
    first = pl.program_id(1) == 0
    for h in range(HEADS_PER_GROUP):
        hs = slice(h * HEAD_DIM, (h + 1) * HEAD_DIM)
        for qb in range(ATTN_QBLOCKS):
            rows = slice(qb * ATTN_BLOCK, (qb + 1) * ATTN_BLOCK)
            before = slice((qb - 1) * ATTN_BLOCK, qb * ATTN_BLOCK)
            k_prev = kp_ref[:, hs] if qb == 0 else kc_ref[before, hs]
            v_prev = vp_ref[:, hs] if qb == 0 else vc_ref[before, hs]
            q = q_ref[rows, hs]
            s_p = lax.dot_general(q, k_prev, nt, preferred_element_type=F32) * scale + bp_ref[h]
            s_c = lax.dot_general(q, kc_ref[rows, hs], nt, preferred_element_type=F32) * scale + bc_ref[h]
            if qb == 0:
                s_p = jnp.where(first, NEG_BIG, s_p)
            m = jnp.maximum(jnp.max(s_p, axis=-1, keepdims=True), jnp.max(s_c, axis=-1, keepdims=True))
            p_p = jnp.exp(s_p - m)
            p_c = jnp.exp(s_c - m)
            l = jnp.sum(p_p, axis=-1, keepdims=True) + jnp.sum(p_c, axis=-1, keepdims=True)
            o = (jnp.dot(p_p.astype(BF16), v_prev, preferred_element_type=F32)
                 + jnp.dot(p_c.astype(BF16), vc_ref[rows, hs], preferred_element_type=F32))
            o_ref[rows, hs] = o / l
            lse_ref[rows, hs] = jnp.broadcast_to(m + jnp.log(l), (ATTN_BLOCK, HEAD_DIM))


def _inproj_dilated_kernel(a_ref, wq_ref, wk_ref, wv_ref, o_ref, acc_ref):
    dilation, rows, _ = o_ref.shape
    a = a_ref[...]
    for j, w_ref in enumerate((wq_ref, wk_ref, wv_ref)):
        tn = w_ref.shape[1]
        res = jnp.dot(a, w_ref[...], preferred_element_type=F32)
        if dilation == 1:
            o_ref[0, :, j * tn:(j + 1) * tn] = res.astype(o_ref.dtype)
            continue
        for c in range(tn // LANES):
            acc_ref[c] = res[:, c * LANES:(c + 1) * LANES]
        for r in range(dilation):
            for c in range(tn // LANES):
                o_ref[r, :, j * tn + c * LANES:j * tn + (c + 1) * LANES] = (
                    acc_ref[c, pl.ds(r, rows, stride=dilation), :].astype(o_ref.dtype))


def _inproj_qkv_group(u, w_in, gi, dilation):
    l, k = u.shape
    tm, tn = 512, ATTN_OUT_WIDTH
    w_spec = lambda j: pl.BlockSpec((k, tn), lambda i: (0, j * N_ATTN_GROUPS + gi))
    return pl.pallas_call(
        _inproj_dilated_kernel,
        out_shape=jax.ShapeDtypeStruct((dilation, l // dilation, 3 * tn), BF16),
        grid=(l // tm,),
        in_specs=[pl.BlockSpec((tm, k), lambda i: (i, 0)), w_spec(0), w_spec(1), w_spec(2)],
        out_specs=pl.BlockSpec((dilation, tm // dilation, 3 * tn), lambda i: (0, i, 0)),
        scratch_shapes=[pltpu.VMEM((tn // LANES, tm, LANES), F32)],
        compiler_params=_params(("arbitrary",)),
        name=f"in_proj_qkv_g{gi}",
    )(u, w_in, w_in, w_in)


def _attention_group(qkv, bias_prev, bias_cur, gi):
    dilation, m, _ = qkv.shape
    tile = ATTN_QBLOCKS * ATTN_BLOCK

    def cur(cb):
        return pl.BlockSpec((None, tile, ATTN_OUT_WIDTH), lambda r, b: (r, b, cb))

    def prev(cb):
        return pl.BlockSpec((None, ATTN_BLOCK, ATTN_OUT_WIDTH),
                            lambda r, b: (r, jnp.maximum(b * ATTN_QBLOCKS - 1, 0), cb))

    bias_spec = pl.BlockSpec((HEADS_PER_GROUP, ATTN_BLOCK, ATTN_BLOCK), lambda r, b: (0, 0, 0))
    out_spec = pl.BlockSpec((None, tile, ATTN_OUT_WIDTH), lambda r, b: (r, b, 0))
    return pl.pallas_call(
        _attn_kernel,
        out_shape=(jax.ShapeDtypeStruct((dilation, m, ATTN_OUT_WIDTH), F32),
                   jax.ShapeDtypeStruct((dilation, m, ATTN_OUT_WIDTH), F32)),
        grid=(dilation, m // tile),
        in_specs=[bias_spec, bias_spec, cur(0), prev(1), cur(1), prev(2), cur(2)],
        out_specs=(out_spec, out_spec),
        compiler_params=_params(("arbitrary", "arbitrary")),
        name=f"dilated_attn_g{gi}",
    )(bias_prev, bias_cur, qkv, qkv, qkv, qkv, qkv)


def _t5_bucket(dist):
    max_exact = REL_BUCKETS // 2
    d_f = jnp.maximum(dist, 1).astype(F32)
    large = max_exact + (jnp.log(d_f / max_exact) / math.log(REL_MAX_DISTANCE / max_exact)
                         * (REL_BUCKETS - max_exact)).astype(jnp.int32)
    large = jnp.minimum(large, REL_BUCKETS - 1)
    return jnp.where(dist < max_exact, dist, large)


def _attn_bias_tables(rel_bias, gi, window, dilation):
    n_keys = window // dilation
    bucket = _t5_bucket(jnp.arange(n_keys + 1, dtype=jnp.int32) * dilation)
    hs = slice(gi * HEADS_PER_GROUP, (gi + 1) * HEADS_PER_GROUP)
    hi = lax.Precision.HIGHEST
    pick_bucket = (bucket[:, None] == jnp.arange(REL_BUCKETS)[None, :]).astype(F32)
    by_dist = jnp.dot(pick_bucket, rel_bias[:, hs].astype(F32), precision=hi)
    qi = jnp.arange(ATTN_BLOCK)[:, None]
    kj = jnp.arange(ATTN_BLOCK)[None, :]
    d_prev = qi + ATTN_BLOCK - kj
    d_cur = qi - kj
    tabs = []
    for dist in (d_prev, d_cur):
        ok = (dist >= 0) & (dist <= n_keys)
        pick_dist = (dist[:, :, None] == jnp.arange(n_keys + 1)[None, None, :]).astype(F32)
        vals = jnp.einsum('ijd,dh->hij', pick_dist, by_dist, precision=hi)
        tabs.append(jnp.where(ok[None], vals, NEG_BIG))
    return tabs


ATTN_COMBINE_TILE = 512


def _attn_combine_kernel(o1, o2, o3, l1, l2, l3, out_ref, *scratch):
    def token_order(ref, buf, c):
        dilation, rows, _ = ref.shape
        cs = slice(c * LANES, (c + 1) * LANES)
        if dilation == 1:
            return ref[0, :, cs]
        for r in range(dilation):
            buf[c, pl.ds(r, rows, stride=dilation), :] = ref[r, :, cs]
        return buf[c]

    for c in range(out_ref.shape[1] // LANES):
        v1, v2, v3, a1, a2, a3 = [token_order(ref, buf, c)
                                  for ref, buf in zip((o1, o2, o3, l1, l2, l3), scratch)]
        m = jnp.maximum(jnp.maximum(a1, a2), a3)
        e1, e2, e3 = jnp.exp(a1 - m), jnp.exp(a2 - m), jnp.exp(a3 - m)
        num = e1 * v1 + e2 * v2 + e3 * v3
        out_ref[:, c * LANES:(c + 1) * LANES] = (num / (e1 + e2 + e3)).astype(out_ref.dtype)


def _attn_combine(outs, lses):
    w = outs[0].shape[-1]
    l = outs[0].shape[0] * outs[0].shape[1]
    tm = ATTN_COMBINE_TILE

    def spec(a):
        dilation = a.shape[0]
        return pl.BlockSpec((dilation, tm // dilation, w), lambda i: (0, i, 0))

    return pl.pallas_call(
        _attn_combine_kernel,
        out_shape=jax.ShapeDtypeStruct((l, w), BF16),
        grid=(l // tm,),
        in_specs=[spec(a) for a in (*outs, *lses)],
        out_specs=pl.BlockSpec((tm, w), lambda i: (i, 0)),
        scratch_shapes=[pltpu.VMEM((w // LANES, tm, LANES), F32)] * 6,
        compiler_params=_params(("arbitrary",)),
        name="attn_combine",
    )(*outs, *lses)


def _ssm_kernel(u_ref, bre_ref, bim_ref, cre_ref, cim_ref, enr_ref, eni_ref, epr_ref, epi_ref,
                lamr_ref, lami_ref, d_ref, tri_ref, y_ref, car_ref, cai_ref):
    @pl.when(pl.program_id(0) == 0)
    def _():
        car_ref[...] = jnp.zeros_like(car_ref)
        cai_ref[...] = jnp.zeros_like(cai_ref)

    tc = tri_ref.shape[0]
    tri = tri_ref[...]
    for j in range(SSM_NBLK):
        ch = slice(j * SSM_BLK_CH, (j + 1) * SSM_BLK_CH)
        st = slice(j * SSM_BLK_ST, (j + 1) * SSM_BLK_ST)
        u = u_ref[:, ch]
        ub = u.astype(BF16)
        bur = jnp.dot(ub, bre_ref[j], preferred_element_type=F32)
        bui = jnp.dot(ub, bim_ref[j], preferred_element_type=F32)
        enr, eni = enr_ref[:, st], eni_ref[:, st]
        epr, epi = epr_ref[:, st], epi_ref[:, st]
        lr, li = lamr_ref[:, st], lami_ref[:, st]
        cr, ci = car_ref[:, st], cai_ref[:, st]
        sr_chunks, si_chunks = [], []
        for q in range(u_ref.shape[0] // tc):
            rows = slice(q * tc, (q + 1) * tc)
            xr = bur[rows] * enr - bui[rows] * eni
            xi = bur[rows] * eni + bui[rows] * enr
            pr = jnp.dot(tri, xr.astype(BF16), preferred_element_type=F32)
            pi = jnp.dot(tri, xi.astype(BF16), preferred_element_type=F32)
            tr = pr + (lr * cr - li * ci)
            ti = pi + (lr * ci + li * cr)
            sr = epr * tr - epi * ti
            si = epr * ti + epi * tr
            cr, ci = sr[tc - 1:tc], si[tc - 1:tc]
            sr_chunks.append(sr.astype(BF16))
            si_chunks.append(si.astype(BF16))
        car_ref[:, st] = cr
        cai_ref[:, st] = ci
        y = (jnp.dot(jnp.concatenate(sr_chunks, axis=0), cre_ref[j], preferred_element_type=F32)
             - jnp.dot(jnp.concatenate(si_chunks, axis=0), cim_ref[j], preferred_element_type=F32))
        y_ref[:, ch] = (y + d_ref[:, ch] * u).astype(y_ref.dtype)


def _ssm_tables(a_re, a_im, log_dt, b_re, b_im, c_re, c_im):
    g, p, hc = b_re.shape
    lam_re = jnp.minimum(a_re.astype(F32), A_RE_MAX)
    lam_im = a_im.astype(F32)
    dt = jnp.exp(log_dt.astype(F32))[:, None]
    mag = jnp.exp(lam_re * dt)
    lb_re, lb_im = mag * jnp.cos(lam_im * dt), mag * jnp.sin(lam_im * dt)
    imag = jnp.exp(-lam_re * dt)
    li_re, li_im = imag * jnp.cos(lam_im * dt), -imag * jnp.sin(lam_im * dt)
    den = lam_re * lam_re + lam_im * lam_im
    nr, ni = lb_re - 1.0, lb_im
    f_re = (nr * lam_re + ni * lam_im) / den
    f_im = (ni * lam_re - nr * lam_im) / den
    bb_re = f_re[:, :, None] * b_re - f_im[:, :, None] * b_im
    bb_im = f_re[:, :, None] * b_im + f_im[:, :, None] * b_re

    def powers(pr, pi):
        er, ei = jnp.ones((1, g * p), F32), jnp.zeros((1, g * p), F32)
        pr, pi = pr.reshape(1, g * p), pi.reshape(1, g * p)
        while er.shape[0] < SSM_CHUNK:
            er, ei = (jnp.concatenate([er, er * pr - ei * pi], axis=0),
                      jnp.concatenate([ei, er * pi + ei * pr], axis=0))
            pr, pi = pr * pr - pi * pi, 2.0 * pr * pi
        return er, ei

    epr, epi = powers(lb_re, lb_im)
    enr, eni = powers(li_re, li_im)
    eye = jnp.eye(SSM_BLK_GROUPS, dtype=F32)

    def b_blocks(b):
        b = b.reshape(SSM_NBLK, SSM_BLK_GROUPS, p, hc)
        return jnp.einsum('jgph,gk->jghkp', b, eye).reshape(SSM_NBLK, SSM_BLK_CH, SSM_BLK_ST).astype(BF16)

    def c_blocks(c):
        c = c.astype(F32).reshape(SSM_NBLK, SSM_BLK_GROUPS, hc, p)
        return jnp.einsum('jghp,gk->jkpgh', c, eye).reshape(SSM_NBLK, SSM_BLK_ST, SSM_BLK_CH).astype(BF16)

    return dict(bre=b_blocks(bb_re), bim=b_blocks(bb_im), cre=c_blocks(c_re), cim=c_blocks(c_im),
                enr=enr, eni=eni, epr=epr, epi=epi,
                lamr=lb_re.reshape(1, g * p), lami=lb_im.reshape(1, g * p))


def _ssm(s_in, tabs, d_skip):
    l, w = s_in.shape
    tc = SSM_CHUNK
    tri = jnp.tril(jnp.ones((tc, tc), F32)).astype(BF16)
    full = lambda shape: pl.BlockSpec(shape, lambda c: (0,) * len(shape))
    rows = tc * SSM_CHUNKS_PER_STEP
    return pl.pallas_call(
        _ssm_kernel,
        out_shape=jax.ShapeDtypeStruct((l, w), BF16),
        grid=(l // rows,),
        in_specs=[pl.BlockSpec((rows, w), lambda c: (c, 0)),
                  full((SSM_NBLK, SSM_BLK_CH, SSM_BLK_ST)), full((SSM_NBLK, SSM_BLK_CH, SSM_BLK_ST)),
                  full((SSM_NBLK, SSM_BLK_ST, SSM_BLK_CH)), full((SSM_NBLK, SSM_BLK_ST, SSM_BLK_CH)),
                  full((tc, SSM_STATES)), full((tc, SSM_STATES)),
                  full((tc, SSM_STATES)), full((tc, SSM_STATES)),
                  full((1, SSM_STATES)), full((1, SSM_STATES)),
                  full((1, w)), full((tc, tc))],
        out_specs=pl.BlockSpec((rows, w), lambda c: (c, 0)),
        scratch_shapes=[pltpu.VMEM((1, SSM_STATES), F32), pltpu.VMEM((1, SSM_STATES), F32)],
        compiler_params=_params(("arbitrary",)),
        name="s5_ssm",
    )(s_in, tabs['bre'], tabs['bim'], tabs['cre'], tabs['cim'], tabs['enr'], tabs['eni'],
      tabs['epr'], tabs['epi'], tabs['lamr'], tabs['lami'], d_skip.reshape(1, w).astype(F32), tri)


def _merge_kernel(attn_ref, y_ref, u_ref, wa_ref, wg1_ref, wg2_ref, wia_ref, wis_ref, o_ref):
    a = jnp.dot(attn_ref[...], wa_ref[...], preferred_element_type=F32)
    y = y_ref[...]
    s = (jnp.dot(y, wg1_ref[...], preferred_element_type=F32)
         * jax.nn.sigmoid(jnp.dot(y, wg2_ref[...], preferred_element_type=F32)))
    u = u_ref[...]
    ga = jax.nn.sigmoid(jnp.dot(u, wia_ref[...], preferred_element_type=F32))
    gs = jax.nn.sigmoid(jnp.dot(u, wis_ref[...], preferred_element_type=F32))
    o_ref[...] = (ga * a + gs * s).astype(o_ref.dtype)


def _merge(attn, y_ssm, u, w_attn_out, w_ssm_glu, w_in):
    l, d = u.shape
    tm, tn = 512, 512
    nd = d // tn
    go = GATE_OFF // tn
    return pl.pallas_call(
        _merge_kernel,
        out_shape=jax.ShapeDtypeStruct((l, d), BF16),
        grid=(nd, l // tm),
        in_specs=[pl.BlockSpec((tm, ATTN_OUT_WIDTH), lambda j, i: (i, 0)),
                  pl.BlockSpec((tm, SSM_WIDTH), lambda j, i: (i, 0)),
                  pl.BlockSpec((tm, d), lambda j, i: (i, 0)),
                  pl.BlockSpec((ATTN_OUT_WIDTH, tn), lambda j, i: (0, j)),
                  pl.BlockSpec((SSM_WIDTH, tn), lambda j, i: (0, j)),
                  pl.BlockSpec((SSM_WIDTH, tn), lambda j, i: (0, j + nd)),
                  pl.BlockSpec((d, tn), lambda j, i: (0, j + go)),
                  pl.BlockSpec((d, tn), lambda j, i: (0, j + go + nd))],
        out_specs=pl.BlockSpec((tm, tn), lambda j, i: (i, j)),
        compiler_params=_params(("arbitrary", "arbitrary")),
        name="branch_merge",
    )(attn, y_ssm, u, w_attn_out, w_ssm_glu, w_ssm_glu, w_in, w_in)


def _layer_norm(h, g, b):
    mu = jnp.mean(h, axis=-1, keepdims=True)
    c = h - mu
    var = jnp.mean(c * c, axis=-1, keepdims=True)
    return c * lax.rsqrt(var + LN_EPS) * g + b


def _pack_rows(v):
    bits = lax.bitcast_convert_type(v.astype(BF16).astype(F32), jnp.uint32)
    half = bits.shape[1] // 2
    return (bits[:, :half] >> 16) | (bits[:, half:] & jnp.uint32(0xFFFF0000))


def _unpack_lo(xp):
    return lax.bitcast_convert_type(xp << 16, F32)


def _unpack_hi(xp):
    return lax.bitcast_convert_type(xp & jnp.uint32(0xFFFF0000), F32)


def _store_tile_rows(ref, packed):
    rows = packed.shape[0]
    for s in range(SUBLANES):
        ref[pl.ds(s, rows, stride=SUBLANES), :] = packed[:, s * LANES:(s + 1) * LANES]


def _load_tile_rows(ref, s):
    return ref[pl.ds(s, ref.shape[0] // SUBLANES, stride=SUBLANES), :]


def _outproj_kernel(mg_ref, x_ref, wo_ref, wr_ref, g1_ref, lg_ref, lb_ref, sc_ref, sh_ref,
                    x1_ref, u2_ref, u2p_ref, sco_ref):
    mix = jnp.dot(mg_ref[...], wo_ref[...], preferred_element_type=F32)
    x1 = _layer_norm(DEEPNORM_ALPHA * x_ref[...] + g1_ref[...] * mix, lg_ref[...], lb_ref[...])
    x1_ref[...] = x1
    u2 = (x1 * (1.0 + sc_ref[...]) + sh_ref[...]).astype(BF16)
    u2_ref[...] = u2
    _store_tile_rows(u2p_ref, _pack_rows(u2))
    logits = lax.dot_general(wr_ref[...], u2, (((1,), (1,)), ((), ())), preferred_element_type=F32)
    sco_ref[...] = jax.nn.sigmoid(logits)


def _outproj(merged, x, w_o, w_router_t, gate1, ln_g, ln_b, scale2, shift2):
    l, d = x.shape
    assert d == 2 * SUBLANES * LANES
    e = w_router_t.shape[0]
    tm = 512
    row = lambda w: pl.BlockSpec((tm, w), lambda i: (i, 0))
    vec = pl.BlockSpec((1, d), lambda i: (0, 0))
    return pl.pallas_call(
        _outproj_kernel,
        out_shape=(jax.ShapeDtypeStruct((l, d), F32), jax.ShapeDtypeStruct((l, d), BF16),
                   jax.ShapeDtypeStruct((l * SUBLANES, LANES), jnp.uint32),
                   jax.ShapeDtypeStruct((e, l), F32)),
        grid=(l // tm,),
        in_specs=[row(d), row(d), pl.BlockSpec((d, d), lambda i: (0, 0)),
                  pl.BlockSpec((e, d), lambda i: (0, 0)), vec, vec, vec, vec, vec],
        out_specs=(row(d), row(d), pl.BlockSpec((tm * SUBLANES, LANES), lambda i: (i, 0)),
                   pl.BlockSpec((e, tm), lambda i: (0, i))),
        compiler_params=_params(("arbitrary",)),
        name="outproj_ln1_router",
    )(merged, x, w_o, w_router_t, gate1, ln_g, ln_b, scale2, shift2)


ROUTE_TILE = 512
EXPERTS_PER_GROUP = N_EXPERTS // N_EXPERT_GROUPS


def _route_kernel(s_ref, b_ref, tri_ref, idx_ref, w_ref, rank_ref, cnt_ref, carry_ref):
    @pl.when(pl.program_id(0) == 0)
    def _():
        carry_ref[...] = jnp.zeros_like(carry_ref)

    ne, tm = s_ref.shape
    neg_inf = -jnp.inf
    s = s_ref[...]
    sel = s + b_ref[...]
    e_iota = lax.broadcasted_iota(jnp.int32, (ne, tm), 0)

    gs_rows = []
    for g in range(N_EXPERT_GROUPS):
        xg = sel[g * EXPERTS_PER_GROUP:(g + 1) * EXPERTS_PER_GROUP]
        m1 = jnp.max(xg, axis=0, keepdims=True)
        n1 = jnp.sum((xg == m1).astype(F32), axis=0, keepdims=True)
        m2 = jnp.max(jnp.where(xg < m1, xg, neg_inf), axis=0, keepdims=True)
        gs_rows.append(m1 + jnp.where(n1 >= 2.0, m1, m2))
    gs = jnp.concatenate(gs_rows, axis=0)

    g_iota = lax.broadcasted_iota(jnp.int32, gs.shape, 0)
    beaten = jnp.zeros(gs.shape, jnp.int32)
    for g2 in range(N_EXPERT_GROUPS):
        row = gs[g2:g2 + 1]
        beats = (row > gs) | ((row == gs) & (g2 < g_iota))
        beaten = beaten + beats.astype(jnp.int32)
    g_ok = beaten < TOPK_GROUPS
    work = jnp.concatenate(
        [jnp.where(g_ok[g:g + 1], sel[g * EXPERTS_PER_GROUP:(g + 1) * EXPERTS_PER_GROUP], neg_inf)
         for g in range(N_EXPERT_GROUPS)], axis=0)

    idxs, vals = [], []
    chosen = jnp.zeros((ne, tm), F32)
    for _ in range(TOP_K):
        m = jnp.max(work, axis=0, keepdims=True)
        i = jnp.min(jnp.where(work == m, e_iota, ne), axis=0, keepdims=True)
        onehot = e_iota == i
        idxs.append(i)
        vals.append(jnp.sum(jnp.where(onehot, s, 0.0), axis=0, keepdims=True))
        chosen = jnp.where(onehot, 1.0, chosen)
        work = jnp.where(onehot, neg_inf, work)
    wsum = vals[0]
    for v in vals[1:]:
        wsum = wsum + v

    before = jnp.dot(chosen.astype(BF16), tri_ref[...], preferred_element_type=F32) + carry_ref[...]
    ranks = [jnp.sum(jnp.where(e_iota == i, before, 0.0), axis=0, keepdims=True) for i in idxs]
    carry_ref[...] = carry_ref[...] + jnp.sum(chosen, axis=1, keepdims=True)

    idx_ref[...] = jnp.concatenate(idxs, axis=0)
    w_ref[...] = jnp.concatenate([v / wsum * ROUTED_SCALE for v in vals], axis=0)
    rank_ref[...] = jnp.concatenate(ranks, axis=0).astype(jnp.int32)
    cnt_ref[...] = carry_ref[...]


def _route(scores_t, router_bias):
    ne, t = scores_t.shape
    tm = ROUTE_TILE
    tri = jnp.triu(jnp.ones((tm, tm), F32), k=1).astype(BF16)
    tok = pl.BlockSpec((TOP_K, tm), lambda i: (0, i))
    return pl.pallas_call(
        _route_kernel,
        out_shape=(jax.ShapeDtypeStruct((TOP_K, t), jnp.int32), jax.ShapeDtypeStruct((TOP_K, t), F32),
                   jax.ShapeDtypeStruct((TOP_K, t), jnp.int32), jax.ShapeDtypeStruct((ne, 1), F32)),
        grid=(t // tm,),
        in_specs=[pl.BlockSpec((ne, tm), lambda i: (0, i)),
                  pl.BlockSpec((ne, 1), lambda i: (0, 0)),
                  pl.BlockSpec((tm, tm), lambda i: (0, 0))],
        out_specs=(tok, tok, tok, pl.BlockSpec((ne, 1), lambda i: (0, 0))),
        scratch_shapes=[pltpu.VMEM((ne, 1), F32)],
        compiler_params=_params(("arbitrary",)),
        name="moe_route",
    )(scores_t, router_bias.astype(F32).reshape(ne, 1), tri)


def _positions_kernel(ps_ref, idx_ref, rank_ref, pos_ref):
    idx = idx_ref[...]

    def add_expert(e, acc):
        return acc + jnp.where(idx == e, ps_ref[e], 0)

    start = lax.fori_loop(0, ps_ref.shape[0], add_expert, jnp.zeros(idx.shape, jnp.int32))
    pos_ref[...] = start + rank_ref[...]


def _positions(idx, rank, pad_start):
    k, t = idx.shape
    tm = 1024
    tok = pl.BlockSpec((k, tm), lambda i: (0, i))
    return pl.pallas_call(
        _positions_kernel,
        out_shape=jax.ShapeDtypeStruct((k, t), jnp.int32),
        grid=(t // tm,),
        in_specs=[pl.BlockSpec(memory_space=pltpu.SMEM), tok, tok],
        out_specs=tok,
        compiler_params=_params(("arbitrary",)),
        name="moe_positions",
    )(pad_start, idx, rank)


DISPATCH_TILE = 256


def _dispatch_kernel(pos_ref, ps_ref, pe_ref, u_ref, xs_ref, zero_ref, sem, zsem):
    tm = u_ref.shape[0] // SUBLANES
    ne = ps_ref.shape[0]
    block_words = MOE_BLOCK * SUBLANES

    def tile_row(r):
        return pl.ds(pl.multiple_of(r * SUBLANES, SUBLANES), SUBLANES)

    @pl.when(pl.program_id(0) == 0)
    def _():
        zero_ref[...] = jnp.zeros_like(zero_ref)

        def block_copy(start):
            start = pl.multiple_of(start * SUBLANES, block_words)
            return pltpu.make_async_copy(zero_ref, xs_ref.at[pl.ds(start, block_words)], zsem)

        def fill(e, carry):
            @pl.when(pe_ref[e] > ps_ref[e])
            def _():
                block_copy(pe_ref[e] - MOE_BLOCK).start()
            return carry

        def fill_done(e, carry):
            @pl.when(pe_ref[e] > ps_ref[e])
            def _():
                block_copy(pe_ref[e] - MOE_BLOCK).wait()
            return carry

        lax.fori_loop(0, ne, fill, 0)
        lax.fori_loop(0, ne, fill_done, 0)

        first_unused = pe_ref[ne - 1] // MOE_BLOCK
        n_blocks = xs_ref.shape[0] // block_words

        def fill_unused(b, carry):
            block_copy(b * MOE_BLOCK).start()
            return carry

        def fill_unused_done(b, carry):
            block_copy(b * MOE_BLOCK).wait()
            return carry

        lax.fori_loop(first_unused, n_blocks, fill_unused, 0)
        lax.fori_loop(first_unused, n_blocks, fill_unused_done, 0)

    def issue(t, carry):
        for k in range(TOP_K):
            pltpu.make_async_copy(u_ref.at[tile_row(t)], xs_ref.at[tile_row(pos_ref[k, t])], sem).start(
                priority=k % 2)
        return carry

    lax.fori_loop(0, tm, issue, 0)
    for _ in range(TOP_K):
        pltpu.make_async_copy(u_ref, xs_ref.at[pl.ds(0, tm * SUBLANES)], sem).wait()


def _dispatch(pos, pad_start, pad_end, u2p, n_rows):
    t = u2p.shape[0] // SUBLANES
    tm = DISPATCH_TILE
    smem_tok = pl.BlockSpec((TOP_K, tm), lambda i: (0, i), memory_space=pltpu.SMEM)
    smem_all = pl.BlockSpec(memory_space=pltpu.SMEM)
    return pl.pallas_call(
        _dispatch_kernel,
        out_shape=jax.ShapeDtypeStruct((n_rows * SUBLANES, LANES), jnp.uint32),
        grid=(t // tm,),
        in_specs=[smem_tok, smem_all, smem_all,
                  pl.BlockSpec((tm * SUBLANES, LANES), lambda i: (i, 0))],
        out_specs=pl.BlockSpec(memory_space=pl.ANY),
        scratch_shapes=[pltpu.VMEM((MOE_BLOCK * SUBLANES, LANES), jnp.uint32),
                        pltpu.SemaphoreType.DMA, pltpu.SemaphoreType.DMA],
        compiler_params=_params(("arbitrary",)),
        name="moe_dispatch",
    )(pos, pad_start, pad_end, u2p)


WEIGHT_CHUNK_BYTES = 2 * 1024 * 1024


def _expert_kernel(be_ref, bv_ref, nx_ref, nx2_ref, sl_ref, x_ref, win_ref, wout_ref, y_ref,
                   wfi_ref, wfo_ref, wbi_ref, wbo_ref, sem):
    b = pl.program_id(0)
    valid = bv_ref[b] > 0
    e = be_ref[b]
    slot = sl_ref[b]
    new_expert = (b == 0) | (e != be_ref[jnp.maximum(b - 1, 0)])

    def weight_chunks():
        chunks = []
        for hbm, land, wb, j in ((win_ref, wfi_ref, wbi_ref, 0), (wout_ref, wfo_ref, wbo_ref, 1)):
            rows = WEIGHT_CHUNK_BYTES // (hbm.shape[2] * 4)
            for c in range(hbm.shape[1] // rows):
                chunks.append((hbm, land, wb, j, pl.ds(c * rows, rows)))
        return chunks

    def chunk_copy(chunk, ex, s):
        hbm, land, _, j, rs = chunk
        return pltpu.make_async_copy(hbm.at[ex, rs], land.at[s, rs], sem.at[s, j])

    def start_weights(ex, s):
        for n, chunk in enumerate(weight_chunks()):
            chunk_copy(chunk, ex, s).start(priority=n % 2)

    @pl.when(b == 0)
    def _():
        start_weights(e, slot)

        @pl.when(nx_ref[b] >= 0)
        def _():
            start_weights(nx_ref[b], 1 - slot)

    @pl.when(valid & new_expert)
    def _():
        for chunk in weight_chunks():
            chunk_copy(chunk, e, slot).wait()
        for n, chunk in enumerate(weight_chunks()):
            _, land, wb, _, rs = chunk
            for s in range(2):
                @pl.when(slot == s)
                def _():
                    wb[rs] = land[s, rs].astype(BF16)

            @pl.when(nx2_ref[b] >= 0)
            def _():
                chunk_copy(chunk, nx2_ref[b], slot).start(priority=n % 2)

    @pl.when(valid)
    def _():
        words = [_load_tile_rows(x_ref, s) for s in range(SUBLANES)]
        lo = jnp.concatenate([_unpack_lo(wd).astype(BF16) for wd in words], axis=1)
        hi = jnp.concatenate([_unpack_hi(wd).astype(BF16) for wd in words], axis=1)
        half = lo.shape[1]
        h = (jnp.dot(lo, wbi_ref[:half], preferred_element_type=F32)
             + jnp.dot(hi, wbi_ref[half:], preferred_element_type=F32))
        hg, hu = h[:, :EXPERT_FF], h[:, EXPERT_FF:]
        act = (hg * jax.nn.sigmoid(hg) * hu).astype(BF16)
        _store_tile_rows(y_ref, _pack_rows(jnp.dot(act, wbo_ref[...], preferred_element_type=F32)))

    @pl.when(jnp.logical_not(valid))
    def _():
        y_ref[...] = jnp.zeros_like(y_ref)


def _experts(block_expert, block_valid, block_next, block_next2, block_slot, x_rows, e_w_in, e_w_out):
    n_rows = x_rows.shape[0] // SUBLANES
    nb = n_rows // MOE_BLOCK
    _, d, ff2 = e_w_in.shape
    ff = ff2 // 2
    rows_spec = pl.BlockSpec((MOE_BLOCK * SUBLANES, LANES), lambda b, *_: (b, 0))
    grid_spec = pltpu.PrefetchScalarGridSpec(
        num_scalar_prefetch=5,
        grid=(nb,),
        in_specs=[rows_spec,
                  pl.BlockSpec(memory_space=pl.ANY),
                  pl.BlockSpec(memory_space=pl.ANY)],
        out_specs=rows_spec,
        scratch_shapes=[pltpu.VMEM((2, d, ff2), F32), pltpu.VMEM((2, ff, d), F32),
                        pltpu.VMEM((d, ff2), BF16), pltpu.VMEM((ff, d), BF16),
                        pltpu.SemaphoreType.DMA((2, 2))],
    )
    return pl.pallas_call(
        _expert_kernel,
        out_shape=jax.ShapeDtypeStruct((n_rows * SUBLANES, LANES), jnp.uint32),
        grid_spec=grid_spec,
        compiler_params=_params(("arbitrary",)),
        name="routed_experts",
    )(block_expert, block_valid, block_next, block_next2, block_slot, x_rows, e_w_in, e_w_out)


COMBINE_TILE = 256


def _final_kernel(pos_ref, posn_ref, u2_ref, x1_ref, w_ref, ys_ref,
                  win_ref, wout_ref, g2_ref, lg_ref, lb_ref, o_ref, buf0_ref, buf1_ref, sem):
    tm = u2_ref.shape[0]
    i = pl.program_id(0)
    bufs = (buf0_ref, buf1_ref)

    def tile_row(r):
        return pl.ds(pl.multiple_of(r * SUBLANES, SUBLANES), SUBLANES)

    def gather(slots_ref, par):
        def issue(t, carry):
            for k in range(TOP_K):
                pltpu.make_async_copy(ys_ref.at[tile_row(slots_ref[k, t])],
                                      bufs[par].at[k, tile_row(t)],
                                      sem.at[par]).start(priority=k % 2)
            return carry

        lax.fori_loop(0, tm, issue, 0)

    @pl.when(i == 0)
    def _():
        gather(pos_ref, 0)

    for par in range(2):
        @pl.when((i + 1 < pl.num_programs(0)) & (i % 2 == par))
        def _():
            gather(posn_ref, 1 - par)

    h = jnp.dot(u2_ref[...], win_ref[...], preferred_element_type=F32)
    hg, hu = h[:, :SHARED_FF], h[:, SHARED_FF:]
    act = (hg * jax.nn.sigmoid(hg) * hu).astype(BF16)
    shared = jnp.dot(act, wout_ref[...], preferred_element_type=F32)

    w = w_ref[...]
    wk = [w[:, k:k + 1] for k in range(TOP_K)]

    for par in range(2):
        @pl.when(i % 2 == par)
        def _():
            buf = bufs[par]
            for k in range(TOP_K):
                pltpu.make_async_copy(ys_ref.at[pl.ds(0, tm * SUBLANES)], buf.at[k], sem.at[par]).wait()
            lo_parts, hi_parts = [], []
            for s in range(SUBLANES):
                lo = jnp.zeros((tm, LANES), F32)
                hi = jnp.zeros((tm, LANES), F32)
                for k in range(TOP_K):
                    words = _load_tile_rows(buf.at[k], s)
                    lo = lo + wk[k] * _unpack_lo(words)
                    hi = hi + wk[k] * _unpack_hi(words)
                lo_parts.append(lo)
                hi_parts.append(hi)
            ffn = shared + jnp.concatenate(lo_parts + hi_parts, axis=1)
            o_ref[...] = _layer_norm(DEEPNORM_ALPHA * x1_ref[...] + g2_ref[...] * ffn,
                                     lg_ref[...], lb_ref[...])


def _final(pos, u2, x1, w_tok, y_rows, s_w_in, s_w_out, gate2, ln_g, ln_b):
    l, d = x1.shape
    tm = COMBINE_TILE
    row = pl.BlockSpec((tm, d), lambda i: (i, 0))
    vec = pl.BlockSpec((1, d), lambda i: (0, 0))
    n_tiles = l // tm
    smem_tok = pl.BlockSpec((TOP_K, tm), lambda i: (0, i), memory_space=pltpu.SMEM)
    smem_next = pl.BlockSpec((TOP_K, tm), lambda i: (0, jnp.minimum(i + 1, n_tiles - 1)),
                             memory_space=pltpu.SMEM)
    row_buf = pltpu.VMEM((TOP_K, tm * SUBLANES, LANES), jnp.uint32)
    return pl.pallas_call(
        _final_kernel,
        out_shape=jax.ShapeDtypeStruct((l, d), F32),
        grid=(n_tiles,),
        in_specs=[smem_tok, smem_next, row, row, pl.BlockSpec((tm, TOP_K), lambda i: (i, 0)),
                  pl.BlockSpec(memory_space=pl.ANY),
                  pl.BlockSpec(s_w_in.shape, lambda i: (0, 0)),
                  pl.BlockSpec(s_w_out.shape, lambda i: (0, 0)), vec, vec, vec],
        out_specs=row,
        scratch_shapes=[row_buf, row_buf, pltpu.SemaphoreType.DMA((2,))],
        compiler_params=_params(("arbitrary",)),
        name="combine_shared_ln2",
    )(pos, pos, u2, x1, w_tok, y_rows, s_w_in, s_w_out, gate2, ln_g, ln_b)


def _block_layout(counts, n_tokens):
    padded = (counts + MOE_BLOCK - 1) // MOE_BLOCK * MOE_BLOCK
    pad_ends = jnp.cumsum(padded)
    pad_starts = (pad_ends - padded).astype(jnp.int32)
    n_rows = -(-(n_tokens * TOP_K + N_EXPERTS * (MOE_BLOCK - 1)) // MOE_BLOCK) * MOE_BLOCK
    block_start = jnp.arange(n_rows // MOE_BLOCK, dtype=jnp.int32) * MOE_BLOCK
    block_expert = jnp.minimum(jnp.sum((block_start[:, None] >= pad_ends[None, :]).astype(jnp.int32), axis=1),
                               N_EXPERTS - 1).astype(jnp.int32)
    block_valid = (block_start < pad_ends[-1]).astype(jnp.int32)
    ar = jnp.arange(N_EXPERTS, dtype=jnp.int32)
    has = counts > 0
    later = (ar[None, :] > ar[:, None]) & has[None, :]
    next_has = jnp.min(jnp.where(later, ar[None, :], N_EXPERTS), axis=1)
    next_has = jnp.where(next_has >= N_EXPERTS, -1, next_has)
    after = (ar[None, :] == next_has[:, None])
    next2_has = jnp.sum(jnp.where(after, next_has[None, :], 0), axis=1)
    next2_has = jnp.where(next_has < 0, -1, next2_has)
    ordinal = jnp.cumsum(has.astype(jnp.int32)) - 1
    mine = block_expert[:, None] == ar[None, :]
    pick = lambda v: jnp.sum(jnp.where(mine, v[None, :], 0), axis=1).astype(jnp.int32)
    return (pad_starts, pad_ends.astype(jnp.int32), block_expert, block_valid, pick(next_has),
            pick(next2_has), pick(ordinal) % 2, n_rows)


def kernel(x, c, w_ada, b_ada, w_in, rel_bias, ssm_a_re, ssm_a_im, ssm_log_dt, ssm_b_re, ssm_b_im, ssm_c_re, ssm_c_im, ssm_d, w_attn_out, w_ssm_glu, w_o, ln1_g, ln1_b, w_router, router_bias, e_w_in, e_w_out, s_w_in, s_w_out, ln2_g, ln2_b):
    bsz, l, d = x.shape
    assert bsz == 1
    xf = x.reshape(l, d)
    i = 0
    mod = _modulation(c, w_ada[i], b_ada[i])
    shift1, scale1, gate1, shift2, scale2, gate2 = [mod[:, k * d:(k + 1) * d] for k in range(6)]

    w_in_b = w_in[i].astype(BF16)
    u = _modulate(xf, scale1, shift1)
    s_in = _matmul(u, w_in_b, QKV_WIDTH, SSM_WIDTH, 512, 512, F32, "in_proj_ssm")

    outs, lses = [], []
    for gi, (window, dilation) in enumerate(DILATED_GROUPS):
        bias_prev, bias_cur = _attn_bias_tables(rel_bias, gi, window, dilation)
        qkv = _inproj_qkv_group(u, w_in_b, gi, dilation)
        o, s = _attention_group(qkv, bias_prev, bias_cur, gi)
        outs.append(o)
        lses.append(s)
    attn = _attn_combine(outs, lses)

    tabs = _ssm_tables(ssm_a_re[i], ssm_a_im[i], ssm_log_dt[i], ssm_b_re[i], ssm_b_im[i],
                       ssm_c_re[i], ssm_c_im[i])
    y_ssm = _ssm(s_in, tabs, ssm_d[i])

    merged = _merge(attn, y_ssm, u, w_attn_out[i].astype(BF16), w_ssm_glu[i].astype(BF16), w_in_b)
    x1, u2, u2p, scores_t = _outproj(merged, xf, w_o[i].astype(BF16), w_router[i].T.astype(BF16), gate1,
                                     ln1_g[i].reshape(1, d), ln1_b[i].reshape(1, d), scale2, shift2)

    idx, w, rank, counts = _route(scores_t, router_bias[i])
    (pad_start, pad_end, block_expert, block_valid, block_next, block_next2, block_slot,
     n_rows) = _block_layout(counts[:, 0].astype(jnp.int32), l)
    pos = _positions(idx, rank, pad_start)
    x_rows = _dispatch(pos, pad_start, pad_end, u2p, n_rows)
    y_rows = _experts(block_expert, block_valid, block_next, block_next2, block_slot, x_rows,
                      e_w_in[i], e_w_out[i])
    out = _final(pos, u2, x1, w.T, y_rows, s_w_in[i].astype(BF16),
                 s_w_out[i].astype(BF16), gate2, ln2_g[i].reshape(1, d), ln2_b[i].reshape(1, d))
    return out.reshape(bsz, l, d)
```

```python
import functools
import math

import jax
import jax.numpy as jnp
from jax import lax
from jax.experimental import pallas as pl
from jax.experimental.pallas import tpu as pltpu

F32 = jnp.float32
BF16 = jnp.bfloat16

D_MODEL = 2048
HEAD_DIM = 128
HEADS_PER_GROUP = 4
DILATED_GROUPS = ((128, 1), (512, 4), (2048, 16))
N_ATTN_GROUPS = len(DILATED_GROUPS)
N_ATTN_HEADS = N_ATTN_GROUPS * HEADS_PER_GROUP
ATTN_WIDTH = N_ATTN_HEADS * HEAD_DIM
ATTN_OUT_WIDTH = HEADS_PER_GROUP * HEAD_DIM
ATTN_BLOCK = 128
REL_BUCKETS = 32
REL_MAX_DISTANCE = 2048
SSM_GROUP_CH = 16
SSM_STATE = 64
SSM_WIDTH = 1024
SSM_GROUPS = SSM_WIDTH // SSM_GROUP_CH
A_RE_MAX = -1e-4
QKV_WIDTH = 3 * ATTN_WIDTH
GATE_OFF = QKV_WIDTH + SSM_WIDTH
N_EXPERTS = 256
TOP_K = 8
N_EXPERT_GROUPS = 8
TOPK_GROUPS = 4
EXPERT_FF = 512
SHARED_FF = 512
ROUTED_SCALE = 2.5
MOE_BLOCK = 128
DEPTH = 1
DEEPNORM_ALPHA = (2 * DEPTH) ** 0.25
LN_EPS = 1e-5
NEG_BIG = -1e30
LANES = 128
SUBLANES = 8

SSM_CHUNK = 128
SSM_CHUNKS_PER_STEP = 2
SSM_BLK_GROUPS = 16
SSM_NBLK = SSM_GROUPS // SSM_BLK_GROUPS
SSM_BLK_CH = SSM_BLK_GROUPS * SSM_GROUP_CH
SSM_BLK_ST = SSM_BLK_GROUPS * SSM_STATE
SSM_STATES = SSM_GROUPS * SSM_STATE

VMEM_LIMIT = 56 * 1024 * 1024


def _params(sem, vmem=VMEM_LIMIT):
    return pltpu.CompilerParams(dimension_semantics=sem, vmem_limit_bytes=vmem)


def _mod_kernel(c_ref, w_ref, b_ref, o_ref):
    c = c_ref[...]
    cond = c * jax.nn.sigmoid(c)
    o_ref[...] = jnp.sum(cond * w_ref[...], axis=0, keepdims=True) + b_ref[...]


def _modulation(c, w_ada, b_ada):
    d, n = w_ada.shape
    assert c.shape == (1, d)
    tn = 1024
    return pl.pallas_call(
        _mod_kernel,
        out_shape=jax.ShapeDtypeStruct((1, n), F32),
        grid=(n // tn,),
        in_specs=[pl.BlockSpec((d, 1), lambda j: (0, 0)),
                  pl.BlockSpec((d, tn), lambda j: (0, j)),
                  pl.BlockSpec((1, tn), lambda j: (0, j))],
        out_specs=pl.BlockSpec((1, tn), lambda j: (0, j)),
        compiler_params=_params(("arbitrary",)),
        name="adaln_mod",
    )(c.reshape(d, 1), w_ada, b_ada.reshape(1, n))


def _modulate_kernel(x_ref, sc_ref, sh_ref, o_ref):
    o_ref[...] = (x_ref[...] * (1.0 + sc_ref[...]) + sh_ref[...]).astype(o_ref.dtype)


def _modulate(x, scale, shift):
    m, d = x.shape
    tm = 512
    return pl.pallas_call(
        _modulate_kernel,
        out_shape=jax.ShapeDtypeStruct((m, d), BF16),
        grid=(m // tm,),
        in_specs=[pl.BlockSpec((tm, d), lambda i: (i, 0)),
                  pl.BlockSpec((1, d), lambda i: (0, 0)),
                  pl.BlockSpec((1, d), lambda i: (0, 0))],
        out_specs=pl.BlockSpec((tm, d), lambda i: (i, 0)),
        compiler_params=_params(("arbitrary",)),
        name="modulate",
    )(x, scale, shift)


def _mm_kernel(a_ref, w_ref, o_ref):
    o_ref[...] = jnp.dot(a_ref[...], w_ref[...], preferred_element_type=F32).astype(o_ref.dtype)


def _matmul(a, w, col_off, n, tm, tn, out_dtype, name):
    m, k = a.shape
    assert col_off % tn == 0 and n % tn == 0 and m % tm == 0
    off_blocks = col_off // tn
    return pl.pallas_call(
        _mm_kernel,
        out_shape=jax.ShapeDtypeStruct((m, n), out_dtype),
        grid=(n // tn, m // tm),
        in_specs=[pl.BlockSpec((tm, k), lambda j, i: (i, 0)),
                  pl.BlockSpec((k, tn), lambda j, i: (0, j + off_blocks))],
        out_specs=pl.BlockSpec((tm, tn), lambda j, i: (i, j)),
        compiler_params=_params(("arbitrary", "arbitrary")),
        name=name,
    )(a, w)


ATTN_QBLOCKS = 2


def _attn_kernel(bp_ref, bc_ref, q_ref, kp_ref, kc_ref, vp_ref, vc_ref, o_ref, lse_ref):
    scale = HEAD_DIM ** -0.5
    nt = (((1,), (1,)), ((), ()))
    first = pl.program_id(1) == 0
    work = []
    for h in range(HEADS_PER_GROUP):
        hs = slice(h * HEAD_DIM, (h + 1) * HEAD_DIM)
        for qb in range(ATTN_QBLOCKS):
            rows = slice(qb * ATTN_BLOCK, (qb + 1) * ATTN_BLOCK)
            before = slice((qb - 1) * ATTN_BLOCK, qb * ATTN_BLOCK)
            work.append((h, hs, qb, rows, before))

    scores = []
    for h, hs, qb, rows, before in work:
        k_prev = kp_ref[:, hs] if qb == 0 else kc_ref[before, hs]
        q = q_ref[rows, hs]
        s_p = lax.dot_general(q, k_prev, nt, preferred_element_type=F32) * scale + bp_ref[h]
        s_c = lax.dot_general(q, kc_ref[rows, hs], nt, preferred_element_type=F32) * scale + bc_ref[h]
        if qb == 0:
            s_p = jnp.where(first, NEG_BIG, s_p)
        scores.append((s_p, s_c))
    maxes = [jnp.maximum(jnp.max(s_p, axis=-1, keepdims=True), jnp.max(s_c, axis=-1, keepdims=True))
             for s_p, s_c in scores]
    probs = [(jnp.exp(s_p - m), jnp.exp(s_c - m)) for (s_p, s_c), m in zip(scores, maxes)]
    sums = [jnp.sum(p_p, axis=-1, keepdims=True) + jnp.sum(p_c, axis=-1, keepdims=True)
            for p_p, p_c in probs]
    outs = []
    for (h, hs, qb, rows, before), (p_p, p_c) in zip(work, probs):
        v_prev = vp_ref[:, hs] if qb == 0 else vc_ref[before, hs]
        outs.append(jnp.dot(p_p.astype(BF16), v_prev, preferred_element_type=F32)
                    + jnp.dot(p_c.astype(BF16), vc_ref[rows, hs], preferred_element_type=F32))
    for (h, hs, qb, rows, before), o, m, l in zip(work, outs, maxes, sums):
        o_ref[rows, hs] = o / l
        lse_ref[rows, hs] = jnp.broadcast_to(m + jnp.log(l), (ATTN_BLOCK, HEAD_DIM))


def _inproj_dilated_kernel(a_ref, wq_ref, wk_ref, wv_ref, o_ref, acc_ref):
    dilation, rows, _ = o_ref.shape
    a = a_ref[...]
    for j, w_ref in enumerate((wq_ref, wk_ref, wv_ref)):
        tn = w_ref.shape[1]
        res = jnp.dot(a, w_ref[...], preferred_element_type=F32)
        if dilation == 1:
            o_ref[0, :, j * tn:(j + 1) * tn] = res.astype(o_ref.dtype)
            continue
        for c in range(tn // LANES):
            acc_ref[c] = res[:, c * LANES:(c + 1) * LANES]
        for r in range(dilation):
            for c in range(tn // LANES):
                o_ref[r, :, j * tn + c * LANES:j * tn + (c + 1) * LANES] = (
                    acc_ref[c, pl.ds(r, rows, stride=dilation), :].astype(o_ref.dtype))


def _inproj_qkv_group(u, w_in, gi, dilation):
    l, k = u.shape
    tm, tn = 512, ATTN_OUT_WIDTH
    w_spec = lambda j: pl.BlockSpec((k, tn), lambda i: (0, j * N_ATTN_GROUPS + gi))
    return pl.pallas_call(
        _inproj_dilated_kernel,
        out_shape=jax.ShapeDtypeStruct((dilation, l // dilation, 3 * tn), BF16),
        grid=(l // tm,),
        in_specs=[pl.BlockSpec((tm, k), lambda i: (i, 0)), w_spec(0), w_spec(1), w_spec(2)],
        out_specs=pl.BlockSpec((dilation, tm // dilation, 3 * tn), lambda i: (0, i, 0)),
        scratch_shapes=[pltpu.VMEM((tn // LANES, tm, LANES), F32)],
        compiler_params=_params(("arbitrary",)),
        name=f"in_proj_qkv_g{gi}",
    )(u, w_in, w_in, w_in)


def _attention_group(qkv, bias_prev, bias_cur, gi):
    dilation, m, _ = qkv.shape
    tile = ATTN_QBLOCKS * ATTN_BLOCK

    def cur(cb):
        return pl.BlockSpec((None, tile, ATTN_OUT_WIDTH), lambda r, b: (r, b, cb))

    def prev(cb):
        return pl.BlockSpec((None, ATTN_BLOCK, ATTN_OUT_WIDTH),
                            lambda r, b: (r, jnp.maximum(b * ATTN_QBLOCKS - 1, 0), cb))

    bias_spec = pl.BlockSpec((HEADS_PER_GROUP, ATTN_BLOCK, ATTN_BLOCK), lambda r, b: (0, 0, 0))
    out_spec = pl.BlockSpec((None, tile, ATTN_OUT_WIDTH), lambda r, b: (r, b, 0))
    return pl.pallas_call(
        _attn_kernel,
        out_shape=(jax.ShapeDtypeStruct((dilation, m, ATTN_OUT_WIDTH), F32),
                   jax.ShapeDtypeStruct((dilation, m, ATTN_OUT_WIDTH), F32)),
        grid=(dilation, m // tile),
        in_specs=[bias_spec, bias_spec, cur(0), prev(1), cur(1), prev(2), cur(2)],
        out_specs=(out_spec, out_spec),
        compiler_params=_params(("arbitrary", "arbitrary")),
        name=f"dilated_attn_g{gi}",
    )(bias_prev, bias_cur, qkv, qkv, qkv, qkv, qkv)


def _t5_bucket(dist):
    max_exact = REL_BUCKETS // 2
    d_f = jnp.maximum(dist, 1).astype(F32)
    large = max_exact + (jnp.log(d_f / max_exact) / math.log(REL_MAX_DISTANCE / max_exact)
                         * (REL_BUCKETS - max_exact)).astype(jnp.int32)
    large = jnp.minimum(large, REL_BUCKETS - 1)
    return jnp.where(dist < max_exact, dist, large)


def _attn_bias_tables(rel_bias, gi, window, dilation):
    n_keys = window // dilation
    bucket = _t5_bucket(jnp.arange(n_keys + 1, dtype=jnp.int32) * dilation)
    hs = slice(gi * HEADS_PER_GROUP, (gi + 1) * HEADS_PER_GROUP)
    hi = lax.Precision.HIGHEST
    pick_bucket = (bucket[:, None] == jnp.arange(REL_BUCKETS)[None, :]).astype(F32)
    by_dist = jnp.dot(pick_bucket, rel_bias[:, hs].astype(F32), precision=hi)
    qi = jnp.arange(ATTN_BLOCK)[:, None]
    kj = jnp.arange(ATTN_BLOCK)[None, :]
    d_prev = qi + ATTN_BLOCK - kj
    d_cur = qi - kj
    tabs = []
    for dist in (d_prev, d_cur):
        ok = (dist >= 0) & (dist <= n_keys)
        pick_dist = (dist[:, :, None] == jnp.arange(n_keys + 1)[None, None, :]).astype(F32)
        vals = jnp.einsum('ijd,dh->hij', pick_dist, by_dist, precision=hi)
        tabs.append(jnp.where(ok[None], vals, NEG_BIG))
    return tabs


ATTN_COMBINE_TILE = 512


def _attn_combine_kernel(o1, o2, o3, l1, l2, l3, out_ref, *scratch):
    def token_order(ref, buf, c):
        dilation, rows, _ = ref.shape
        cs = slice(c * LANES, (c + 1) * LANES)
        if dilation == 1:
            return ref[0, :, cs]
        for r in range(dilation):
            buf[c, pl.ds(r, rows, stride=dilation), :] = ref[r, :, cs]
        return buf[c]

    for c in range(out_ref.shape[1] // LANES):
        v1, v2, v3, a1, a2, a3 = [token_order(ref, buf, c)
                                  for ref, buf in zip((o1, o2, o3, l1, l2, l3), scratch)]
        m = jnp.maximum(jnp.maximum(a1, a2), a3)
        e1, e2, e3 = jnp.exp(a1 - m), jnp.exp(a2 - m), jnp.exp(a3 - m)
        num = e1 * v1 + e2 * v2 + e3 * v3
        out_ref[:, c * LANES:(c + 1) * LANES] = (num / (e1 + e2 + e3)).astype(out_ref.dtype)


def _attn_combine(outs, lses):
    w = outs[0].shape[-1]
    l = outs[0].shape[0] * outs[0].shape[1]
    tm = ATTN_COMBINE_TILE

    def spec(a):
        dilation = a.shape[0]
        return pl.BlockSpec((dilation, tm // dilation, w), lambda i: (0, i, 0))

    return pl.pallas_call(
        _attn_combine_kernel,
        out_shape=jax.ShapeDtypeStruct((l, w), BF16),
        grid=(l // tm,),
        in_specs=[spec(a) for a in (*outs, *lses)],
        out_specs=pl.BlockSpec((tm, w), lambda i: (i, 0)),
        scratch_shapes=[pltpu.VMEM((w // LANES, tm, LANES), F32)] * 6,
        compiler_params=_params(("arbitrary",)),
        name="attn_combine",
    )(*outs, *lses)


def _ssm_kernel(u_ref, bre_ref, bim_ref, cre_ref, cim_ref, enr_ref, eni_ref, epr_ref, epi_ref,
                lamr_ref, lami_ref, d_ref, tri_ref, y_ref, car_ref, cai_ref):
    @pl.when(pl.program_id(0) == 0)
    def _():
        car_ref[...] = jnp.zeros_like(car_ref)
        cai_ref[...] = jnp.zeros_like(cai_ref)

    tc = tri_ref.shape[0]
    tri = tri_ref[...]
    for j in range(SSM_NBLK):
        ch = slice(j * SSM_BLK_CH, (j + 1) * SSM_BLK_CH)
        st = slice(j * SSM_BLK_ST, (j + 1) * SSM_BLK_ST)
        u = u_ref[:, ch]
        ub = u.astype(BF16)
        bur = jnp.dot(ub, bre_ref[j], preferred_element_type=F32)
        bui = jnp.dot(ub, bim_ref[j], preferred_element_type=F32)
        enr, eni = enr_ref[:, st], eni_ref[:, st]
        epr, epi = epr_ref[:, st], epi_ref[:, st]
        lr, li = lamr_ref[:, st], lami_ref[:, st]
        cr, ci = car_ref[:, st], cai_ref[:, st]
        sr_chunks, si_chunks = [], []
        for q in range(u_ref.shape[0] // tc):
            rows = slice(q * tc, (q + 1) * tc)
            xr = bur[rows] * enr - bui[rows] * eni
            xi = bur[rows] * eni + bui[rows] * enr
            pr = jnp.dot(tri, xr.astype(BF16), preferred_element_type=F32)
            pi = jnp.dot(tri, xi.astype(BF16), preferred_element_type=F32)
            tr = pr + (lr * cr - li * ci)
            ti = pi + (lr * ci + li * cr)
            sr = epr * tr - epi * ti
            si = epr * ti + epi * tr
            cr, ci = sr[tc - 1:tc], si[tc - 1:tc]
            sr_chunks.append(sr.astype(BF16))
            si_chunks.append(si.astype(BF16))
        car_ref[:, st] = cr
        cai_ref[:, st] = ci
        y = (jnp.dot(jnp.concatenate(sr_chunks, axis=0), cre_ref[j], preferred_element_type=F32)
             - jnp.dot(jnp.concatenate(si_chunks, axis=0), cim_ref[j], preferred_element_type=F32))
        y_ref[:, ch] = (y + d_ref[:, ch] * u).astype(y_ref.dtype)


def _ssm_tables(a_re, a_im, log_dt, b_re, b_im, c_re, c_im):
    g, p, hc = b_re.shape
    lam_re = jnp.minimum(a_re.astype(F32), A_RE_MAX)
    lam_im = a_im.astype(F32)
    dt = jnp.exp(log_dt.astype(F32))[:, None]
    mag = jnp.exp(lam_re * dt)
    lb_re, lb_im = mag * jnp.cos(lam_im * dt), mag * jnp.sin(lam_im * dt)
    imag = jnp.exp(-lam_re * dt)
    li_re, li_im = imag * jnp.cos(lam_im * dt), -imag * jnp.sin(lam_im * dt)
    den = lam_re * lam_re + lam_im * lam_im
    nr, ni = lb_re - 1.0, lb_im
    f_re = (nr * lam_re + ni * lam_im) / den
    f_im = (ni * lam_re - nr * lam_im) / den
    bb_re = f_re[:, :, None] * b_re - f_im[:, :, None] * b_im
    bb_im = f_re[:, :, None] * b_im + f_im[:, :, None] * b_re

    def powers(pr, pi):
        er, ei = jnp.ones((1, g * p), F32), jnp.zeros((1, g * p), F32)
        pr, pi = pr.reshape(1, g * p), pi.reshape(1, g * p)
        while er.shape[0] < SSM_CHUNK:
            er, ei = (jnp.concatenate([er, er * pr - ei * pi], axis=0),
                      jnp.concatenate([ei, er * pi + ei * pr], axis=0))
            pr, pi = pr * pr - pi * pi, 2.0 * pr * pi
        return er, ei

    epr, epi = powers(lb_re, lb_im)
    enr, eni = powers(li_re, li_im)
    eye = jnp.eye(SSM_BLK_GROUPS, dtype=F32)

    def b_blocks(b):
        b = b.reshape(SSM_NBLK, SSM_BLK_GROUPS, p, hc)
        return jnp.einsum('jgph,gk->jghkp', b, eye).reshape(SSM_NBLK, SSM_BLK_CH, SSM_BLK_ST).astype(BF16)

    def c_blocks(c):
        c = c.astype(F32).reshape(SSM_NBLK, SSM_BLK_GROUPS, hc, p)
        return jnp.einsum('jghp,gk->jkpgh', c, eye).reshape(SSM_NBLK, SSM_BLK_ST, SSM_BLK_CH).astype(BF16)

    return dict(bre=b_blocks(bb_re), bim=b_blocks(bb_im), cre=c_blocks(c_re), cim=c_blocks(c_im),
                enr=enr, eni=eni, epr=epr, epi=epi,
                lamr=lb_re.reshape(1, g * p), lami=lb_im.reshape(1, g * p))


def _ssm(s_in, tabs, d_skip):
    l, w = s_in.shape
    tc = SSM_CHUNK
    tri = jnp.tril(jnp.ones((tc, tc), F32)).astype(BF16)
    full = lambda shape: pl.BlockSpec(shape, lambda c: (0,) * len(shape))
    rows = tc * SSM_CHUNKS_PER_STEP
    return pl.pallas_call(
        _ssm_kernel,
        out_shape=jax.ShapeDtypeStruct((l, w), BF16),
        grid=(l // rows,),
        in_specs=[pl.BlockSpec((rows, w), lambda c: (c, 0)),
                  full((SSM_NBLK, SSM_BLK_CH, SSM_BLK_ST)), full((SSM_NBLK, SSM_BLK_CH, SSM_BLK_ST)),
                  full((SSM_NBLK, SSM_BLK_ST, SSM_BLK_CH)), full((SSM_NBLK, SSM_BLK_ST, SSM_BLK_CH)),
                  full((tc, SSM_STATES)), full((tc, SSM_STATES)),
                  full((tc, SSM_STATES)), full((tc, SSM_STATES)),
                  full((1, SSM_STATES)), full((1, SSM_STATES)),
                  full((1, w)), full((tc, tc))],
        out_specs=pl.BlockSpec((rows, w), lambda c: (c, 0)),
        scratch_shapes=[pltpu.VMEM((1, SSM_STATES), F32), pltpu.VMEM((1, SSM_STATES), F32)],
        compiler_params=_params(("arbitrary",)),
        name="s5_ssm",
    )(s_in, tabs['bre'], tabs['bim'], tabs['cre'], tabs['cim'], tabs['enr'], tabs['eni'],
      tabs['epr'], tabs['epi'], tabs['lamr'], tabs['lami'], d_skip.reshape(1, w).astype(F32), tri)


def _merge_kernel(attn_ref, y_ref, u_ref, wa_ref, wg1_ref, wg2_ref, wia_ref, wis_ref, o_ref):
    a = jnp.dot(attn_ref[...], wa_ref[...], preferred_element_type=F32)
    y = y_ref[...]
    s = (jnp.dot(y, wg1_ref[...], preferred_element_type=F32)
         * jax.nn.sigmoid(jnp.dot(y, wg2_ref[...], preferred_element_type=F32)))
    u = u_ref[...]
    ga = jax.nn.sigmoid(jnp.dot(u, wia_ref[...], preferred_element_type=F32))
    gs = jax.nn.sigmoid(jnp.dot(u, wis_ref[...], preferred_element_type=F32))
    o_ref[...] = (ga * a + gs * s).astype(o_ref.dtype)


def _merge(attn, y_ssm, u, w_attn_out, w_ssm_glu, w_in):
    l, d = u.shape
    tm, tn = 512, 512
    nd = d // tn
    go = GATE_OFF // tn
    return pl.pallas_call(
        _merge_kernel,
        out_shape=jax.ShapeDtypeStruct((l, d), BF16),
        grid=(nd, l // tm),
        in_specs=[pl.BlockSpec((tm, ATTN_OUT_WIDTH), lambda j, i: (i, 0)),
                  pl.BlockSpec((tm, SSM_WIDTH), lambda j, i: (i, 0)),
                  pl.BlockSpec((tm, d), lambda j, i: (i, 0)),
                  pl.BlockSpec((ATTN_OUT_WIDTH, tn), lambda j, i: (0, j)),
                  pl.BlockSpec((SSM_WIDTH, tn), lambda j, i: (0, j)),
                  pl.BlockSpec((SSM_WIDTH, tn), lambda j, i: (0, j + nd)),
                  pl.BlockSpec((d, tn), lambda j, i: (0, j + go)),
                  pl.BlockSpec((d, tn), lambda j, i: (0, j + go + nd))],
        out_specs=pl.BlockSpec((tm, tn), lambda j, i: (i, j)),
        compiler_params=_params(("arbitrary", "arbitrary")),
        name="branch_merge",
    )(attn, y_ssm, u, w_attn_out, w_ssm_glu, w_ssm_glu, w_in, w_in)


def _layer_norm(h, g, b):
    mu = jnp.mean(h, axis=-1, keepdims=True)
    c = h - mu
    var = jnp.mean(c * c, axis=-1, keepdims=True)
    return c * lax.rsqrt(var + LN_EPS) * g + b


def _pack_rows(v):
    bits = lax.bitcast_convert_type(v.astype(BF16).astype(F32), jnp.uint32)
    half = bits.shape[1] // 2
    return (bits[:, :half] >> 16) | (bits[:, half:] & jnp.uint32(0xFFFF0000))


def _unpack_lo(xp):
    return lax.bitcast_convert_type(xp << 16, F32)


def _unpack_hi(xp):
    return lax.bitcast_convert_type(xp & jnp.uint32(0xFFFF0000), F32)


def _store_tile_rows(ref, packed):
    rows = packed.shape[0]
    for s in range(SUBLANES):
        ref[pl.ds(s, rows, stride=SUBLANES), :] = packed[:, s * LANES:(s + 1) * LANES]


def _load_tile_rows(ref, s):
    return ref[pl.ds(s, ref.shape[0] // SUBLANES, stride=SUBLANES), :]


def _outproj_kernel(mg_ref, x_ref, wo_ref, wr_ref, g1_ref, lg_ref, lb_ref, sc_ref, sh_ref,
                    x1_ref, u2_ref, u2p_ref, sco_ref):
    mix = jnp.dot(mg_ref[...], wo_ref[...], preferred_element_type=F32)
    x1 = _layer_norm(DEEPNORM_ALPHA * x_ref[...] + g1_ref[...] * mix, lg_ref[...], lb_ref[...])
    x1_ref[...] = x1
    u2 = (x1 * (1.0 + sc_ref[...]) + sh_ref[...]).astype(BF16)
    u2_ref[...] = u2
    _store_tile_rows(u2p_ref, _pack_rows(u2))
    logits = lax.dot_general(wr_ref[...], u2, (((1,), (1,)), ((), ())), preferred_element_type=F32)
    sco_ref[...] = jax.nn.sigmoid(logits)


def _outproj(merged, x, w_o, w_router_t, gate1, ln_g, ln_b, scale2, shift2):
    l, d = x.shape
    assert d == 2 * SUBLANES * LANES
    e = w_router_t.shape[0]
    tm = 512
    row = lambda w: pl.BlockSpec((tm, w), lambda i: (i, 0))
    vec = pl.BlockSpec((1, d), lambda i: (0, 0))
    return pl.pallas_call(
        _outproj_kernel,
        out_shape=(jax.ShapeDtypeStruct((l, d), F32), jax.ShapeDtypeStruct((l, d), BF16),
                   jax.ShapeDtypeStruct((l * SUBLANES, LANES), jnp.uint32),
                   jax.ShapeDtypeStruct((e, l), F32)),
        grid=(l // tm,),
        in_specs=[row(d), row(d), pl.BlockSpec((d, d), lambda i: (0, 0)),
                  pl.BlockSpec((e, d), lambda i: (0, 0)), vec, vec, vec, vec, vec],
        out_specs=(row(d), row(d), pl.BlockSpec((tm * SUBLANES, LANES), lambda i: (i, 0)),
                   pl.BlockSpec((e, tm), lambda i: (0, i))),
        compiler_params=_params(("arbitrary",)),
        name="outproj_ln1_router",
    )(merged, x, w_o, w_router_t, gate1, ln_g, ln_b, scale2, shift2)


ROUTE_TILE = 512
EXPERTS_PER_GROUP = N_EXPERTS // N_EXPERT_GROUPS


def _route_kernel(s_ref, b_ref, tri_ref, idx_ref, w_ref, rank_ref, cnt_ref, carry_ref):
    @pl.when(pl.program_id(0) == 0)
    def _():
        carry_ref[...] = jnp.zeros_like(carry_ref)

    ne, tm = s_ref.shape
    neg_inf = -jnp.inf
    s = s_ref[...]
    sel = s + b_ref[...]
    e_iota = lax.broadcasted_iota(jnp.int32, (ne, tm), 0)

    gs_rows = []
    for g in range(N_EXPERT_GROUPS):
        xg = sel[g * EXPERTS_PER_GROUP:(g + 1) * EXPERTS_PER_GROUP]
        m1 = jnp.max(xg, axis=0, keepdims=True)
        n1 = jnp.sum((xg == m1).astype(F32), axis=0, keepdims=True)
        m2 = jnp.max(jnp.where(xg < m1, xg, neg_inf), axis=0, keepdims=True)
        gs_rows.append(m1 + jnp.where(n1 >= 2.0, m1, m2))
    gs = jnp.concatenate(gs_rows, axis=0)

    g_iota = lax.broadcasted_iota(jnp.int32, gs.shape, 0)
    beaten = jnp.zeros(gs.shape, jnp.int32)
    for g2 in range(N_EXPERT_GROUPS):
        row = gs[g2:g2 + 1]
        beats = (row > gs) | ((row == gs) & (g2 < g_iota))
        beaten = beaten + beats.astype(jnp.int32)
    g_ok = beaten < TOPK_GROUPS
    work = jnp.concatenate(
        [jnp.where(g_ok[g:g + 1], sel[g * EXPERTS_PER_GROUP:(g + 1) * EXPERTS_PER_GROUP], neg_inf)
         for g in range(N_EXPERT_GROUPS)], axis=0)

    idxs, vals = [], []
    chosen = jnp.zeros((ne, tm), F32)
    for _ in range(TOP_K):
        m = jnp.max(work, axis=0, keepdims=True)
        i = jnp.min(jnp.where(work == m, e_iota, ne), axis=0, keepdims=True)
        onehot = e_iota == i
        idxs.append(i)
        vals.append(jnp.sum(jnp.where(onehot, s, 0.0), axis=0, keepdims=True))
        chosen = jnp.where(onehot, 1.0, chosen)
        work = jnp.where(onehot, neg_inf, work)
    wsum = vals[0]
    for v in vals[1:]:
        wsum = wsum + v

    before = jnp.dot(chosen.astype(BF16), tri_ref[...], preferred_element_type=F32) + carry_ref[...]
    ranks = [jnp.sum(jnp.where(e_iota == i, before, 0.0), axis=0, keepdims=True) for i in idxs]
    carry_ref[...] = carry_ref[...] + jnp.sum(chosen, axis=1, keepdims=True)

    idx_ref[...] = jnp.concatenate(idxs, axis=0)
    w_ref[...] = jnp.concatenate([v / wsum * ROUTED_SCALE for v in vals], axis=0)
    rank_ref[...] = jnp.concatenate(ranks, axis=0).astype(jnp.int32)
    cnt_ref[...] = carry_ref[...]


def _route(scores_t, router_bias):
    ne, t = scores_t.shape
    tm = ROUTE_TILE
    tri = jnp.triu(jnp.ones((tm, tm), F32), k=1).astype(BF16)
    tok = pl.BlockSpec((TOP_K, tm), lambda i: (0, i))
    return pl.pallas_call(
        _route_kernel,
        out_shape=(jax.ShapeDtypeStruct((TOP_K, t), jnp.int32), jax.ShapeDtypeStruct((TOP_K, t), F32),
                   jax.ShapeDtypeStruct((TOP_K, t), jnp.int32), jax.ShapeDtypeStruct((ne, 1), F32)),
        grid=(t // tm,),
        in_specs=[pl.BlockSpec((ne, tm), lambda i: (0, i)),
                  pl.BlockSpec((ne, 1), lambda i: (0, 0)),
                  pl.BlockSpec((tm, tm), lambda i: (0, 0))],
        out_specs=(tok, tok, tok, pl.BlockSpec((ne, 1), lambda i: (0, 0))),
        scratch_shapes=[pltpu.VMEM((ne, 1), F32)],
        compiler_params=_params(("arbitrary",)),
        name="moe_route",
    )(scores_t, router_bias.astype(F32).reshape(ne, 1), tri)


def _positions_kernel(ps_ref, idx_ref, rank_ref, pos_ref):
    idx = idx_ref[...]

    def add_expert(e, acc):
        return acc + jnp.where(idx == e, ps_ref[e], 0)

    start = lax.fori_loop(0, ps_ref.shape[0], add_expert, jnp.zeros(idx.shape, jnp.int32))
    pos_ref[...] = (start + rank_ref[...]) * SUBLANES


def _positions(idx, rank, pad_start):
    k, t = idx.shape
    tm = 1024
    tok = pl.BlockSpec((k, tm), lambda i: (0, i))
    return pl.pallas_call(
        _positions_kernel,
        out_shape=jax.ShapeDtypeStruct((k, t), jnp.int32),
        grid=(t // tm,),
        in_specs=[pl.BlockSpec(memory_space=pltpu.SMEM), tok, tok],
        out_specs=tok,
        compiler_params=_params(("arbitrary",)),
        name="moe_positions",
    )(pad_start, idx, rank)


DISPATCH_TILE = 256


def _dispatch_kernel(pos_ref, ps_ref, pe_ref, u_ref, xs_ref, zero_ref, sem, zsem):
    tm = u_ref.shape[0] // SUBLANES
    ne = ps_ref.shape[0]
    block_words = MOE_BLOCK * SUBLANES

    def tile_row(r):
        return pl.ds(pl.multiple_of(r * SUBLANES, SUBLANES), SUBLANES)

    @pl.when(pl.program_id(0) == 0)
    def _():
        zero_ref[...] = jnp.zeros_like(zero_ref)

        def block_copy(start):
            start = pl.multiple_of(start * SUBLANES, block_words)
            return pltpu.make_async_copy(zero_ref, xs_ref.at[pl.ds(start, block_words)], zsem)

        def fill(e, carry):
            @pl.when(pe_ref[e] > ps_ref[e])
            def _():
                block_copy(pe_ref[e] - MOE_BLOCK).start()
            return carry

        def fill_done(e, carry):
            @pl.when(pe_ref[e] > ps_ref[e])
            def _():
                block_copy(pe_ref[e] - MOE_BLOCK).wait()
            return carry

        lax.fori_loop(0, ne, fill, 0)
        lax.fori_loop(0, ne, fill_done, 0)

        first_unused = pe_ref[ne - 1] // MOE_BLOCK
        n_blocks = xs_ref.shape[0] // block_words

        def fill_unused(b, carry):
            block_copy(b * MOE_BLOCK).start()
            return carry

        def fill_unused_done(b, carry):
            block_copy(b * MOE_BLOCK).wait()
            return carry

        lax.fori_loop(first_unused, n_blocks, fill_unused, 0)
        lax.fori_loop(first_unused, n_blocks, fill_unused_done, 0)

    def issue(pair, carry):
        for t in (2 * pair, 2 * pair + 1):
            for k in range(TOP_K):
                dst = pl.ds(pl.multiple_of(pos_ref[k, t], SUBLANES), SUBLANES)
                pltpu.make_async_copy(u_ref.at[tile_row(t)], xs_ref.at[dst], sem).start(priority=k % 2)
        return carry

    lax.fori_loop(0, tm // 2, issue, 0)
    for _ in range(TOP_K):
        pltpu.make_async_copy(u_ref, xs_ref.at[pl.ds(0, tm * SUBLANES)], sem).wait()


def _dispatch(pos, pad_start, pad_end, u2p, n_rows):
    t = u2p.shape[0] // SUBLANES
    tm = DISPATCH_TILE
    smem_tok = pl.BlockSpec((TOP_K, tm), lambda i: (0, i), memory_space=pltpu.SMEM)
    smem_all = pl.BlockSpec(memory_space=pltpu.SMEM)
    return pl.pallas_call(
        _dispatch_kernel,
        out_shape=jax.ShapeDtypeStruct((n_rows * SUBLANES, LANES), jnp.uint32),
        grid=(t // tm,),
        in_specs=[smem_tok, smem_all, smem_all,
                  pl.BlockSpec((tm * SUBLANES, LANES), lambda i: (i, 0))],
        out_specs=pl.BlockSpec(memory_space=pl.ANY),
        scratch_shapes=[pltpu.VMEM((MOE_BLOCK * SUBLANES, LANES), jnp.uint32),
                        pltpu.SemaphoreType.DMA, pltpu.SemaphoreType.DMA],
        compiler_params=_params(("arbitrary",)),
        name="moe_dispatch",
    )(pos, pad_start, pad_end, u2p)


WEIGHT_CHUNK_BYTES = 2 * 1024 * 1024


def _expert_kernel(be_ref, bv_ref, nx_ref, nx2_ref, sl_ref, x_ref, win_ref, wout_ref, y_ref,
                   wfi_ref, wfo_ref, wbi_ref, wbo_ref, sem):
    b = pl.program_id(0)
    valid = bv_ref[b] > 0
    e = be_ref[b]
    slot = sl_ref[b]
    new_expert = (b == 0) | (e != be_ref[jnp.maximum(b - 1, 0)])

    def weight_chunks():
        chunks = []
        for hbm, land, wb, j in ((win_ref, wfi_ref, wbi_ref, 0), (wout_ref, wfo_ref, wbo_ref, 1)):
            rows = WEIGHT_CHUNK_BYTES // (hbm.shape[2] * 4)
            for c in range(hbm.shape[1] // rows):
                chunks.append((hbm, land, wb, j, pl.ds(c * rows, rows)))
        return chunks

    def chunk_copy(chunk, ex, s):
        hbm, land, _, j, rs = chunk
        return pltpu.make_async_copy(hbm.at[ex, rs], land.at[s, rs], sem.at[s, j])

    def start_weights(ex, s):
        for n, chunk in enumerate(weight_chunks()):
            chunk_copy(chunk, ex, s).start(priority=n % 2)

    @pl.when(b == 0)
    def _():
        start_weights(e, slot)

        @pl.when(nx_ref[b] >= 0)
        def _():
            start_weights(nx_ref[b], 1 - slot)

    @pl.when(valid & new_expert)
    def _():
        for chunk in weight_chunks():
            chunk_copy(chunk, e, slot).wait()
        for n, chunk in enumerate(weight_chunks()):
            _, land, wb, _, rs = chunk
            for s in range(2):
                @pl.when(slot == s)
                def _():
                    wb[rs] = land[s, rs].astype(BF16)

            @pl.when(nx2_ref[b] >= 0)
            def _():
                chunk_copy(chunk, nx2_ref[b], slot).start(priority=n % 2)

    @pl.when(valid)
    def _():
        words = [_load_tile_rows(x_ref, s) for s in range(SUBLANES)]
        lo = jnp.concatenate([_unpack_lo(wd).astype(BF16) for wd in words], axis=1)
        hi = jnp.concatenate([_unpack_hi(wd).astype(BF16) for wd in words], axis=1)
        half = lo.shape[1]
        h = (jnp.dot(lo, wbi_ref[:half], preferred_element_type=F32)
             + jnp.dot(hi, wbi_ref[half:], preferred_element_type=F32))
        hg, hu = h[:, :EXPERT_FF], h[:, EXPERT_FF:]
        act = (hg * jax.nn.sigmoid(hg) * hu).astype(BF16)
        _store_tile_rows(y_ref, _pack_rows(jnp.dot(act, wbo_ref[...], preferred_element_type=F32)))

    @pl.when(jnp.logical_not(valid))
    def _():
        y_ref[...] = jnp.zeros_like(y_ref)


def _experts(block_expert, block_valid, block_next, block_next2, block_slot, x_rows, e_w_in, e_w_out):
    n_rows = x_rows.shape[0] // SUBLANES
    nb = n_rows // MOE_BLOCK
    _, d, ff2 = e_w_in.shape
    ff = ff2 // 2
    rows_spec = pl.BlockSpec((MOE_BLOCK * SUBLANES, LANES), lambda b, *_: (b, 0))
    grid_spec = pltpu.PrefetchScalarGridSpec(
        num_scalar_prefetch=5,
        grid=(nb,),
        in_specs=[rows_spec,
                  pl.BlockSpec(memory_space=pl.ANY),
                  pl.BlockSpec(memory_space=pl.ANY)],
        out_specs=rows_spec,
        scratch_shapes=[pltpu.VMEM((2, d, ff2), F32), pltpu.VMEM((2, ff, d), F32),
                        pltpu.VMEM((d, ff2), BF16), pltpu.VMEM((ff, d), BF16),
                        pltpu.SemaphoreType.DMA((2, 2))],
    )
    return pl.pallas_call(
        _expert_kernel,
        out_shape=jax.ShapeDtypeStruct((n_rows * SUBLANES, LANES), jnp.uint32),
        grid_spec=grid_spec,
        compiler_params=_params(("arbitrary",)),
        name="routed_experts",
    )(block_expert, block_valid, block_next, block_next2, block_slot, x_rows, e_w_in, e_w_out)


COMBINE_TILE = 256


def _final_kernel(pos_ref, posn_ref, u2_ref, x1_ref, w_ref, ys_ref,
                  win_ref, wout_ref, g2_ref, lg_ref, lb_ref, o_ref, buf0_ref, buf1_ref, sem):
    tm = u2_ref.shape[0]
    i = pl.program_id(0)
    bufs = (buf0_ref, buf1_ref)

    def tile_row(r):
        return pl.ds(pl.multiple_of(r * SUBLANES, SUBLANES), SUBLANES)

    def gather(slots_ref, par):
        def issue(pair, carry):
            for t in (2 * pair, 2 * pair + 1):
                for k in range(TOP_K):
                    src = pl.ds(pl.multiple_of(slots_ref[k, t], SUBLANES), SUBLANES)
                    pltpu.make_async_copy(ys_ref.at[src], bufs[par].at[k, tile_row(t)],
                                          sem.at[par]).start(priority=k % 2)
            return carry

        lax.fori_loop(0, tm // 2, issue, 0)

    @pl.when(i == 0)
    def _():
        gather(pos_ref, 0)

    for par in range(2):
        @pl.when((i + 1 < pl.num_programs(0)) & (i % 2 == par))
        def _():
            gather(posn_ref, 1 - par)

    h = jnp.dot(u2_ref[...], win_ref[...], preferred_element_type=F32)
    hg, hu = h[:, :SHARED_FF], h[:, SHARED_FF:]
    act = (hg * jax.nn.sigmoid(hg) * hu).astype(BF16)
    shared = jnp.dot(act, wout_ref[...], preferred_element_type=F32)

    w = w_ref[...]
    wk = [w[:, k:k + 1] for k in range(TOP_K)]

    for par in range(2):
        @pl.when(i % 2 == par)
        def _():
            buf = bufs[par]
            for k in range(TOP_K):
                pltpu.make_async_copy(ys_ref.at[pl.ds(0, tm * SUBLANES)], buf.at[k], sem.at[par]).wait()
            lo_parts, hi_parts = [], []
            for s in range(SUBLANES):
                lo = jnp.zeros((tm, LANES), F32)
                hi = jnp.zeros((tm, LANES), F32)
                for k in range(TOP_K):
                    words = _load_tile_rows(buf.at[k], s)
                    lo = lo + wk[k] * _unpack_lo(words)
                    hi = hi + wk[k] * _unpack_hi(words)
                lo_parts.append(lo)
                hi_parts.append(hi)
            ffn = shared + jnp.concatenate(lo_parts + hi_parts, axis=1)
            o_ref[...] = _layer_norm(DEEPNORM_ALPHA * x1_ref[...] + g2_ref[...] * ffn,
                                     lg_ref[...], lb_ref[...])


def _final(pos, u2, x1, w_tok, y_rows, s_w_in, s_w_out, gate2, ln_g, ln_b):
    l, d = x1.shape
    tm = COMBINE_TILE
    row = pl.BlockSpec((tm, d), lambda i: (i, 0))
    vec = pl.BlockSpec((1, d), lambda i: (0, 0))
    n_tiles = l // tm
    smem_tok = pl.BlockSpec((TOP_K, tm), lambda i: (0, i), memory_space=pltpu.SMEM)
    smem_next = pl.BlockSpec((TOP_K, tm), lambda i: (0, jnp.minimum(i + 1, n_tiles - 1)),
                             memory_space=pltpu.SMEM)
    row_buf = pltpu.VMEM((TOP_K, tm * SUBLANES, LANES), jnp.uint32)
    return pl.pallas_call(
        _final_kernel,
        out_shape=jax.ShapeDtypeStruct((l, d), F32),
        grid=(n_tiles,),
        in_specs=[smem_tok, smem_next, row, row, pl.BlockSpec((tm, TOP_K), lambda i: (i, 0)),
                  pl.BlockSpec(memory_space=pl.ANY),
                  pl.BlockSpec(s_w_in.shape, lambda i: (0, 0)),
                  pl.BlockSpec(s_w_out.shape, lambda i: (0, 0)), vec, vec, vec],
        out_specs=row,
        scratch_shapes=[row_buf, row_buf, pltpu.SemaphoreType.DMA((2,))],
        compiler_params=_params(("arbitrary",)),
        name="combine_shared_ln2",
    )(pos, pos, u2, x1, w_tok, y_rows, s_w_in, s_w_out, gate2, ln_g, ln_b)


def _block_layout(counts, n_tokens):
    padded = (counts + MOE_BLOCK - 1) // MOE_BLOCK * MOE_BLOCK
    pad_ends = jnp.cumsum(padded)
    pad_starts = (pad_ends - padded).astype(jnp.int32)
    n_rows = -(-(n_tokens * TOP_K + N_EXPERTS * (MOE_BLOCK - 1)) // MOE_BLOCK) * MOE_BLOCK
    block_start = jnp.arange(n_rows // MOE_BLOCK, dtype=jnp.int32) * MOE_BLOCK
    block_expert = jnp.minimum(jnp.sum((block_start[:, None] >= pad_ends[None, :]).astype(jnp.int32), axis=1),
                               N_EXPERTS - 1).astype(jnp.int32)
    block_valid = (block_start < pad_ends[-1]).astype(jnp.int32)
    ar = jnp.arange(N_EXPERTS, dtype=jnp.int32)
    has = counts > 0
    later = (ar[None, :] > ar[:, None]) & has[None, :]
    next_has = jnp.min(jnp.where(later, ar[None, :], N_EXPERTS), axis=1)
    next_has = jnp.where(next_has >= N_EXPERTS, -1, next_has)
    after = (ar[None, :] == next_has[:, None])
    next2_has = jnp.sum(jnp.where(after, next_has[None, :], 0), axis=1)
    next2_has = jnp.where(next_has < 0, -1, next2_has)
    ordinal = jnp.cumsum(has.astype(jnp.int32)) - 1
    mine = block_expert[:, None] == ar[None, :]
    pick = lambda v: jnp.sum(jnp.where(mine, v[None, :], 0), axis=1).astype(jnp.int32)
    return (pad_starts, pad_ends.astype(jnp.int32), block_expert, block_valid, pick(next_has),
            pick(next2_has), pick(ordinal) % 2, n_rows)


def kernel(x, c, w_ada, b_ada, w_in, rel_bias, ssm_a_re, ssm_a_im, ssm_log_dt, ssm_b_re, ssm_b_im, ssm_c_re, ssm_c_im, ssm_d, w_attn_out, w_ssm_glu, w_o, ln1_g, ln1_b, w_router, router_bias, e_w_in, e_w_out, s_w_in, s_w_out, ln2_g, ln2_b):
    bsz, l, d = x.shape
    assert bsz == 1
    xf = x.reshape(l, d)
    i = 0
    mod = _modulation(c, w_ada[i], b_ada[i])
    shift1, scale1, gate1, shift2, scale2, gate2 = [mod[:, k * d:(k + 1) * d] for k in range(6)]

    w_in_b = w_in[i].astype(BF16)
    u = _modulate(xf, scale1, shift1)
    s_in = _matmul(u, w_in_b, QKV_WIDTH, SSM_WIDTH, 512, 512, F32, "in_proj_ssm")

    outs, lses = [], []
    for gi, (window, dilation) in enumerate(DILATED_GROUPS):
        bias_prev, bias_cur = _attn_bias_tables(rel_bias, gi, window, dilation)
        qkv = _inproj_qkv_group(u, w_in_b, gi, dilation)
        o, s = _attention_group(qkv, bias_prev, bias_cur, gi)
        outs.append(o)
        lses.append(s)
    attn = _attn_combine(outs, lses)

    tabs = _ssm_tables(ssm_a_re[i], ssm_a_im[i], ssm_log_dt[i], ssm_b_re[i], ssm_b_im[i],
                       ssm_c_re[i], ssm_c_im[i])
    y_ssm = _ssm(s_in, tabs, ssm_d[i])

    merged = _merge(attn, y_ssm, u, w_attn_out[i].astype(BF16), w_ssm_glu[i].astype(BF16), w_in_b)
    x1, u2, u2p, scores_t = _outproj(merged, xf, w_o[i].astype(BF16), w_router[i].T.astype(BF16), gate1,
                                     ln1_g[i].reshape(1, d), ln1_b[i].reshape(1, d), scale2, shift2)

    idx, w, rank, counts = _route(scores_t, router_bias[i])
    (pad_start, pad_end, block_expert, block_valid, block_next, block_next2, block_slot,
     n_rows) = _block_layout(counts[:, 0].astype(jnp.int32), l)
    pos = _positions(idx, rank, pad_start)
    x_rows = _dispatch(pos, pad_start, pad_end, u2p, n_rows)
    y_rows = _experts(block_expert, block_valid, block_next, block_next2, block_slot, x_rows,
                      e_w_in[i], e_w_out[i])
    out = _final(pos, u2, x1, w.T, y_rows, s_w_in[i].astype(BF16),
                 s_w_out[i].astype(BF16), gate2, ln2_g[i].reshape(1, d), ln2_b[i].reshape(1, d))
    return out.reshape(bsz, l, d)
```

```python
import functools
import math

import jax
import jax.numpy as jnp
from jax import lax
from jax.experimental import pallas as pl
from jax.experimental.pallas import tpu as pltpu

F32 = jnp.float32
BF16 = jnp.bfloat16

D_MODEL = 2048
HEAD_DIM = 128
HEADS_PER_GROUP = 4
DILATED_GROUPS = ((128, 1), (512, 4), (2048, 16))
N_ATTN_GROUPS = len(DILATED_GROUPS)
N_ATTN_HEADS = N_ATTN_GROUPS * HEADS_PER_GROUP
ATTN_WIDTH = N_ATTN_HEADS * HEAD_DIM
ATTN_OUT_WIDTH = HEADS_PER_GROUP * HEAD_DIM
ATTN_BLOCK = 128
REL_BUCKETS = 32
REL_MAX_DISTANCE = 2048
SSM_GROUP_CH = 16
SSM_STATE = 64
SSM_WIDTH = 1024
SSM_GROUPS = SSM_WIDTH // SSM_GROUP_CH
A_RE_MAX = -1e-4
QKV_WIDTH = 3 * ATTN_WIDTH
GATE_OFF = QKV_WIDTH + SSM_WIDTH
N_EXPERTS = 256
TOP_K = 8
N_EXPERT_GROUPS = 8
TOPK_GROUPS = 4
EXPERT_FF = 512
SHARED_FF = 512
ROUTED_SCALE = 2.5
MOE_BLOCK = 128
DEPTH = 1
DEEPNORM_ALPHA = (2 * DEPTH) ** 0.25
LN_EPS = 1e-5
NEG_BIG = -1e30
LANES = 128
SUBLANES = 8

SSM_CHUNK = 128
SSM_CHUNKS_PER_STEP = 2
SSM_BLK_GROUPS = 16
SSM_NBLK = SSM_GROUPS // SSM_BLK_GROUPS
SSM_BLK_CH = SSM_BLK_GROUPS * SSM_GROUP_CH
SSM_BLK_ST = SSM_BLK_GROUPS * SSM_STATE
SSM_STATES = SSM_GROUPS * SSM_STATE

VMEM_LIMIT = 56 * 1024 * 1024


def _params(sem, vmem=VMEM_LIMIT):
    return pltpu.CompilerParams(dimension_semantics=sem, vmem_limit_bytes=vmem)


def _mod_kernel(c_ref, w_ref, b_ref, o_ref):
    c = c_ref[...]
    cond = c * jax.nn.sigmoid(c)
    o_ref[...] = jnp.sum(cond * w_ref[...], axis=0, keepdims=True) + b_ref[...]


def _modulation(c, w_ada, b_ada):
    d, n = w_ada.shape
    assert c.shape == (1, d)
    tn = 1024
    return pl.pallas_call(
        _mod_kernel,
        out_shape=jax.ShapeDtypeStruct((1, n), F32),
        grid=(n // tn,),
        in_specs=[pl.BlockSpec((d, 1), lambda j: (0, 0)),
                  pl.BlockSpec((d, tn), lambda j: (0, j)),
                  pl.BlockSpec((1, tn), lambda j: (0, j))],
        out_specs=pl.BlockSpec((1, tn), lambda j: (0, j)),
        compiler_params=_params(("arbitrary",)),
        name="adaln_mod",
    )(c.reshape(d, 1), w_ada, b_ada.reshape(1, n))


def _modulate_kernel(x_ref, sc_ref, sh_ref, o_ref):
    o_ref[...] = (x_ref[...] * (1.0 + sc_ref[...]) + sh_ref[...]).astype(o_ref.dtype)


def _modulate(x, scale, shift):
    m, d = x.shape
    tm = 512
    return pl.pallas_call(
        _modulate_kernel,
        out_shape=jax.ShapeDtypeStruct((m, d), BF16),
        grid=(m // tm,),
        in_specs=[pl.BlockSpec((tm, d), lambda i: (i, 0)),
                  pl.BlockSpec((1, d), lambda i: (0, 0)),
                  pl.BlockSpec((1, d), lambda i: (0, 0))],
        out_specs=pl.BlockSpec((tm, d), lambda i: (i, 0)),
        compiler_params=_params(("arbitrary",)),
        name="modulate",
    )(x, scale, shift)


def _mm_kernel(a_ref, w_ref, o_ref):
    o_ref[...] = jnp.dot(a_ref[...], w_ref[...], preferred_element_type=F32).astype(o_ref.dtype)


def _matmul(a, w, col_off, n, tm, tn, out_dtype, name):
    m, k = a.shape
    assert col_off % tn == 0 and n % tn == 0 and m % tm == 0
    off_blocks = col_off // tn
    return pl.pallas_call(
        _mm_kernel,
        out_shape=jax.ShapeDtypeStruct((m, n), out_dtype),
        grid=(n // tn, m // tm),
        in_specs=[pl.BlockSpec((tm, k), lambda j, i: (i, 0)),
                  pl.BlockSpec((k, tn), lambda j, i: (0, j + off_blocks))],
        out_specs=pl.BlockSpec((tm, tn), lambda j, i: (i, j)),
        compiler_params=_params(("arbitrary", "arbitrary")),
        name=name,
    )(a, w)


ATTN_QBLOCKS = 4


def _attn_kernel(bp_ref, bc_ref, q_ref, kp_ref, kc_ref, vp_ref, vc_ref, o_ref, lse_ref):
    scale = HEAD_DIM ** -0.5
    nt = (((1,), (1,)), ((), ()))
    first = pl.program_id(1) == 0
    work = []
    for h in range(HEADS_PER_GROUP):
        hs = slice(h * HEAD_DIM, (h + 1) * HEAD_DIM)
        for qb in range(ATTN_QBLOCKS):
            rows = slice(qb * ATTN_BLOCK, (qb + 1) * ATTN_BLOCK)
            before = slice((qb - 1) * ATTN_BLOCK, qb * ATTN_BLOCK)
            work.append((h, hs, qb, rows, before))

    scores = []
    for h, hs, qb, rows, before in work:
        k_prev = kp_ref[:, hs] if qb == 0 else kc_ref[before, hs]
        q = q_ref[rows, hs]
        s_p = lax.dot_general(q, k_prev, nt, preferred_element_type=F32) * scale + bp_ref[h]
        s_c = lax.dot_general(q, kc_ref[rows, hs], nt, preferred_element_type=F32) * scale + bc_ref[h]
        if qb == 0:
            s_p = jnp.where(first, NEG_BIG, s_p)
        scores.append((s_p, s_c))
    maxes = [jnp.maximum(jnp.max(s_p, axis=-1, keepdims=True), jnp.max(s_c, axis=-1, keepdims=True))
             for s_p, s_c in scores]
    probs = [(jnp.exp(s_p - m), jnp.exp(s_c - m)) for (s_p, s_c), m in zip(scores, maxes)]
    sums = [jnp.sum(p_p, axis=-1, keepdims=True) + jnp.sum(p_c, axis=-1, keepdims=True)
            for p_p, p_c in probs]
    outs = []
    for (h, hs, qb, rows, before), (p_p, p_c) in zip(work, probs):
        v_prev = vp_ref[:, hs] if qb == 0 else vc_ref[before, hs]
        outs.append(jnp.dot(p_p.astype(BF16), v_prev, preferred_element_type=F32)
                    + jnp.dot(p_c.astype(BF16), vc_ref[rows, hs], preferred_element_type=F32))
    for (h, hs, qb, rows, before), o, m, l in zip(work, outs, maxes, sums):
        o_ref[rows, hs] = o / l
        lse_ref[rows, hs] = jnp.broadcast_to(m + jnp.log(l), (ATTN_BLOCK, HEAD_DIM))


def _inproj_dilated_kernel(a_ref, wq_ref, wk_ref, wv_ref, o_ref, acc_ref):
    dilation, rows, _ = o_ref.shape
    a = a_ref[...]
    for j, w_ref in enumerate((wq_ref, wk_ref, wv_ref)):
        tn = w_ref.shape[1]
        res = jnp.dot(a, w_ref[...], preferred_element_type=F32)
        if dilation == 1:
            o_ref[0, :, j * tn:(j + 1) * tn] = res.astype(o_ref.dtype)
            continue
        for c in range(tn // LANES):
            acc_ref[c] = res[:, c * LANES:(c + 1) * LANES]
        for r in range(dilation):
            for c in range(tn // LANES):
                o_ref[r, :, j * tn + c * LANES:j * tn + (c + 1) * LANES] = (
                    acc_ref[c, pl.ds(r, rows, stride=dilation), :].astype(o_ref.dtype))


def _inproj_qkv_group(u, w_in, gi, dilation):
    l, k = u.shape
    tm, tn = 512, ATTN_OUT_WIDTH
    w_spec = lambda j: pl.BlockSpec((k, tn), lambda i: (0, j * N_ATTN_GROUPS + gi))
    return pl.pallas_call(
        _inproj_dilated_kernel,
        out_shape=jax.ShapeDtypeStruct((dilation, l // dilation, 3 * tn), BF16),
        grid=(l // tm,),
        in_specs=[pl.BlockSpec((tm, k), lambda i: (i, 0)), w_spec(0), w_spec(1), w_spec(2)],
        out_specs=pl.BlockSpec((dilation, tm // dilation, 3 * tn), lambda i: (0, i, 0)),
        scratch_shapes=[pltpu.VMEM((tn // LANES, tm, LANES), F32)],
        compiler_params=_params(("arbitrary",)),
        name=f"in_proj_qkv_g{gi}",
    )(u, w_in, w_in, w_in)


def _attention_group(qkv, bias_prev, bias_cur, gi):
    dilation, m, _ = qkv.shape
    tile = ATTN_QBLOCKS * ATTN_BLOCK

    def cur(cb):
        return pl.BlockSpec((None, tile, ATTN_OUT_WIDTH), lambda r, b: (r, b, cb))

    def prev(cb):
        return pl.BlockSpec((None, ATTN_BLOCK, ATTN_OUT_WIDTH),
                            lambda r, b: (r, jnp.maximum(b * ATTN_QBLOCKS - 1, 0), cb))

    bias_spec = pl.BlockSpec((HEADS_PER_GROUP, ATTN_BLOCK, ATTN_BLOCK), lambda r, b: (0, 0, 0))
    out_spec = pl.BlockSpec((None, tile, ATTN_OUT_WIDTH), lambda r, b: (r, b, 0))
    return pl.pallas_call(
        _attn_kernel,
        out_shape=(jax.ShapeDtypeStruct((dilation, m, ATTN_OUT_WIDTH), F32),
                   jax.ShapeDtypeStruct((dilation, m, ATTN_OUT_WIDTH), F32)),
        grid=(dilation, m // tile),
        in_specs=[bias_spec, bias_spec, cur(0), prev(1), cur(1), prev(2), cur(2)],
        out_specs=(out_spec, out_spec),
        compiler_params=_params(("arbitrary", "arbitrary")),
        name=f"dilated_attn_g{gi}",
    )(bias_prev, bias_cur, qkv, qkv, qkv, qkv, qkv)


def _t5_bucket(dist):
    max_exact = REL_BUCKETS // 2
    d_f = jnp.maximum(dist, 1).astype(F32)
    large = max_exact + (jnp.log(d_f / max_exact) / math.log(REL_MAX_DISTANCE / max_exact)
                         * (REL_BUCKETS - max_exact)).astype(jnp.int32)
    large = jnp.minimum(large, REL_BUCKETS - 1)
    return jnp.where(dist < max_exact, dist, large)


def _attn_bias_tables(rel_bias, gi, window, dilation):
    n_keys = window // dilation
    bucket = _t5_bucket(jnp.arange(n_keys + 1, dtype=jnp.int32) * dilation)
    hs = slice(gi * HEADS_PER_GROUP, (gi + 1) * HEADS_PER_GROUP)
    hi = lax.Precision.HIGHEST
    pick_bucket = (bucket[:, None] == jnp.arange(REL_BUCKETS)[None, :]).astype(F32)
    by_dist = jnp.dot(pick_bucket, rel_bias[:, hs].astype(F32), precision=hi)
    qi = jnp.arange(ATTN_BLOCK)[:, None]
    kj = jnp.arange(ATTN_BLOCK)[None, :]
    d_prev = qi + ATTN_BLOCK - kj
    d_cur = qi - kj
    tabs = []
    for dist in (d_prev, d_cur):
        ok = (dist >= 0) & (dist <= n_keys)
        pick_dist = (dist[:, :, None] == jnp.arange(n_keys + 1)[None, None, :]).astype(F32)
        vals = jnp.einsum('ijd,dh->hij', pick_dist, by_dist, precision=hi)
        tabs.append(jnp.where(ok[None], vals, NEG_BIG))
    return tabs


ATTN_COMBINE_TILE = 512


def _attn_combine_kernel(o1, o2, o3, l1, l2, l3, out_ref, *scratch):
    def token_order(ref, buf, c):
        dilation, rows, _ = ref.shape
        cs = slice(c * LANES, (c + 1) * LANES)
        if dilation == 1:
            return ref[0, :, cs]
        for r in range(dilation):
            buf[c, pl.ds(r, rows, stride=dilation), :] = ref[r, :, cs]
        return buf[c]

    for c in range(out_ref.shape[1] // LANES):
        v1, v2, v3, a1, a2, a3 = [token_order(ref, buf, c)
                                  for ref, buf in zip((o1, o2, o3, l1, l2, l3), scratch)]
        m = jnp.maximum(jnp.maximum(a1, a2), a3)
        e1, e2, e3 = jnp.exp(a1 - m), jnp.exp(a2 - m), jnp.exp(a3 - m)
        num = e1 * v1 + e2 * v2 + e3 * v3
        out_ref[:, c * LANES:(c + 1) * LANES] = (num / (e1 + e2 + e3)).astype(out_ref.dtype)


def _attn_combine(outs, lses):
    w = outs[0].shape[-1]
    l = outs[0].shape[0] * outs[0].shape[1]
    tm = ATTN_COMBINE_TILE

    def spec(a):
        dilation = a.shape[0]
        return pl.BlockSpec((dilation, tm // dilation, w), lambda i: (0, i, 0))

    return pl.pallas_call(
        _attn_combine_kernel,
        out_shape=jax.ShapeDtypeStruct((l, w), BF16),
        grid=(l // tm,),
        in_specs=[spec(a) for a in (*outs, *lses)],
        out_specs=pl.BlockSpec((tm, w), lambda i: (i, 0)),
        scratch_shapes=[pltpu.VMEM((w // LANES, tm, LANES), F32)] * 6,
        compiler_params=_params(("arbitrary",)),
        name="attn_combine",
    )(*outs, *lses)


def _ssm_kernel(u_ref, bre_ref, bim_ref, cre_ref, cim_ref, enr_ref, eni_ref, epr_ref, epi_ref,
                lamr_ref, lami_ref, d_ref, tri_ref, y_ref, car_ref, cai_ref):
    @pl.when(pl.program_id(0) == 0)
    def _():
        car_ref[...] = jnp.zeros_like(car_ref)
        cai_ref[...] = jnp.zeros_like(cai_ref)

    tc = tri_ref.shape[0]
    tri = tri_ref[...]
    for j in range(SSM_NBLK):
        ch = slice(j * SSM_BLK_CH, (j + 1) * SSM_BLK_CH)
        st = slice(j * SSM_BLK_ST, (j + 1) * SSM_BLK_ST)
        u = u_ref[:, ch]
        ub = u.astype(BF16)
        bur = jnp.dot(ub, bre_ref[j], preferred_element_type=F32)
        bui = jnp.dot(ub, bim_ref[j], preferred_element_type=F32)
        enr, eni = enr_ref[:, st], eni_ref[:, st]
        epr, epi = epr_ref[:, st], epi_ref[:, st]
        lr, li = lamr_ref[:, st], lami_ref[:, st]
        cr, ci = car_ref[:, st], cai_ref[:, st]
        sr_chunks, si_chunks = [], []
        for q in range(u_ref.shape[0] // tc):
            rows = slice(q * tc, (q + 1) * tc)
            xr = bur[rows] * enr - bui[rows] * eni
            xi = bur[rows] * eni + bui[rows] * enr
            pr = jnp.dot(tri, xr.astype(BF16), preferred_element_type=F32)
            pi = jnp.dot(tri, xi.astype(BF16), preferred_element_type=F32)
            tr = pr + (lr * cr - li * ci)
            ti = pi + (lr * ci + li * cr)
            sr = epr * tr - epi * ti
            si = epr * ti + epi * tr
            cr, ci = sr[tc - 1:tc], si[tc - 1:tc]
            sr_chunks.append(sr.astype(BF16))
            si_chunks.append(si.astype(BF16))
        car_ref[:, st] = cr
        cai_ref[:, st] = ci
        y = (jnp.dot(jnp.concatenate(sr_chunks, axis=0), cre_ref[j], preferred_element_type=F32)
             - jnp.dot(jnp.concatenate(si_chunks, axis=0), cim_ref[j], preferred_element_type=F32))
        y_ref[:, ch] = (y + d_ref[:, ch] * u).astype(y_ref.dtype)


def _ssm_tables(a_re, a_im, log_dt, b_re, b_im, c_re, c_im):
    g, p, hc = b_re.shape
    lam_re = jnp.minimum(a_re.astype(F32), A_RE_MAX)
    lam_im = a_im.astype(F32)
    dt = jnp.exp(log_dt.astype(F32))[:, None]
    mag = jnp.exp(lam_re * dt)
    lb_re, lb_im = mag * jnp.cos(lam_im * dt), mag * jnp.sin(lam_im * dt)
    imag = jnp.exp(-lam_re * dt)
    li_re, li_im = imag * jnp.cos(lam_im * dt), -imag * jnp.sin(lam_im * dt)
    den = lam_re * lam_re + lam_im * lam_im
    nr, ni = lb_re - 1.0, lb_im
    f_re = (nr * lam_re + ni * lam_im) / den
    f_im = (ni * lam_re - nr * lam_im) / den
    bb_re = f_re[:, :, None] * b_re - f_im[:, :, None] * b_im
    bb_im = f_re[:, :, None] * b_im + f_im[:, :, None] * b_re

    def powers(pr, pi):
        er, ei = jnp.ones((1, g * p), F32), jnp.zeros((1, g * p), F32)
        pr, pi = pr.reshape(1, g * p), pi.reshape(1, g * p)
        while er.shape[0] < SSM_CHUNK:
            er, ei = (jnp.concatenate([er, er * pr - ei * pi], axis=0),
                      jnp.concatenate([ei, er * pi + ei * pr], axis=0))
            pr, pi = pr * pr - pi * pi, 2.0 * pr * pi
        return er, ei

    epr, epi = powers(lb_re, lb_im)
    enr, eni = powers(li_re, li_im)
    eye = jnp.eye(SSM_BLK_GROUPS, dtype=F32)

    def b_blocks(b):
        b = b.reshape(SSM_NBLK, SSM_BLK_GROUPS, p, hc)
        return jnp.einsum('jgph,gk->jghkp', b, eye).reshape(SSM_NBLK, SSM_BLK_CH, SSM_BLK_ST).astype(BF16)

    def c_blocks(c):
        c = c.astype(F32).reshape(SSM_NBLK, SSM_BLK_GROUPS, hc, p)
        return jnp.einsum('jghp,gk->jkpgh', c, eye).reshape(SSM_NBLK, SSM_BLK_ST, SSM_BLK_CH).astype(BF16)

    return dict(bre=b_blocks(bb_re), bim=b_blocks(bb_im), cre=c_blocks(c_re), cim=c_blocks(c_im),
                enr=enr, eni=eni, epr=epr, epi=epi,
                lamr=lb_re.reshape(1, g * p), lami=lb_im.reshape(1, g * p))


def _ssm(s_in, tabs, d_skip):
    l, w = s_in.shape
    tc = SSM_CHUNK
    tri = jnp.tril(jnp.ones((tc, tc), F32)).astype(BF16)
    full = lambda shape: pl.BlockSpec(shape, lambda c: (0,) * len(shape))
    rows = tc * SSM_CHUNKS_PER_STEP
    return pl.pallas_call(
        _ssm_kernel,
        out_shape=jax.ShapeDtypeStruct((l, w), BF16),
        grid=(l // rows,),
        in_specs=[pl.BlockSpec((rows, w), lambda c: (c, 0)),
                  full((SSM_NBLK, SSM_BLK_CH, SSM_BLK_ST)), full((SSM_NBLK, SSM_BLK_CH, SSM_BLK_ST)),
                  full((SSM_NBLK, SSM_BLK_ST, SSM_BLK_CH)), full((SSM_NBLK, SSM_BLK_ST, SSM_BLK_CH)),
                  full((tc, SSM_STATES)), full((tc, SSM_STATES)),
                  full((tc, SSM_STATES)), full((tc, SSM_STATES)),
                  full((1, SSM_STATES)), full((1, SSM_STATES)),
                  full((1, w)), full((tc, tc))],
        out_specs=pl.BlockSpec((rows, w), lambda c: (c, 0)),
        scratch_shapes=[pltpu.VMEM((1, SSM_STATES), F32), pltpu.VMEM((1, SSM_STATES), F32)],
        compiler_params=_params(("arbitrary",)),
        name="s5_ssm",
    )(s_in, tabs['bre'], tabs['bim'], tabs['cre'], tabs['cim'], tabs['enr'], tabs['eni'],
      tabs['epr'], tabs['epi'], tabs['lamr'], tabs['lami'], d_skip.reshape(1, w).astype(F32), tri)


def _merge_kernel(attn_ref, y_ref, u_ref, wa_ref, wg1_ref, wg2_ref, wia_ref, wis_ref, o_ref):
    a = jnp.dot(attn_ref[...], wa_ref[...], preferred_element_type=F32)
    y = y_ref[...]
    s = (jnp.dot(y, wg1_ref[...], preferred_element_type=F32)
         * jax.nn.sigmoid(jnp.dot(y, wg2_ref[...], preferred_element_type=F32)))
    u = u_ref[...]
    ga = jax.nn.sigmoid(jnp.dot(u, wia_ref[...], preferred_element_type=F32))
    gs = jax.nn.sigmoid(jnp.dot(u, wis_ref[...], preferred_element_type=F32))
    o_ref[...] = (ga * a + gs * s).astype(o_ref.dtype)


def _merge(attn, y_ssm, u, w_attn_out, w_ssm_glu, w_in):
    l, d = u.shape
    tm, tn = 512, 512
    nd = d // tn
    go = GATE_OFF // tn
    return pl.pallas_call(
        _merge_kernel,
        out_shape=jax.ShapeDtypeStruct((l, d), BF16),
        grid=(nd, l // tm),
        in_specs=[pl.BlockSpec((tm, ATTN_OUT_WIDTH), lambda j, i: (i, 0)),
                  pl.BlockSpec((tm, SSM_WIDTH), lambda j, i: (i, 0)),
                  pl.BlockSpec((tm, d), lambda j, i: (i, 0)),
                  pl.BlockSpec((ATTN_OUT_WIDTH, tn), lambda j, i: (0, j)),
                  pl.BlockSpec((SSM_WIDTH, tn), lambda j, i: (0, j)),
                  pl.BlockSpec((SSM_WIDTH, tn), lambda j, i: (0, j + nd)),
                  pl.BlockSpec((d, tn), lambda j, i: (0, j + go)),
                  pl.BlockSpec((d, tn), lambda j, i: (0, j + go + nd))],
        out_specs=pl.BlockSpec((tm, tn), lambda j, i: (i, j)),
        compiler_params=_params(("arbitrary", "arbitrary")),
        name="branch_merge",
    )(attn, y_ssm, u, w_attn_out, w_ssm_glu, w_ssm_glu, w_in, w_in)


def _layer_norm(h, g, b):
    mu = jnp.mean(h, axis=-1, keepdims=True)
    c = h - mu
    var = jnp.mean(c * c, axis=-1, keepdims=True)
    return c * lax.rsqrt(var + LN_EPS) * g + b


def _pack_rows(v):
    bits = lax.bitcast_convert_type(v.astype(BF16).astype(F32), jnp.uint32)
    half = bits.shape[1] // 2
    return (bits[:, :half] >> 16) | (bits[:, half:] & jnp.uint32(0xFFFF0000))


def _unpack_lo(xp):
    return lax.bitcast_convert_type(xp << 16, F32)


def _unpack_hi(xp):
    return lax.bitcast_convert_type(xp & jnp.uint32(0xFFFF0000), F32)


def _store_tile_rows(ref, packed):
    rows = packed.shape[0]
    for s in range(SUBLANES):
        ref[pl.ds(s, rows, stride=SUBLANES), :] = packed[:, s * LANES:(s + 1) * LANES]


def _load_tile_rows(ref, s):
    return ref[pl.ds(s, ref.shape[0] // SUBLANES, stride=SUBLANES), :]


def _outproj_kernel(mg_ref, x_ref, wo_ref, wr_ref, g1_ref, lg_ref, lb_ref, sc_ref, sh_ref,
                    x1_ref, u2_ref, u2p_ref, sco_ref):
    mix = jnp.dot(mg_ref[...], wo_ref[...], preferred_element_type=F32)
    x1 = _layer_norm(DEEPNORM_ALPHA * x_ref[...] + g1_ref[...] * mix, lg_ref[...], lb_ref[...])
    x1_ref[...] = x1
    u2 = (x1 * (1.0 + sc_ref[...]) + sh_ref[...]).astype(BF16)
    u2_ref[...] = u2
    _store_tile_rows(u2p_ref, _pack_rows(u2))
    logits = lax.dot_general(wr_ref[...], u2, (((1,), (1,)), ((), ())), preferred_element_type=F32)
    sco_ref[...] = jax.nn.sigmoid(logits)


def _outproj(merged, x, w_o, w_router_t, gate1, ln_g, ln_b, scale2, shift2):
    l, d = x.shape
    assert d == 2 * SUBLANES * LANES
    e = w_router_t.shape[0]
    tm = 512
    row = lambda w: pl.BlockSpec((tm, w), lambda i: (i, 0))
    vec = pl.BlockSpec((1, d), lambda i: (0, 0))
    return pl.pallas_call(
        _outproj_kernel,
        out_shape=(jax.ShapeDtypeStruct((l, d), F32), jax.ShapeDtypeStruct((l, d), BF16),
                   jax.ShapeDtypeStruct((l * SUBLANES, LANES), jnp.uint32),
                   jax.ShapeDtypeStruct((e, l), F32)),
        grid=(l // tm,),
        in_specs=[row(d), row(d), pl.BlockSpec((d, d), lambda i: (0, 0)),
                  pl.BlockSpec((e, d), lambda i: (0, 0)), vec, vec, vec, vec, vec],
        out_specs=(row(d), row(d), pl.BlockSpec((tm * SUBLANES, LANES), lambda i: (i, 0)),
                   pl.BlockSpec((e, tm), lambda i: (0, i))),
        compiler_params=_params(("arbitrary",)),
        name="outproj_ln1_router",
    )(merged, x, w_o, w_router_t, gate1, ln_g, ln_b, scale2, shift2)


ROUTE_TILE = 512
EXPERTS_PER_GROUP = N_EXPERTS // N_EXPERT_GROUPS


def _route_kernel(s_ref, b_ref, tri_ref, idx_ref, w_ref, rank_ref, cnt_ref, carry_ref):
    @pl.when(pl.program_id(0) == 0)
    def _():
        carry_ref[...] = jnp.zeros_like(carry_ref)

    ne, tm = s_ref.shape
    neg_inf = -jnp.inf
    s = s_ref[...]
    sel = s + b_ref[...]
    e_iota = lax.broadcasted_iota(jnp.int32, (ne, tm), 0)

    gs_rows = []
    for g in range(N_EXPERT_GROUPS):
        xg = sel[g * EXPERTS_PER_GROUP:(g + 1) * EXPERTS_PER_GROUP]
        m1 = jnp.max(xg, axis=0, keepdims=True)
        n1 = jnp.sum((xg == m1).astype(F32), axis=0, keepdims=True)
        m2 = jnp.max(jnp.where(xg < m1, xg, neg_inf), axis=0, keepdims=True)
        gs_rows.append(m1 + jnp.where(n1 >= 2.0, m1, m2))
    gs = jnp.concatenate(gs_rows, axis=0)

    g_iota = lax.broadcasted_iota(jnp.int32, gs.shape, 0)
    beaten = jnp.zeros(gs.shape, jnp.int32)
    for g2 in range(N_EXPERT_GROUPS):
        row = gs[g2:g2 + 1]
        beats = (row > gs) | ((row == gs) & (g2 < g_iota))
        beaten = beaten + beats.astype(jnp.int32)
    g_ok = beaten < TOPK_GROUPS
    work = jnp.concatenate(
        [jnp.where(g_ok[g:g + 1], sel[g * EXPERTS_PER_GROUP:(g + 1) * EXPERTS_PER_GROUP], neg_inf)
         for g in range(N_EXPERT_GROUPS)], axis=0)

    idxs, vals = [], []
    chosen = jnp.zeros((ne, tm), F32)
    for _ in range(TOP_K):
        m = jnp.max(work, axis=0, keepdims=True)
        i = jnp.min(jnp.where(work == m, e_iota, ne), axis=0, keepdims=True)
        onehot = e_iota == i
        idxs.append(i)
        vals.append(jnp.sum(jnp.where(onehot, s, 0.0), axis=0, keepdims=True))
        chosen = jnp.where(onehot, 1.0, chosen)
        work = jnp.where(onehot, neg_inf, work)
    wsum = vals[0]
    for v in vals[1:]:
        wsum = wsum + v

    before = jnp.dot(chosen.astype(BF16), tri_ref[...], preferred_element_type=F32) + carry_ref[...]
    ranks = [jnp.sum(jnp.where(e_iota == i, before, 0.0), axis=0, keepdims=True) for i in idxs]
    carry_ref[...] = carry_ref[...] + jnp.sum(chosen, axis=1, keepdims=True)

    idx_ref[...] = jnp.concatenate(idxs, axis=0)
    w_ref[...] = jnp.concatenate([v / wsum * ROUTED_SCALE for v in vals], axis=0)
    rank_ref[...] = jnp.concatenate(ranks, axis=0).astype(jnp.int32)
    cnt_ref[...] = carry_ref[...]


def _route(scores_t, router_bias):
    ne, t = scores_t.shape
    tm = ROUTE_TILE
    tri = jnp.triu(jnp.ones((tm, tm), F32), k=1).astype(BF16)
    tok = pl.BlockSpec((TOP_K, tm), lambda i: (0, i))
    return pl.pallas_call(
        _route_kernel,
        out_shape=(jax.ShapeDtypeStruct((TOP_K, t), jnp.int32), jax.ShapeDtypeStruct((TOP_K, t), F32),
                   jax.ShapeDtypeStruct((TOP_K, t), jnp.int32), jax.ShapeDtypeStruct((ne, 1), F32)),
        grid=(t // tm,),
        in_specs=[pl.BlockSpec((ne, tm), lambda i: (0, i)),
                  pl.BlockSpec((ne, 1), lambda i: (0, 0)),
                  pl.BlockSpec((tm, tm), lambda i: (0, 0))],
        out_specs=(tok, tok, tok, pl.BlockSpec((ne, 1), lambda i: (0, 0))),
        scratch_shapes=[pltpu.VMEM((ne, 1), F32)],
        compiler_params=_params(("arbitrary",)),
        name="moe_route",
    )(scores_t, router_bias.astype(F32).reshape(ne, 1), tri)


def _positions_kernel(ps_ref, idx_ref, rank_ref, pos_ref):
    idx = idx_ref[...]

    def add_expert(e, acc):
        return acc + jnp.where(idx == e, ps_ref[e], 0)

    start = lax.fori_loop(0, ps_ref.shape[0], add_expert, jnp.zeros(idx.shape, jnp.int32))
    pos_ref[...] = (start + rank_ref[...]) * SUBLANES


def _positions(idx, rank, pad_start):
    k, t = idx.shape
    tm = 1024
    tok = pl.BlockSpec((k, tm), lambda i: (0, i))
    return pl.pallas_call(
        _positions_kernel,
        out_shape=jax.ShapeDtypeStruct((k, t), jnp.int32),
        grid=(t // tm,),
        in_specs=[pl.BlockSpec(memory_space=pltpu.SMEM), tok, tok],
        out_specs=tok,
        compiler_params=_params(("arbitrary",)),
        name="moe_positions",
    )(pad_start, idx, rank)


DISPATCH_TILE = 512


def _dispatch_kernel(pos_ref, ps_ref, pe_ref, u_ref, xs_ref, zero_ref, sem, zsem):
    tm = u_ref.shape[0] // SUBLANES
    ne = ps_ref.shape[0]
    block_words = MOE_BLOCK * SUBLANES

    def tile_row(r):
        return pl.ds(pl.multiple_of(r * SUBLANES, SUBLANES), SUBLANES)

    @pl.when(pl.program_id(0) == 0)
    def _():
        zero_ref[...] = jnp.zeros_like(zero_ref)

        def block_copy(start):
            start = pl.multiple_of(start * SUBLANES, block_words)
            return pltpu.make_async_copy(zero_ref, xs_ref.at[pl.ds(start, block_words)], zsem)

        def fill(e, carry):
            @pl.when(pe_ref[e] > ps_ref[e])
            def _():
                block_copy(pe_ref[e] - MOE_BLOCK).start()
            return carry

        def fill_done(e, carry):
            @pl.when(pe_ref[e] > ps_ref[e])
            def _():
                block_copy(pe_ref[e] - MOE_BLOCK).wait()
            return carry

        lax.fori_loop(0, ne, fill, 0)
        lax.fori_loop(0, ne, fill_done, 0)

        first_unused = pe_ref[ne - 1] // MOE_BLOCK
        n_blocks = xs_ref.shape[0] // block_words

        def fill_unused(b, carry):
            block_copy(b * MOE_BLOCK).start()
            return carry

        def fill_unused_done(b, carry):
            block_copy(b * MOE_BLOCK).wait()
            return carry

        lax.fori_loop(first_unused, n_blocks, fill_unused, 0)
        lax.fori_loop(first_unused, n_blocks, fill_unused_done, 0)

    def issue(pair, carry):
        for t in (2 * pair, 2 * pair + 1):
            for k in range(TOP_K):
                dst = pl.ds(pl.multiple_of(pos_ref[k, t], SUBLANES), SUBLANES)
                pltpu.make_async_copy(u_ref.at[tile_row(t)], xs_ref.at[dst], sem).start(priority=k % 2)
        return carry

    lax.fori_loop(0, tm // 2, issue, 0)
    for _ in range(TOP_K):
        pltpu.make_async_copy(u_ref, xs_ref.at[pl.ds(0, tm * SUBLANES)], sem).wait()


def _dispatch(pos, pad_start, pad_end, u2p, n_rows):
    t = u2p.shape[0] // SUBLANES
    tm = DISPATCH_TILE
    smem_tok = pl.BlockSpec((TOP_K, tm), lambda i: (0, i), memory_space=pltpu.SMEM)
    smem_all = pl.BlockSpec(memory_space=pltpu.SMEM)
    return pl.pallas_call(
        _dispatch_kernel,
        out_shape=jax.ShapeDtypeStruct((n_rows * SUBLANES, LANES), jnp.uint32),
        grid=(t // tm,),
        in_specs=[smem_tok, smem_all, smem_all,
                  pl.BlockSpec((tm * SUBLANES, LANES), lambda i: (i, 0))],
        out_specs=pl.BlockSpec(memory_space=pl.ANY),
        scratch_shapes=[pltpu.VMEM((MOE_BLOCK * SUBLANES, LANES), jnp.uint32),
                        pltpu.SemaphoreType.DMA, pltpu.SemaphoreType.DMA],
        compiler_params=_params(("arbitrary",)),
        name="moe_dispatch",
    )(pos, pad_start, pad_end, u2p)


WEIGHT_CHUNK_BYTES = 2 * 1024 * 1024


def _expert_kernel(be_ref, bv_ref, nx_ref, nx2_ref, sl_ref, x_ref, win_ref, wout_ref, y_ref,
                   wfi_ref, wfo_ref, wbi_ref, wbo_ref, sem):
    b = pl.program_id(0)
    valid = bv_ref[b] > 0
    e = be_ref[b]
    slot = sl_ref[b]
    new_expert = (b == 0) | (e != be_ref[jnp.maximum(b - 1, 0)])

    def weight_chunks():
        chunks = []
        for hbm, land, wb, j in ((win_ref, wfi_ref, wbi_ref, 0), (wout_ref, wfo_ref, wbo_ref, 1)):
            rows = WEIGHT_CHUNK_BYTES // (hbm.shape[2] * 4)
            for c in range(hbm.shape[1] // rows):
                chunks.append((hbm, land, wb, j, pl.ds(c * rows, rows)))
        return chunks

    def chunk_copy(chunk, ex, s):
        hbm, land, _, j, rs = chunk
        return pltpu.make_async_copy(hbm.at[ex, rs], land.at[s, rs], sem.at[s, j])

    def start_weights(ex, s):
        for n, chunk in enumerate(weight_chunks()):
            chunk_copy(chunk, ex, s).start(priority=n % 2)

    @pl.when(b == 0)
    def _():
        start_weights(e, slot)

        @pl.when(nx_ref[b] >= 0)
        def _():
            start_weights(nx_ref[b], 1 - slot)

    @pl.when(valid & new_expert)
    def _():
        for chunk in weight_chunks():
            chunk_copy(chunk, e, slot).wait()
        for n, chunk in enumerate(weight_chunks()):
            _, land, wb, _, rs = chunk
            for s in range(2):
                @pl.when(slot == s)
                def _():
                    wb[rs] = land[s, rs].astype(BF16)

            @pl.when(nx2_ref[b] >= 0)
            def _():
                chunk_copy(chunk, nx2_ref[b], slot).start(priority=n % 2)

    @pl.when(valid)
    def _():
        words = [_load_tile_rows(x_ref, s) for s in range(SUBLANES)]
        lo = jnp.concatenate([_unpack_lo(wd).astype(BF16) for wd in words], axis=1)
        hi = jnp.concatenate([_unpack_hi(wd).astype(BF16) for wd in words], axis=1)
        half = lo.shape[1]
        h = (jnp.dot(lo, wbi_ref[:half], preferred_element_type=F32)
             + jnp.dot(hi, wbi_ref[half:], preferred_element_type=F32))
        hg, hu = h[:, :EXPERT_FF], h[:, EXPERT_FF:]
        act = (hg * jax.nn.sigmoid(hg) * hu).astype(BF16)
        _store_tile_rows(y_ref, _pack_rows(jnp.dot(act, wbo_ref[...], preferred_element_type=F32)))

    @pl.when(jnp.logical_not(valid))
    def _():
        y_ref[...] = jnp.zeros_like(y_ref)


def _experts(block_expert, block_valid, block_next, block_next2, block_slot, x_rows, e_w_in, e_w_out):
    n_rows = x_rows.shape[0] // SUBLANES
    nb = n_rows // MOE_BLOCK
    _, d, ff2 = e_w_in.shape
    ff = ff2 // 2
    rows_spec = pl.BlockSpec((MOE_BLOCK * SUBLANES, LANES), lambda b, *_: (b, 0))
    grid_spec = pltpu.PrefetchScalarGridSpec(
        num_scalar_prefetch=5,
        grid=(nb,),
        in_specs=[rows_spec,
                  pl.BlockSpec(memory_space=pl.ANY),
                  pl.BlockSpec(memory_space=pl.ANY)],
        out_specs=rows_spec,
        scratch_shapes=[pltpu.VMEM((2, d, ff2), F32), pltpu.VMEM((2, ff, d), F32),
                        pltpu.VMEM((d, ff2), BF16), pltpu.VMEM((ff, d), BF16),
                        pltpu.SemaphoreType.DMA((2, 2))],
    )
    return pl.pallas_call(
        _expert_kernel,
        out_shape=jax.ShapeDtypeStruct((n_rows * SUBLANES, LANES), jnp.uint32),
        grid_spec=grid_spec,
        compiler_params=_params(("arbitrary",)),
        name="routed_experts",
    )(block_expert, block_valid, block_next, block_next2, block_slot, x_rows, e_w_in, e_w_out)


COMBINE_TILE = 256


def _final_kernel(pos_ref, posn_ref, u2_ref, x1_ref, w_ref, ys_ref,
                  win_ref, wout_ref, g2_ref, lg_ref, lb_ref, o_ref, buf0_ref, buf1_ref, sem):
    tm = u2_ref.shape[0]
    i = pl.program_id(0)
    bufs = (buf0_ref, buf1_ref)

    def tile_row(r):
        return pl.ds(pl.multiple_of(r * SUBLANES, SUBLANES), SUBLANES)

    def gather(slots_ref, par):
        def issue(pair, carry):
            for t in (2 * pair, 2 * pair + 1):
                for k in range(TOP_K):
                    src = pl.ds(pl.multiple_of(slots_ref[k, t], SUBLANES), SUBLANES)
                    pltpu.make_async_copy(ys_ref.at[src], bufs[par].at[k, tile_row(t)],
                                          sem.at[par]).start(priority=k % 2)
            return carry

        lax.fori_loop(0, tm // 2, issue, 0)

    @pl.when(i == 0)
    def _():
        gather(pos_ref, 0)

    for par in range(2):
        @pl.when((i + 1 < pl.num_programs(0)) & (i % 2 == par))
        def _():
            gather(posn_ref, 1 - par)

    h = jnp.dot(u2_ref[...], win_ref[...], preferred_element_type=F32)
    hg, hu = h[:, :SHARED_FF], h[:, SHARED_FF:]
    act = (hg * jax.nn.sigmoid(hg) * hu).astype(BF16)
    shared = jnp.dot(act, wout_ref[...], preferred_element_type=F32)

    w = w_ref[...]
    wk = [w[:, k:k + 1] for k in range(TOP_K)]

    for par in range(2):
        @pl.when(i % 2 == par)
        def _():
            buf = bufs[par]
            for k in range(TOP_K):
                pltpu.make_async_copy(ys_ref.at[pl.ds(0, tm * SUBLANES)], buf.at[k], sem.at[par]).wait()
            lo_parts, hi_parts = [], []
            for s in range(SUBLANES):
                lo = jnp.zeros((tm, LANES), F32)
                hi = jnp.zeros((tm, LANES), F32)
                for k in range(TOP_K):
                    words = _load_tile_rows(buf.at[k], s)
                    lo = lo + wk[k] * _unpack_lo(words)
                    hi = hi + wk[k] * _unpack_hi(words)
                lo_parts.append(lo)
                hi_parts.append(hi)
            ffn = shared + jnp.concatenate(lo_parts + hi_parts, axis=1)
            o_ref[...] = _layer_norm(DEEPNORM_ALPHA * x1_ref[...] + g2_ref[...] * ffn,
                                     lg_ref[...], lb_ref[...])


def _final(pos, u2, x1, w_tok, y_rows, s_w_in, s_w_out, gate2, ln_g, ln_b):
    l, d = x1.shape
    tm = COMBINE_TILE
    row = pl.BlockSpec((tm, d), lambda i: (i, 0))
    vec = pl.BlockSpec((1, d), lambda i: (0, 0))
    n_tiles = l // tm
    smem_tok = pl.BlockSpec((TOP_K, tm), lambda i: (0, i), memory_space=pltpu.SMEM)
    smem_next = pl.BlockSpec((TOP_K, tm), lambda i: (0, jnp.minimum(i + 1, n_tiles - 1)),
                             memory_space=pltpu.SMEM)
    row_buf = pltpu.VMEM((TOP_K, tm * SUBLANES, LANES), jnp.uint32)
    return pl.pallas_call(
        _final_kernel,
        out_shape=jax.ShapeDtypeStruct((l, d), F32),
        grid=(n_tiles,),
        in_specs=[smem_tok, smem_next, row, row, pl.BlockSpec((tm, TOP_K), lambda i: (i, 0)),
                  pl.BlockSpec(memory_space=pl.ANY),
                  pl.BlockSpec(s_w_in.shape, lambda i: (0, 0)),
                  pl.BlockSpec(s_w_out.shape, lambda i: (0, 0)), vec, vec, vec],
        out_specs=row,
        scratch_shapes=[row_buf, row_buf, pltpu.SemaphoreType.DMA((2,))],
        compiler_params=_params(("arbitrary",)),
        name="combine_shared_ln2",
    )(pos, pos, u2, x1, w_tok, y_rows, s_w_in, s_w_out, gate2, ln_g, ln_b)


def _block_layout(counts, n_tokens):
    padded = (counts + MOE_BLOCK - 1) // MOE_BLOCK * MOE_BLOCK
    pad_ends = jnp.cumsum(padded)
    pad_starts = (pad_ends - padded).astype(jnp.int32)
    n_rows = -(-(n_tokens * TOP_K + N_EXPERTS * (MOE_BLOCK - 1)) // MOE_BLOCK) * MOE_BLOCK
    block_start = jnp.arange(n_rows // MOE_BLOCK, dtype=jnp.int32) * MOE_BLOCK
    block_expert = jnp.minimum(jnp.sum((block_start[:, None] >= pad_ends[None, :]).astype(jnp.int32), axis=1),
                               N_EXPERTS - 1).astype(jnp.int32)
    block_valid = (block_start < pad_ends[-1]).astype(jnp.int32)
    ar = jnp.arange(N_EXPERTS, dtype=jnp.int32)
    has = counts > 0
    later = (ar[None, :] > ar[:, None]) & has[None, :]
    next_has = jnp.min(jnp.where(later, ar[None, :], N_EXPERTS), axis=1)
    next_has = jnp.where(next_has >= N_EXPERTS, -1, next_has)
    after = (ar[None, :] == next_has[:, None])
    next2_has = jnp.sum(jnp.where(after, next_has[None, :], 0), axis=1)
    next2_has = jnp.where(next_has < 0, -1, next2_has)
    ordinal = jnp.cumsum(has.astype(jnp.int32)) - 1
    mine = block_expert[:, None] == ar[None, :]
    pick = lambda v: jnp.sum(jnp.where(mine, v[None, :], 0), axis=1).astype(jnp.int32)
    return (pad_starts, pad_ends.astype(jnp.int32), block_expert, block_valid, pick(next_has),
            pick(next2_has), pick(ordinal) % 2, n_rows)


def kernel(x, c, w_ada, b_ada, w_in, rel_bias, ssm_a_re, ssm_a_im, ssm_log_dt, ssm_b_re, ssm_b_im, ssm_c_re, ssm_c_im, ssm_d, w_attn_out, w_ssm_glu, w_o, ln1_g, ln1_b, w_router, router_bias, e_w_in, e_w_out, s_w_in, s_w_out, ln2_g, ln2_b):
    bsz, l, d = x.shape
    assert bsz == 1
    xf = x.reshape(l, d)
    i = 0
    mod = _modulation(c, w_ada[i], b_ada[i])
    shift1, scale1, gate1, shift2, scale2, gate2 = [mod[:, k * d:(k + 1) * d] for k in range(6)]

    w_in_b = w_in[i].astype(BF16)
    u = _modulate(xf, scale1, shift1)
    s_in = _matmul(u, w_in_b, QKV_WIDTH, SSM_WIDTH, 512, 512, F32, "in_proj_ssm")

    outs, lses = [], []
    for gi, (window, dilation) in enumerate(DILATED_GROUPS):
        bias_prev, bias_cur = _attn_bias_tables(rel_bias, gi, window, dilation)
        qkv = _inproj_qkv_group(u, w_in_b, gi, dilation)
        o, s = _attention_group(qkv, bias_prev, bias_cur, gi)
        outs.append(o)
        lses.append(s)
    attn = _attn_combine(outs, lses)

    tabs = _ssm_tables(ssm_a_re[i], ssm_a_im[i], ssm_log_dt[i], ssm_b_re[i], ssm_b_im[i],
                       ssm_c_re[i], ssm_c_im[i])
    y_ssm = _ssm(s_in, tabs, ssm_d[i])

    merged = _merge(attn, y_ssm, u, w_attn_out[i].astype(BF16), w_ssm_glu[i].astype(BF16), w_in_b)
    x1, u2, u2p, scores_t = _outproj(merged, xf, w_o[i].astype(BF16), w_router[i].T.astype(BF16), gate1,
                                     ln1_g[i].reshape(1, d), ln1_b[i].reshape(1, d), scale2, shift2)

    idx, w, rank, counts = _route(scores_t, router_bias[i])
    (pad_start, pad_end, block_expert, block_valid, block_next, block_next2, block_slot,
     n_rows) = _block_layout(counts[:, 0].astype(jnp.int32), l)
    pos = _positions(idx, rank, pad_start)
    x_rows = _dispatch(pos, pad_start, pad_end, u2p, n_rows)
    y_rows = _experts(block_expert, block_valid, block_next, block_next2, block_slot, x_rows,
                      e_w_in[i], e_w_out[i])
    out = _final(pos, u2, x1, w.T, y_rows, s_w_in[i].astype(BF16),
                 s_w_out[i].astype(BF16), gate2, ln2_g[i].reshape(1, d), ln2_b[i].reshape(1, d))
    return out.reshape(bsz, l, d)
```

```python
import functools
import math

import jax
import jax.numpy as jnp
from jax import lax
from jax.experimental import pallas as pl
from jax.experimental.pallas import tpu as pltpu

F32 = jnp.float32
BF16 = jnp.bfloat16

D_MODEL = 2048
HEAD_DIM = 128
HEADS_PER_GROUP = 4
DILATED_GROUPS = ((128, 1), (512, 4), (2048, 16))
N_ATTN_GROUPS = len(DILATED_GROUPS)
N_ATTN_HEADS = N_ATTN_GROUPS * HEADS_PER_GROUP
ATTN_WIDTH = N_ATTN_HEADS * HEAD_DIM
ATTN_OUT_WIDTH = HEADS_PER_GROUP * HEAD_DIM
ATTN_BLOCK = 128
REL_BUCKETS = 32
REL_MAX_DISTANCE = 2048
SSM_GROUP_CH = 16
SSM_STATE = 64
SSM_WIDTH = 1024
SSM_GROUPS = SSM_WIDTH // SSM_GROUP_CH
A_RE_MAX = -1e-4
QKV_WIDTH = 3 * ATTN_WIDTH
GATE_OFF = QKV_WIDTH + SSM_WIDTH
N_EXPERTS = 256
TOP_K = 8
N_EXPERT_GROUPS = 8
TOPK_GROUPS = 4
EXPERT_FF = 512
SHARED_FF = 512
ROUTED_SCALE = 2.5
MOE_BLOCK = 128
DEPTH = 1
DEEPNORM_ALPHA = (2 * DEPTH) ** 0.25
LN_EPS = 1e-5
NEG_BIG = -1e30
LANES = 128
SUBLANES = 8

SSM_CHUNK = 128
SSM_CHUNKS_PER_STEP = 4
SSM_BLK_GROUPS = 16
SSM_NBLK = SSM_GROUPS // SSM_BLK_GROUPS
SSM_BLK_CH = SSM_BLK_GROUPS * SSM_GROUP_CH
SSM_BLK_ST = SSM_BLK_GROUPS * SSM_STATE
SSM_STATES = SSM_GROUPS * SSM_STATE

VMEM_LIMIT = 56 * 1024 * 1024


def _params(sem, vmem=VMEM_LIMIT):
    return pltpu.CompilerParams(dimension_semantics=sem, vmem_limit_bytes=vmem)


def _mod_kernel(c_ref, w_ref, b_ref, o_ref):
    c = c_ref[...]
    cond = c * jax.nn.sigmoid(c)
    o_ref[...] = jnp.sum(cond * w_ref[...], axis=0, keepdims=True) + b_ref[...]


def _modulation(c, w_ada, b_ada):
    d, n = w_ada.shape
    assert c.shape == (1, d)
    tn = 1024
    return pl.pallas_call(
        _mod_kernel,
        out_shape=jax.ShapeDtypeStruct((1, n), F32),
        grid=(n // tn,),
        in_specs=[pl.BlockSpec((d, 1), lambda j: (0, 0)),
                  pl.BlockSpec((d, tn), lambda j: (0, j)),
                  pl.BlockSpec((1, tn), lambda j: (0, j))],
        out_specs=pl.BlockSpec((1, tn), lambda j: (0, j)),
        compiler_params=_params(("arbitrary",)),
        name="adaln_mod",
    )(c.reshape(d, 1), w_ada, b_ada.reshape(1, n))


def _modulate_kernel(x_ref, sc_ref, sh_ref, o_ref):
    o_ref[...] = (x_ref[...] * (1.0 + sc_ref[...]) + sh_ref[...]).astype(o_ref.dtype)


def _modulate(x, scale, shift):
    m, d = x.shape
    tm = 512
    return pl.pallas_call(
        _modulate_kernel,
        out_shape=jax.ShapeDtypeStruct((m, d), BF16),
        grid=(m // tm,),
        in_specs=[pl.BlockSpec((tm, d), lambda i: (i, 0)),
                  pl.BlockSpec((1, d), lambda i: (0, 0)),
                  pl.BlockSpec((1, d), lambda i: (0, 0))],
        out_specs=pl.BlockSpec((tm, d), lambda i: (i, 0)),
        compiler_params=_params(("arbitrary",)),
        name="modulate",
    )(x, scale, shift)


def _mm_kernel(a_ref, w_ref, o_ref):
    o_ref[...] = jnp.dot(a_ref[...], w_ref[...], preferred_element_type=F32).astype(o_ref.dtype)


def _matmul(a, w, col_off, n, tm, tn, out_dtype, name):
    m, k = a.shape
    assert col_off % tn == 0 and n % tn == 0 and m % tm == 0
    off_blocks = col_off // tn
    return pl.pallas_call(
        _mm_kernel,
        out_shape=jax.ShapeDtypeStruct((m, n), out_dtype),
        grid=(n // tn, m // tm),
        in_specs=[pl.BlockSpec((tm, k), lambda j, i: (i, 0)),
                  pl.BlockSpec((k, tn), lambda j, i: (0, j + off_blocks))],
        out_specs=pl.BlockSpec((tm, tn), lambda j, i: (i, j)),
        compiler_params=_params(("arbitrary", "arbitrary")),
        name=name,
    )(a, w)


ATTN_QBLOCKS = 4


def _attn_kernel(bp_ref, bc_ref, q_ref, kp_ref, kc_ref, vp_ref, vc_ref, o_ref, lse_ref):
    scale = HEAD_DIM ** -0.5
    nt = (((1,), (1,)), ((), ()))
    first = pl.program_id(1) == 0
    work = []
    for h in range(HEADS_PER_GROUP):
        hs = slice(h * HEAD_DIM, (h + 1) * HEAD_DIM)
        for qb in range(ATTN_QBLOCKS):
            rows = slice(qb * ATTN_BLOCK, (qb + 1) * ATTN_BLOCK)
            before = slice((qb - 1) * ATTN_BLOCK, qb * ATTN_BLOCK)
            work.append((h, hs, qb, rows, before))

    scores = []
    for h, hs, qb, rows, before in work:
        k_prev = kp_ref[:, hs] if qb == 0 else kc_ref[before, hs]
        q = q_ref[rows, hs]
        s_p = lax.dot_general(q, k_prev, nt, preferred_element_type=F32) * scale + bp_ref[h]
        s_c = lax.dot_general(q, kc_ref[rows, hs], nt, preferred_element_type=F32) * scale + bc_ref[h]
        if qb == 0:
            s_p = jnp.where(first, NEG_BIG, s_p)
        scores.append((s_p, s_c))
    maxes = [jnp.maximum(jnp.max(s_p, axis=-1, keepdims=True), jnp.max(s_c, axis=-1, keepdims=True))
             for s_p, s_c in scores]
    probs = [(jnp.exp(s_p - m), jnp.exp(s_c - m)) for (s_p, s_c), m in zip(scores, maxes)]
    sums = [jnp.sum(p_p, axis=-1, keepdims=True) + jnp.sum(p_c, axis=-1, keepdims=True)
            for p_p, p_c in probs]
    outs = []
    for (h, hs, qb, rows, before), (p_p, p_c) in zip(work, probs):
        v_prev = vp_ref[:, hs] if qb == 0 else vc_ref[before, hs]
        outs.append(jnp.dot(p_p.astype(BF16), v_prev, preferred_element_type=F32)
                    + jnp.dot(p_c.astype(BF16), vc_ref[rows, hs], preferred_element_type=F32))
    for (h, hs, qb, rows, before), o, m, l in zip(work, outs, maxes, sums):
        o_ref[rows, hs] = o / l
        lse_ref[rows, hs] = jnp.broadcast_to(m + jnp.log(l), (ATTN_BLOCK, HEAD_DIM))


def _inproj_dilated_kernel(a_ref, wq_ref, wk_ref, wv_ref, o_ref, acc_ref):
    dilation, rows, _ = o_ref.shape
    a = a_ref[...]
    for j, w_ref in enumerate((wq_ref, wk_ref, wv_ref)):
        tn = w_ref.shape[1]
        res = jnp.dot(a, w_ref[...], preferred_element_type=F32)
        if dilation == 1:
            o_ref[0, :, j * tn:(j + 1) * tn] = res.astype(o_ref.dtype)
            continue
        for c in range(tn // LANES):
            acc_ref[c] = res[:, c * LANES:(c + 1) * LANES]
        for r in range(dilation):
            for c in range(tn // LANES):
                o_ref[r, :, j * tn + c * LANES:j * tn + (c + 1) * LANES] = (
                    acc_ref[c, pl.ds(r, rows, stride=dilation), :].astype(o_ref.dtype))


def _inproj_qkv_group(u, w_in, gi, dilation):
    l, k = u.shape
    tm, tn = 512, ATTN_OUT_WIDTH
    w_spec = lambda j: pl.BlockSpec((k, tn), lambda i: (0, j * N_ATTN_GROUPS + gi))
    return pl.pallas_call(
        _inproj_dilated_kernel,
        out_shape=jax.ShapeDtypeStruct((dilation, l // dilation, 3 * tn), BF16),
        grid=(l // tm,),
        in_specs=[pl.BlockSpec((tm, k), lambda i: (i, 0)), w_spec(0), w_spec(1), w_spec(2)],
        out_specs=pl.BlockSpec((dilation, tm // dilation, 3 * tn), lambda i: (0, i, 0)),
        scratch_shapes=[pltpu.VMEM((tn // LANES, tm, LANES), F32)],
        compiler_params=_params(("arbitrary",)),
        name=f"in_proj_qkv_g{gi}",
    )(u, w_in, w_in, w_in)


def _attention_group(qkv, bias_prev, bias_cur, gi):
    dilation, m, _ = qkv.shape
    tile = ATTN_QBLOCKS * ATTN_BLOCK

    def cur(cb):
        return pl.BlockSpec((None, tile, ATTN_OUT_WIDTH), lambda r, b: (r, b, cb))

    def prev(cb):
        return pl.BlockSpec((None, ATTN_BLOCK, ATTN_OUT_WIDTH),
                            lambda r, b: (r, jnp.maximum(b * ATTN_QBLOCKS - 1, 0), cb))

    bias_spec = pl.BlockSpec((HEADS_PER_GROUP, ATTN_BLOCK, ATTN_BLOCK), lambda r, b: (0, 0, 0))
    out_spec = pl.BlockSpec((None, tile, ATTN_OUT_WIDTH), lambda r, b: (r, b, 0))
    return pl.pallas_call(
        _attn_kernel,
        out_shape=(jax.ShapeDtypeStruct((dilation, m, ATTN_OUT_WIDTH), F32),
                   jax.ShapeDtypeStruct((dilation, m, ATTN_OUT_WIDTH), F32)),
        grid=(dilation, m // tile),
        in_specs=[bias_spec, bias_spec, cur(0), prev(1), cur(1), prev(2), cur(2)],
        out_specs=(out_spec, out_spec),
        compiler_params=_params(("arbitrary", "arbitrary")),
        name=f"dilated_attn_g{gi}",
    )(bias_prev, bias_cur, qkv, qkv, qkv, qkv, qkv)


def _t5_bucket(dist):
    max_exact = REL_BUCKETS // 2
    d_f = jnp.maximum(dist, 1).astype(F32)
    large = max_exact + (jnp.log(d_f / max_exact) / math.log(REL_MAX_DISTANCE / max_exact)
                         * (REL_BUCKETS - max_exact)).astype(jnp.int32)
    large = jnp.minimum(large, REL_BUCKETS - 1)
    return jnp.where(dist < max_exact, dist, large)


def _attn_bias_tables(rel_bias, gi, window, dilation):
    n_keys = window // dilation
    bucket = _t5_bucket(jnp.arange(n_keys + 1, dtype=jnp.int32) * dilation)
    hs = slice(gi * HEADS_PER_GROUP, (gi + 1) * HEADS_PER_GROUP)
    hi = lax.Precision.HIGHEST
    pick_bucket = (bucket[:, None] == jnp.arange(REL_BUCKETS)[None, :]).astype(F32)
    by_dist = jnp.dot(pick_bucket, rel_bias[:, hs].astype(F32), precision=hi)
    qi = jnp.arange(ATTN_BLOCK)[:, None]
    kj = jnp.arange(ATTN_BLOCK)[None, :]
    d_prev = qi + ATTN_BLOCK - kj
    d_cur = qi - kj
    tabs = []
    for dist in (d_prev, d_cur):
        ok = (dist >= 0) & (dist <= n_keys)
        pick_dist = (dist[:, :, None] == jnp.arange(n_keys + 1)[None, None, :]).astype(F32)
        vals = jnp.einsum('ijd,dh->hij', pick_dist, by_dist, precision=hi)
        tabs.append(jnp.where(ok[None], vals, NEG_BIG))
    return tabs


ATTN_COMBINE_TILE = 512


def _attn_combine_kernel(o1, o2, o3, l1, l2, l3, out_ref, *scratch):
    def token_order(ref, buf, c):
        dilation, rows, _ = ref.shape
        cs = slice(c * LANES, (c + 1) * LANES)
        if dilation == 1:
            return ref[0, :, cs]
        for r in range(dilation):
            buf[c, pl.ds(r, rows, stride=dilation), :] = ref[r, :, cs]
        return buf[c]

    for c in range(out_ref.shape[1] // LANES):
        v1, v2, v3, a1, a2, a3 = [token_order(ref, buf, c)
                                  for ref, buf in zip((o1, o2, o3, l1, l2, l3), scratch)]
        m = jnp.maximum(jnp.maximum(a1, a2), a3)
        e1, e2, e3 = jnp.exp(a1 - m), jnp.exp(a2 - m), jnp.exp(a3 - m)
        num = e1 * v1 + e2 * v2 + e3 * v3
        out_ref[:, c * LANES:(c + 1) * LANES] = (num / (e1 + e2 + e3)).astype(out_ref.dtype)


def _attn_combine(outs, lses):
    w = outs[0].shape[-1]
    l = outs[0].shape[0] * outs[0].shape[1]
    tm = ATTN_COMBINE_TILE

    def spec(a):
        dilation = a.shape[0]
        return pl.BlockSpec((dilation, tm // dilation, w), lambda i: (0, i, 0))

    return pl.pallas_call(
        _attn_combine_kernel,
        out_shape=jax.ShapeDtypeStruct((l, w), BF16),
        grid=(l // tm,),
        in_specs=[spec(a) for a in (*outs, *lses)],
        out_specs=pl.BlockSpec((tm, w), lambda i: (i, 0)),
        scratch_shapes=[pltpu.VMEM((w // LANES, tm, LANES), F32)] * 6,
        compiler_params=_params(("arbitrary",)),
        name="attn_combine",
    )(*outs, *lses)


def _ssm_kernel(u_ref, bre_ref, bim_ref, cre_ref, cim_ref, enr_ref, eni_ref, epr_ref, epi_ref,
                lamr_ref, lami_ref, d_ref, tri_ref, y_ref, car_ref, cai_ref):
    @pl.when(pl.program_id(0) == 0)
    def _():
        car_ref[...] = jnp.zeros_like(car_ref)
        cai_ref[...] = jnp.zeros_like(cai_ref)

    tc = tri_ref.shape[0]
    tri = tri_ref[...]
    for j in range(SSM_NBLK):
        ch = slice(j * SSM_BLK_CH, (j + 1) * SSM_BLK_CH)
        st = slice(j * SSM_BLK_ST, (j + 1) * SSM_BLK_ST)
        u = u_ref[:, ch]
        ub = u.astype(BF16)
        bur = jnp.dot(ub, bre_ref[j], preferred_element_type=F32)
        bui = jnp.dot(ub, bim_ref[j], preferred_element_type=F32)
        enr, eni = enr_ref[:, st], eni_ref[:, st]
        epr, epi = epr_ref[:, st], epi_ref[:, st]
        lr, li = lamr_ref[:, st], lami_ref[:, st]
        cr, ci = car_ref[:, st], cai_ref[:, st]
        sr_chunks, si_chunks = [], []
        for q in range(u_ref.shape[0] // tc):
            rows = slice(q * tc, (q + 1) * tc)
            xr = bur[rows] * enr - bui[rows] * eni
            xi = bur[rows] * eni + bui[rows] * enr
            pr = jnp.dot(tri, xr.astype(BF16), preferred_element_type=F32)
            pi = jnp.dot(tri, xi.astype(BF16), preferred_element_type=F32)
            tr = pr + (lr * cr - li * ci)
            ti = pi + (lr * ci + li * cr)
            sr = epr * tr - epi * ti
            si = epr * ti + epi * tr
            cr, ci = sr[tc - 1:tc], si[tc - 1:tc]
            sr_chunks.append(sr.astype(BF16))
            si_chunks.append(si.astype(BF16))
        car_ref[:, st] = cr
        cai_ref[:, st] = ci
        y = (jnp.dot(jnp.concatenate(sr_chunks, axis=0), cre_ref[j], preferred_element_type=F32)
             - jnp.dot(jnp.concatenate(si_chunks, axis=0), cim_ref[j], preferred_element_type=F32))
        y_ref[:, ch] = (y + d_ref[:, ch] * u).astype(y_ref.dtype)


def _ssm_tables(a_re, a_im, log_dt, b_re, b_im, c_re, c_im):
    g, p, hc = b_re.shape
    lam_re = jnp.minimum(a_re.astype(F32), A_RE_MAX)
    lam_im = a_im.astype(F32)
    dt = jnp.exp(log_dt.astype(F32))[:, None]
    mag = jnp.exp(lam_re * dt)
    lb_re, lb_im = mag * jnp.cos(lam_im * dt), mag * jnp.sin(lam_im * dt)
    imag = jnp.exp(-lam_re * dt)
    li_re, li_im = imag * jnp.cos(lam_im * dt), -imag * jnp.sin(lam_im * dt)
    den = lam_re * lam_re + lam_im * lam_im
    nr, ni = lb_re - 1.0, lb_im
    f_re = (nr * lam_re + ni * lam_im) / den
    f_im = (ni * lam_re - nr * lam_im) / den
    bb_re = f_re[:, :, None] * b_re - f_im[:, :, None] * b_im
    bb_im = f_re[:, :, None] * b_im + f_im[:, :, None] * b_re

    def powers(pr, pi):
        er, ei = jnp.ones((1, g * p), F32), jnp.zeros((1, g * p), F32)
        pr, pi = pr.reshape(1, g * p), pi.reshape(1, g * p)
        while er.shape[0] < SSM_CHUNK:
            er, ei = (jnp.concatenate([er, er * pr - ei * pi], axis=0),
                      jnp.concatenate([ei, er * pi + ei * pr], axis=0))
            pr, pi = pr * pr - pi * pi, 2.0 * pr * pi
        return er, ei

    epr, epi = powers(lb_re, lb_im)
    enr, eni = powers(li_re, li_im)
    eye = jnp.eye(SSM_BLK_GROUPS, dtype=F32)

    def b_blocks(b):
        b = b.reshape(SSM_NBLK, SSM_BLK_GROUPS, p, hc)
        return jnp.einsum('jgph,gk->jghkp', b, eye).reshape(SSM_NBLK, SSM_BLK_CH, SSM_BLK_ST).astype(BF16)

    def c_blocks(c):
        c = c.astype(F32).reshape(SSM_NBLK, SSM_BLK_GROUPS, hc, p)
        return jnp.einsum('jghp,gk->jkpgh', c, eye).reshape(SSM_NBLK, SSM_BLK_ST, SSM_BLK_CH).astype(BF16)

    return dict(bre=b_blocks(bb_re), bim=b_blocks(bb_im), cre=c_blocks(c_re), cim=c_blocks(c_im),
                enr=enr, eni=eni, epr=epr, epi=epi,
                lamr=lb_re.reshape(1, g * p), lami=lb_im.reshape(1, g * p))


def _ssm(s_in, tabs, d_skip):
    l, w = s_in.shape
    tc = SSM_CHUNK
    tri = jnp.tril(jnp.ones((tc, tc), F32)).astype(BF16)
    full = lambda shape: pl.BlockSpec(shape, lambda c: (0,) * len(shape))
    rows = tc * SSM_CHUNKS_PER_STEP
    return pl.pallas_call(
        _ssm_kernel,
        out_shape=jax.ShapeDtypeStruct((l, w), BF16),
        grid=(l // rows,),
        in_specs=[pl.BlockSpec((rows, w), lambda c: (c, 0)),
                  full((SSM_NBLK, SSM_BLK_CH, SSM_BLK_ST)), full((SSM_NBLK, SSM_BLK_CH, SSM_BLK_ST)),
                  full((SSM_NBLK, SSM_BLK_ST, SSM_BLK_CH)), full((SSM_NBLK, SSM_BLK_ST, SSM_BLK_CH)),
                  full((tc, SSM_STATES)), full((tc, SSM_STATES)),
                  full((tc, SSM_STATES)), full((tc, SSM_STATES)),
                  full((1, SSM_STATES)), full((1, SSM_STATES)),
                  full((1, w)), full((tc, tc))],
        out_specs=pl.BlockSpec((rows, w), lambda c: (c, 0)),
        scratch_shapes=[pltpu.VMEM((1, SSM_STATES), F32), pltpu.VMEM((1, SSM_STATES), F32)],
        compiler_params=_params(("arbitrary",)),
        name="s5_ssm",
    )(s_in, tabs['bre'], tabs['bim'], tabs['cre'], tabs['cim'], tabs['enr'], tabs['eni'],
      tabs['epr'], tabs['epi'], tabs['lamr'], tabs['lami'], d_skip.reshape(1, w).astype(F32), tri)


def _merge_kernel(attn_ref, y_ref, u_ref, wa_ref, wg1_ref, wg2_ref, wia_ref, wis_ref, o_ref):
    a = jnp.dot(attn_ref[...], wa_ref[...], preferred_element_type=F32)
    y = y_ref[...]
    s = (jnp.dot(y, wg1_ref[...], preferred_element_type=F32)
         * jax.nn.sigmoid(jnp.dot(y, wg2_ref[...], preferred_element_type=F32)))
    u = u_ref[...]
    ga = jax.nn.sigmoid(jnp.dot(u, wia_ref[...], preferred_element_type=F32))
    gs = jax.nn.sigmoid(jnp.dot(u, wis_ref[...], preferred_element_type=F32))
    o_ref[...] = (ga * a + gs * s).astype(o_ref.dtype)


def _merge(attn, y_ssm, u, w_attn_out, w_ssm_glu, w_in):
    l, d = u.shape
    tm, tn = 512, 512
    nd = d // tn
    go = GATE_OFF // tn
    return pl.pallas_call(
        _merge_kernel,
        out_shape=jax.ShapeDtypeStruct((l, d), BF16),
        grid=(nd, l // tm),
        in_specs=[pl.BlockSpec((tm, ATTN_OUT_WIDTH), lambda j, i: (i, 0)),
                  pl.BlockSpec((tm, SSM_WIDTH), lambda j, i: (i, 0)),
                  pl.BlockSpec((tm, d), lambda j, i: (i, 0)),
                  pl.BlockSpec((ATTN_OUT_WIDTH, tn), lambda j, i: (0, j)),
                  pl.BlockSpec((SSM_WIDTH, tn), lambda j, i: (0, j)),
                  pl.BlockSpec((SSM_WIDTH, tn), lambda j, i: (0, j + nd)),
                  pl.BlockSpec((d, tn), lambda j, i: (0, j + go)),
                  pl.BlockSpec((d, tn), lambda j, i: (0, j + go + nd))],
        out_specs=pl.BlockSpec((tm, tn), lambda j, i: (i, j)),
        compiler_params=_params(("arbitrary", "arbitrary")),
        name="branch_merge",
    )(attn, y_ssm, u, w_attn_out, w_ssm_glu, w_ssm_glu, w_in, w_in)


def _layer_norm(h, g, b):
    mu = jnp.mean(h, axis=-1, keepdims=True)
    c = h - mu
    var = jnp.mean(c * c, axis=-1, keepdims=True)
    return c * lax.rsqrt(var + LN_EPS) * g + b


def _pack_rows(v):
    bits = lax.bitcast_convert_type(v.astype(BF16).astype(F32), jnp.uint32)
    half = bits.shape[1] // 2
    return (bits[:, :half] >> 16) | (bits[:, half:] & jnp.uint32(0xFFFF0000))


def _unpack_lo(xp):
    return lax.bitcast_convert_type(xp << 16, F32)


def _unpack_hi(xp):
    return lax.bitcast_convert_type(xp & jnp.uint32(0xFFFF0000), F32)


def _store_tile_rows(ref, packed):
    rows = packed.shape[0]
    for s in range(SUBLANES):
        ref[pl.ds(s, rows, stride=SUBLANES), :] = packed[:, s * LANES:(s + 1) * LANES]


def _load_tile_rows(ref, s):
    return ref[pl.ds(s, ref.shape[0] // SUBLANES, stride=SUBLANES), :]


def _outproj_kernel(mg_ref, x_ref, wo_ref, wr_ref, g1_ref, lg_ref, lb_ref, sc_ref, sh_ref,
                    x1_ref, u2_ref, u2p_ref, sco_ref):
    mix = jnp.dot(mg_ref[...], wo_ref[...], preferred_element_type=F32)
    x1 = _layer_norm(DEEPNORM_ALPHA * x_ref[...] + g1_ref[...] * mix, lg_ref[...], lb_ref[...])
    x1_ref[...] = x1
    u2 = (x1 * (1.0 + sc_ref[...]) + sh_ref[...]).astype(BF16)
    u2_ref[...] = u2
    _store_tile_rows(u2p_ref, _pack_rows(u2))
    logits = lax.dot_general(wr_ref[...], u2, (((1,), (1,)), ((), ())), preferred_element_type=F32)
    sco_ref[...] = jax.nn.sigmoid(logits)


def _outproj(merged, x, w_o, w_router_t, gate1, ln_g, ln_b, scale2, shift2):
    l, d = x.shape
    assert d == 2 * SUBLANES * LANES
    e = w_router_t.shape[0]
    tm = 512
    row = lambda w: pl.BlockSpec((tm, w), lambda i: (i, 0))
    vec = pl.BlockSpec((1, d), lambda i: (0, 0))
    return pl.pallas_call(
        _outproj_kernel,
        out_shape=(jax.ShapeDtypeStruct((l, d), F32), jax.ShapeDtypeStruct((l, d), BF16),
                   jax.ShapeDtypeStruct((l * SUBLANES, LANES), jnp.uint32),
                   jax.ShapeDtypeStruct((e, l), F32)),
        grid=(l // tm,),
        in_specs=[row(d), row(d), pl.BlockSpec((d, d), lambda i: (0, 0)),
                  pl.BlockSpec((e, d), lambda i: (0, 0)), vec, vec, vec, vec, vec],
        out_specs=(row(d), row(d), pl.BlockSpec((tm * SUBLANES, LANES), lambda i: (i, 0)),
                   pl.BlockSpec((e, tm), lambda i: (0, i))),
        compiler_params=_params(("arbitrary",)),
        name="outproj_ln1_router",
    )(merged, x, w_o, w_router_t, gate1, ln_g, ln_b, scale2, shift2)


ROUTE_TILE = 512
EXPERTS_PER_GROUP = N_EXPERTS // N_EXPERT_GROUPS


def _route_kernel(s_ref, b_ref, tri_ref, idx_ref, w_ref, rank_ref, cnt_ref, carry_ref):
    @pl.when(pl.program_id(0) == 0)
    def _():
        carry_ref[...] = jnp.zeros_like(carry_ref)

    ne, tm = s_ref.shape
    neg_inf = -jnp.inf
    s = s_ref[...]
    sel = s + b_ref[...]
    e_iota = lax.broadcasted_iota(jnp.int32, (ne, tm), 0)

    gs_rows = []
    for g in range(N_EXPERT_GROUPS):
        xg = sel[g * EXPERTS_PER_GROUP:(g + 1) * EXPERTS_PER_GROUP]
        m1 = jnp.max(xg, axis=0, keepdims=True)
        n1 = jnp.sum((xg == m1).astype(F32), axis=0, keepdims=True)
        m2 = jnp.max(jnp.where(xg < m1, xg, neg_inf), axis=0, keepdims=True)
        gs_rows.append(m1 + jnp.where(n1 >= 2.0, m1, m2))
    gs = jnp.concatenate(gs_rows, axis=0)

    g_iota = lax.broadcasted_iota(jnp.int32, gs.shape, 0)
    beaten = jnp.zeros(gs.shape, jnp.int32)
    for g2 in range(N_EXPERT_GROUPS):
        row = gs[g2:g2 + 1]
        beats = (row > gs) | ((row == gs) & (g2 < g_iota))
        beaten = beaten + beats.astype(jnp.int32)
    g_ok = beaten < TOPK_GROUPS
    work = jnp.concatenate(
        [jnp.where(g_ok[g:g + 1], sel[g * EXPERTS_PER_GROUP:(g + 1) * EXPERTS_PER_GROUP], neg_inf)
         for g in range(N_EXPERT_GROUPS)], axis=0)

    idxs, vals = [], []
    chosen = jnp.zeros((ne, tm), F32)
    for _ in range(TOP_K):
        m = jnp.max(work, axis=0, keepdims=True)
        i = jnp.min(jnp.where(work == m, e_iota, ne), axis=0, keepdims=True)
        onehot = e_iota == i
        idxs.append(i)
        vals.append(jnp.sum(jnp.where(onehot, s, 0.0), axis=0, keepdims=True))
        chosen = jnp.where(onehot, 1.0, chosen)
        work = jnp.where(onehot, neg_inf, work)
    wsum = vals[0]
    for v in vals[1:]:
        wsum = wsum + v

    before = jnp.dot(chosen.astype(BF16), tri_ref[...], preferred_element_type=F32) + carry_ref[...]
    ranks = [jnp.sum(jnp.where(e_iota == i, before, 0.0), axis=0, keepdims=True) for i in idxs]
    carry_ref[...] = carry_ref[...] + jnp.sum(chosen, axis=1, keepdims=True)

    idx_ref[...] = jnp.concatenate(idxs, axis=0)
    w_ref[...] = jnp.concatenate([v / wsum * ROUTED_SCALE for v in vals], axis=0)
    rank_ref[...] = jnp.concatenate(ranks, axis=0).astype(jnp.int32)
    cnt_ref[...] = carry_ref[...]


def _route(scores_t, router_bias):
    ne, t = scores_t.shape
    tm = ROUTE_TILE
    tri = jnp.triu(jnp.ones((tm, tm), F32), k=1).astype(BF16)
    tok = pl.BlockSpec((TOP_K, tm), lambda i: (0, i))
    return pl.pallas_call(
        _route_kernel,
        out_shape=(jax.ShapeDtypeStruct((TOP_K, t), jnp.int32), jax.ShapeDtypeStruct((TOP_K, t), F32),
                   jax.ShapeDtypeStruct((TOP_K, t), jnp.int32), jax.ShapeDtypeStruct((ne, 1), F32)),
        grid=(t // tm,),
        in_specs=[pl.BlockSpec((ne, tm), lambda i: (0, i)),
                  pl.BlockSpec((ne, 1), lambda i: (0, 0)),
                  pl.BlockSpec((tm, tm), lambda i: (0, 0))],
        out_specs=(tok, tok, tok, pl.BlockSpec((ne, 1), lambda i: (0, 0))),
        scratch_shapes=[pltpu.VMEM((ne, 1), F32)],
        compiler_params=_params(("arbitrary",)),
        name="moe_route",
    )(scores_t, router_bias.astype(F32).reshape(ne, 1), tri)


def _positions_kernel(ps_ref, idx_ref, rank_ref, pos_ref):
    idx = idx_ref[...]

    def add_expert(e, acc):
        return acc + jnp.where(idx == e, ps_ref[e], 0)

    start = lax.fori_loop(0, ps_ref.shape[0], add_expert, jnp.zeros(idx.shape, jnp.int32))
    pos_ref[...] = (start + rank_ref[...]) * SUBLANES


def _positions(idx, rank, pad_start):
    k, t = idx.shape
    tm = 1024
    tok = pl.BlockSpec((k, tm), lambda i: (0, i))
    return pl.pallas_call(
        _positions_kernel,
        out_shape=jax.ShapeDtypeStruct((k, t), jnp.int32),
        grid=(t // tm,),
        in_specs=[pl.BlockSpec(memory_space=pltpu.SMEM), tok, tok],
        out_specs=tok,
        compiler_params=_params(("arbitrary",)),
        name="moe_positions",
    )(pad_start, idx, rank)


DISPATCH_TILE = 512


def _dispatch_kernel(pos_ref, ps_ref, pe_ref, u_ref, xs_ref, zero_ref, sem, zsem, tsem):
    tm = u_ref.shape[0] // SUBLANES
    ne = ps_ref.shape[0]
    block_words = MOE_BLOCK * SUBLANES

    def tile_row(r):
        return pl.ds(pl.multiple_of(r * SUBLANES, SUBLANES), SUBLANES)

    first_unused = pe_ref[ne - 1] // MOE_BLOCK
    n_blocks = xs_ref.shape[0] // block_words

    def unused_copy(b):
        start = pl.multiple_of(b * block_words, block_words)
        return pltpu.make_async_copy(zero_ref, xs_ref.at[pl.ds(start, block_words)], tsem)

    @pl.when(pl.program_id(0) == 0)
    def _():
        zero_ref[...] = jnp.zeros_like(zero_ref)

        def block_copy(start):
            start = pl.multiple_of(start * SUBLANES, block_words)
            return pltpu.make_async_copy(zero_ref, xs_ref.at[pl.ds(start, block_words)], zsem)

        def fill(e, carry):
            @pl.when(pe_ref[e] > ps_ref[e])
            def _():
                block_copy(pe_ref[e] - MOE_BLOCK).start()
            return carry

        def fill_done(e, carry):
            @pl.when(pe_ref[e] > ps_ref[e])
            def _():
                block_copy(pe_ref[e] - MOE_BLOCK).wait()
            return carry

        lax.fori_loop(0, ne, fill, 0)
        lax.fori_loop(0, ne, fill_done, 0)

        def fill_unused(b, carry):
            unused_copy(b).start()
            return carry

        lax.fori_loop(first_unused, n_blocks, fill_unused, 0)

    def issue(pair, carry):
        for t in (2 * pair, 2 * pair + 1):
            for k in range(TOP_K):
                dst = pl.ds(pl.multiple_of(pos_ref[k, t], SUBLANES), SUBLANES)
                pltpu.make_async_copy(u_ref.at[tile_row(t)], xs_ref.at[dst], sem).start(priority=k % 2)
        return carry

    lax.fori_loop(0, tm // 2, issue, 0)
    for _ in range(TOP_K):
        pltpu.make_async_copy(u_ref, xs_ref.at[pl.ds(0, tm * SUBLANES)], sem).wait()

    @pl.when(pl.program_id(0) == pl.num_programs(0) - 1)
    def _():
        def fill_unused_done(b, carry):
            unused_copy(b).wait()
            return carry

        lax.fori_loop(first_unused, n_blocks, fill_unused_done, 0)


def _dispatch(pos, pad_start, pad_end, u2p, n_rows):
    t = u2p.shape[0] // SUBLANES
    tm = DISPATCH_TILE
    smem_tok = pl.BlockSpec((TOP_K, tm), lambda i: (0, i), memory_space=pltpu.SMEM)
    smem_all = pl.BlockSpec(memory_space=pltpu.SMEM)
    return pl.pallas_call(
        _dispatch_kernel,
        out_shape=jax.ShapeDtypeStruct((n_rows * SUBLANES, LANES), jnp.uint32),
        grid=(t // tm,),
        in_specs=[smem_tok, smem_all, smem_all,
                  pl.BlockSpec((tm * SUBLANES, LANES), lambda i: (i, 0))],
        out_specs=pl.BlockSpec(memory_space=pl.ANY),
        scratch_shapes=[pltpu.VMEM((MOE_BLOCK * SUBLANES, LANES), jnp.uint32),
                        pltpu.SemaphoreType.DMA, pltpu.SemaphoreType.DMA, pltpu.SemaphoreType.DMA],
        compiler_params=_params(("arbitrary",)),
        name="moe_dispatch",
    )(pos, pad_start, pad_end, u2p)


WEIGHT_CHUNK_BYTES = 2 * 1024 * 1024


def _expert_kernel(be_ref, bv_ref, nx_ref, nx2_ref, sl_ref, x_ref, win_ref, wout_ref, y_ref,
                   wfi_ref, wfo_ref, wbi_ref, wbo_ref, sem):
    b = pl.program_id(0)
    valid = bv_ref[b] > 0
    e = be_ref[b]
    slot = sl_ref[b]
    new_expert = (b == 0) | (e != be_ref[jnp.maximum(b - 1, 0)])

    def weight_chunks():
        chunks = []
        for hbm, land, wb, j in ((win_ref, wfi_ref, wbi_ref, 0), (wout_ref, wfo_ref, wbo_ref, 1)):
            rows = WEIGHT_CHUNK_BYTES // (hbm.shape[2] * 4)
            for c in range(hbm.shape[1] // rows):
                chunks.append((hbm, land, wb, j, pl.ds(c * rows, rows)))
        return chunks

    def chunk_copy(chunk, ex, s):
        hbm, land, _, j, rs = chunk
        return pltpu.make_async_copy(hbm.at[ex, rs], land.at[s, rs], sem.at[s, j])

    def start_weights(ex, s):
        for n, chunk in enumerate(weight_chunks()):
            chunk_copy(chunk, ex, s).start(priority=n % 2)

    @pl.when(b == 0)
    def _():
        start_weights(e, slot)

        @pl.when(nx_ref[b] >= 0)
        def _():
            start_weights(nx_ref[b], 1 - slot)

    @pl.when(valid & new_expert)
    def _():
        for chunk in weight_chunks():
            chunk_copy(chunk, e, slot).wait()
        for n, chunk in enumerate(weight_chunks()):
            _, land, wb, _, rs = chunk
            for s in range(2):
                @pl.when(slot == s)
                def _():
                    wb[rs] = land[s, rs].astype(BF16)

            @pl.when(nx2_ref[b] >= 0)
            def _():
                chunk_copy(chunk, nx2_ref[b], slot).start(priority=n % 2)

    @pl.when(valid)
    def _():
        words = [_load_tile_rows(x_ref, s) for s in range(SUBLANES)]
        lo = jnp.concatenate([_unpack_lo(wd).astype(BF16) for wd in words], axis=1)
        hi = jnp.concatenate([_unpack_hi(wd).astype(BF16) for wd in words], axis=1)
        half = lo.shape[1]
        h = (jnp.dot(lo, wbi_ref[:half], preferred_element_type=F32)
             + jnp.dot(hi, wbi_ref[half:], preferred_element_type=F32))
        hg, hu = h[:, :EXPERT_FF], h[:, EXPERT_FF:]
        act = (hg * jax.nn.sigmoid(hg) * hu).astype(BF16)
        _store_tile_rows(y_ref, _pack_rows(jnp.dot(act, wbo_ref[...], preferred_element_type=F32)))

    @pl.when(jnp.logical_not(valid))
    def _():
        y_ref[...] = jnp.zeros_like(y_ref)


def _experts(block_expert, block_valid, block_next, block_next2, block_slot, x_rows, e_w_in, e_w_out):
    n_rows = x_rows.shape[0] // SUBLANES
    nb = n_rows // MOE_BLOCK
    _, d, ff2 = e_w_in.shape
    ff = ff2 // 2
    rows_spec = pl.BlockSpec((MOE_BLOCK * SUBLANES, LANES), lambda b, *_: (b, 0))
    grid_spec = pltpu.PrefetchScalarGridSpec(
        num_scalar_prefetch=5,
        grid=(nb,),
        in_specs=[rows_spec,
                  pl.BlockSpec(memory_space=pl.ANY),
                  pl.BlockSpec(memory_space=pl.ANY)],
        out_specs=rows_spec,
        scratch_shapes=[pltpu.VMEM((2, d, ff2), F32), pltpu.VMEM((2, ff, d), F32),
                        pltpu.VMEM((d, ff2), BF16), pltpu.VMEM((ff, d), BF16),
                        pltpu.SemaphoreType.DMA((2, 2))],
    )
    return pl.pallas_call(
        _expert_kernel,
        out_shape=jax.ShapeDtypeStruct((n_rows * SUBLANES, LANES), jnp.uint32),
        grid_spec=grid_spec,
        compiler_params=_params(("arbitrary",)),
        name="routed_experts",
    )(block_expert, block_valid, block_next, block_next2, block_slot, x_rows, e_w_in, e_w_out)


COMBINE_TILE = 256


def _final_kernel(pos_ref, posn_ref, u2_ref, x1_ref, w_ref, ys_ref,
                  win_ref, wout_ref, g2_ref, lg_ref, lb_ref, o_ref, buf0_ref, buf1_ref, sem):
    tm = u2_ref.shape[0]
    i = pl.program_id(0)
    bufs = (buf0_ref, buf1_ref)

    def tile_row(r):
        return pl.ds(pl.multiple_of(r * SUBLANES, SUBLANES), SUBLANES)

    def gather(slots_ref, par):
        def issue(pair, carry):
            for t in (2 * pair, 2 * pair + 1):
                for k in range(TOP_K):
                    src = pl.ds(pl.multiple_of(slots_ref[k, t], SUBLANES), SUBLANES)
                    pltpu.make_async_copy(ys_ref.at[src], bufs[par].at[k, tile_row(t)],
                                          sem.at[par]).start(priority=k % 2)
            return carry

        lax.fori_loop(0, tm // 2, issue, 0)

    @pl.when(i == 0)
    def _():
        gather(pos_ref, 0)

    for par in range(2):
        @pl.when((i + 1 < pl.num_programs(0)) & (i % 2 == par))
        def _():
            gather(posn_ref, 1 - par)

    h = jnp.dot(u2_ref[...], win_ref[...], preferred_element_type=F32)
    hg, hu = h[:, :SHARED_FF], h[:, SHARED_FF:]
    act = (hg * jax.nn.sigmoid(hg) * hu).astype(BF16)
    shared = jnp.dot(act, wout_ref[...], preferred_element_type=F32)

    w = w_ref[...]
    wk = [w[:, k:k + 1] for k in range(TOP_K)]

    for par in range(2):
        @pl.when(i % 2 == par)
        def _():
            buf = bufs[par]
            for k in range(TOP_K):
                pltpu.make_async_copy(ys_ref.at[pl.ds(0, tm * SUBLANES)], buf.at[k], sem.at[par]).wait()
            lo_parts, hi_parts = [], []
            for s in range(SUBLANES):
                lo = jnp.zeros((tm, LANES), F32)
                hi = jnp.zeros((tm, LANES), F32)
                for k in range(TOP_K):
                    words = _load_tile_rows(buf.at[k], s)
                    lo = lo + wk[k] * _unpack_lo(words)
                    hi = hi + wk[k] * _unpack_hi(words)
                lo_parts.append(lo)
                hi_parts.append(hi)
            ffn = shared + jnp.concatenate(lo_parts + hi_parts, axis=1)
            o_ref[...] = _layer_norm(DEEPNORM_ALPHA * x1_ref[...] + g2_ref[...] * ffn,
                                     lg_ref[...], lb_ref[...])


def _final(pos, u2, x1, w_tok, y_rows, s_w_in, s_w_out, gate2, ln_g, ln_b):
    l, d = x1.shape
    tm = COMBINE_TILE
    row = pl.BlockSpec((tm, d), lambda i: (i, 0))
    vec = pl.BlockSpec((1, d), lambda i: (0, 0))
    n_tiles = l // tm
    smem_tok = pl.BlockSpec((TOP_K, tm), lambda i: (0, i), memory_space=pltpu.SMEM)
    smem_next = pl.BlockSpec((TOP_K, tm), lambda i: (0, jnp.minimum(i + 1, n_tiles - 1)),
                             memory_space=pltpu.SMEM)
    row_buf = pltpu.VMEM((TOP_K, tm * SUBLANES, LANES), jnp.uint32)
    return pl.pallas_call(
        _final_kernel,
        out_shape=jax.ShapeDtypeStruct((l, d), F32),
        grid=(n_tiles,),
        in_specs=[smem_tok, smem_next, row, row, pl.BlockSpec((tm, TOP_K), lambda i: (i, 0)),
                  pl.BlockSpec(memory_space=pl.ANY),
                  pl.BlockSpec(s_w_in.shape, lambda i: (0, 0)),
                  pl.BlockSpec(s_w_out.shape, lambda i: (0, 0)), vec, vec, vec],
        out_specs=row,
        scratch_shapes=[row_buf, row_buf, pltpu.SemaphoreType.DMA((2,))],
        compiler_params=_params(("arbitrary",)),
        name="combine_shared_ln2",
    )(pos, pos, u2, x1, w_tok, y_rows, s_w_in, s_w_out, gate2, ln_g, ln_b)


def _block_layout(counts, n_tokens):
    padded = (counts + MOE_BLOCK - 1) // MOE_BLOCK * MOE_BLOCK
    pad_ends = jnp.cumsum(padded)
    pad_starts = (pad_ends - padded).astype(jnp.int32)
    n_rows = -(-(n_tokens * TOP_K + N_EXPERTS * (MOE_BLOCK - 1)) // MOE_BLOCK) * MOE_BLOCK
    block_start = jnp.arange(n_rows // MOE_BLOCK, dtype=jnp.int32) * MOE_BLOCK
    block_expert = jnp.minimum(jnp.sum((block_start[:, None] >= pad_ends[None, :]).astype(jnp.int32), axis=1),
                               N_EXPERTS - 1).astype(jnp.int32)
    block_valid = (block_start < pad_ends[-1]).astype(jnp.int32)
    ar = jnp.arange(N_EXPERTS, dtype=jnp.int32)
    has = counts > 0
    later = (ar[None, :] > ar[:, None]) & has[None, :]
    next_has = jnp.min(jnp.where(later, ar[None, :], N_EXPERTS), axis=1)
    next_has = jnp.where(next_has >= N_EXPERTS, -1, next_has)
    after = (ar[None, :] == next_has[:, None])
    next2_has = jnp.sum(jnp.where(after, next_has[None, :], 0), axis=1)
    next2_has = jnp.where(next_has < 0, -1, next2_has)
    ordinal = jnp.cumsum(has.astype(jnp.int32)) - 1
    mine = block_expert[:, None] == ar[None, :]
    pick = lambda v: jnp.sum(jnp.where(mine, v[None, :], 0), axis=1).astype(jnp.int32)
    return (pad_starts, pad_ends.astype(jnp.int32), block_expert, block_valid, pick(next_has),
            pick(next2_has), pick(ordinal) % 2, n_rows)


def kernel(x, c, w_ada, b_ada, w_in, rel_bias, ssm_a_re, ssm_a_im, ssm_log_dt, ssm_b_re, ssm_b_im, ssm_c_re, ssm_c_im, ssm_d, w_attn_out, w_ssm_glu, w_o, ln1_g, ln1_b, w_router, router_bias, e_w_in, e_w_out, s_w_in, s_w_out, ln2_g, ln2_b):
    bsz, l, d = x.shape
    assert bsz == 1
    xf = x.reshape(l, d)
    i = 0
    mod = _modulation(c, w_ada[i], b_ada[i])
    shift1, scale1, gate1, shift2, scale2, gate2 = [mod[:, k * d:(k + 1) * d] for k in range(6)]

    w_in_b = w_in[i].astype(BF16)
    u = _modulate(xf, scale1, shift1)
    s_in = _matmul(u, w_in_b, QKV_WIDTH, SSM_WIDTH, 512, 512, F32, "in_proj_ssm")

    outs, lses = [], []
    for gi, (window, dilation) in enumerate(DILATED_GROUPS):
        bias_prev, bias_cur = _attn_bias_tables(rel_bias, gi, window, dilation)
        qkv = _inproj_qkv_group(u, w_in_b, gi, dilation)
        o, s = _attention_group(qkv, bias_prev, bias_cur, gi)
        outs.append(o)
        lses.append(s)
    attn = _attn_combine(outs, lses)

    tabs = _ssm_tables(ssm_a_re[i], ssm_a_im[i], ssm_log_dt[i], ssm_b_re[i], ssm_b_im[i],
                       ssm_c_re[i], ssm_c_im[i])
    y_ssm = _ssm(s_in, tabs, ssm_d[i])

    merged = _merge(attn, y_ssm, u, w_attn_out[i].astype(BF16), w_ssm_glu[i].astype(BF16), w_in_b)
    x1, u2, u2p, scores_t = _outproj(merged, xf, w_o[i].astype(BF16), w_router[i].T.astype(BF16), gate1,
                                     ln1_g[i].reshape(1, d), ln1_b[i].reshape(1, d), scale2, shift2)

    idx, w, rank, counts = _route(scores_t, router_bias[i])
    (pad_start, pad_end, block_expert, block_valid, block_next, block_next2, block_slot,
     n_rows) = _block_layout(counts[:, 0].astype(jnp.int32), l)
    pos = _positions(idx, rank, pad_start)
    x_rows = _dispatch(pos, pad_start, pad_end, u2p, n_rows)
    y_rows = _experts(block_expert, block_valid, block_next, block_next2, block_slot, x_rows,
                      e_w_in[i], e_w_out[i])
    out = _final(pos, u2, x1, w.T, y_rows, s_w_in[i].astype(BF16),
                 s_w_out[i].astype(BF16), gate2, ln2_g[i].reshape(1, d), ln2_b[i].reshape(1, d))
    return out.reshape(bsz, l, d)
```

```python
import functools
import math

import jax
import jax.numpy as jnp
from jax import lax
from jax.experimental import pallas as pl
from jax.experimental.pallas import tpu as pltpu

F32 = jnp.float32
BF16 = jnp.bfloat16

D_MODEL = 2048
HEAD_DIM = 128
HEADS_PER_GROUP = 4
DILATED_GROUPS = ((128, 1), (512, 4), (2048, 16))
N_ATTN_GROUPS = len(DILATED_GROUPS)
N_ATTN_HEADS = N_ATTN_GROUPS * HEADS_PER_GROUP
ATTN_WIDTH = N_ATTN_HEADS * HEAD_DIM
ATTN_OUT_WIDTH = HEADS_PER_GROUP * HEAD_DIM
ATTN_BLOCK = 128
REL_BUCKETS = 32
REL_MAX_DISTANCE = 2048
SSM_GROUP_CH = 16
SSM_STATE = 64
SSM_WIDTH = 1024
SSM_GROUPS = SSM_WIDTH // SSM_GROUP_CH
A_RE_MAX = -1e-4
QKV_WIDTH = 3 * ATTN_WIDTH
GATE_OFF = QKV_WIDTH + SSM_WIDTH
N_EXPERTS = 256
TOP_K = 8
N_EXPERT_GROUPS = 8
TOPK_GROUPS = 4
EXPERT_FF = 512
SHARED_FF = 512
ROUTED_SCALE = 2.5
MOE_BLOCK = 128
DEPTH = 1
DEEPNORM_ALPHA = (2 * DEPTH) ** 0.25
LN_EPS = 1e-5
NEG_BIG = -1e30
LANES = 128
SUBLANES = 8

SSM_CHUNK = 128
SSM_CHUNKS_PER_STEP = 4
SSM_BLK_GROUPS = 16
SSM_NBLK = SSM_GROUPS // SSM_BLK_GROUPS
SSM_BLK_CH = SSM_BLK_GROUPS * SSM_GROUP_CH
SSM_BLK_ST = SSM_BLK_GROUPS * SSM_STATE
SSM_STATES = SSM_GROUPS * SSM_STATE

VMEM_LIMIT = 56 * 1024 * 1024


def _params(sem, vmem=VMEM_LIMIT):
    return pltpu.CompilerParams(dimension_semantics=sem, vmem_limit_bytes=vmem)


def _mod_kernel(c_ref, w_ref, b_ref, o_ref):
    c = c_ref[...]
    cond = c * jax.nn.sigmoid(c)
    o_ref[...] = jnp.sum(cond * w_ref[...], axis=0, keepdims=True) + b_ref[...]


def _modulation(c, w_ada, b_ada):
    d, n = w_ada.shape
    assert c.shape == (1, d)
    tn = 1024
    return pl.pallas_call(
        _mod_kernel,
        out_shape=jax.ShapeDtypeStruct((1, n), F32),
        grid=(n // tn,),
        in_specs=[pl.BlockSpec((d, 1), lambda j: (0, 0)),
                  pl.BlockSpec((d, tn), lambda j: (0, j)),
                  pl.BlockSpec((1, tn), lambda j: (0, j))],
        out_specs=pl.BlockSpec((1, tn), lambda j: (0, j)),
        compiler_params=_params(("arbitrary",)),
        name="adaln_mod",
    )(c.reshape(d, 1), w_ada, b_ada.reshape(1, n))


def _modulate_kernel(x_ref, sc_ref, sh_ref, o_ref):
    o_ref[...] = (x_ref[...] * (1.0 + sc_ref[...]) + sh_ref[...]).astype(o_ref.dtype)


def _modulate(x, scale, shift):
    m, d = x.shape
    tm = 512
    return pl.pallas_call(
        _modulate_kernel,
        out_shape=jax.ShapeDtypeStruct((m, d), BF16),
        grid=(m // tm,),
        in_specs=[pl.BlockSpec((tm, d), lambda i: (i, 0)),
                  pl.BlockSpec((1, d), lambda i: (0, 0)),
                  pl.BlockSpec((1, d), lambda i: (0, 0))],
        out_specs=pl.BlockSpec((tm, d), lambda i: (i, 0)),
        compiler_params=_params(("arbitrary",)),
        name="modulate",
    )(x, scale, shift)


def _modulate_inproj_kernel(x_ref, sc_ref, sh_ref, w1_ref, w2_ref, u_ref, s_ref):
    u = (x_ref[...] * (1.0 + sc_ref[...]) + sh_ref[...]).astype(BF16)
    u_ref[...] = u
    tn = w1_ref.shape[1]
    s_ref[:, :tn] = jnp.dot(u, w1_ref[...], preferred_element_type=F32)
    s_ref[:, tn:] = jnp.dot(u, w2_ref[...], preferred_element_type=F32)


def _modulate_inproj_ssm(x, scale, shift, w_in):
    m, d = x.shape
    tm, tn = 512, SSM_WIDTH // 2
    cb = QKV_WIDTH // tn
    vec = pl.BlockSpec((1, d), lambda i: (0, 0))
    return pl.pallas_call(
        _modulate_inproj_kernel,
        out_shape=(jax.ShapeDtypeStruct((m, d), BF16), jax.ShapeDtypeStruct((m, SSM_WIDTH), F32)),
        grid=(m // tm,),
        in_specs=[pl.BlockSpec((tm, d), lambda i: (i, 0)), vec, vec,
                  pl.BlockSpec((d, tn), lambda i: (0, cb)),
                  pl.BlockSpec((d, tn), lambda i: (0, cb + 1))],
        out_specs=(pl.BlockSpec((tm, d), lambda i: (i, 0)),
                   pl.BlockSpec((tm, SSM_WIDTH), lambda i: (i, 0))),
        compiler_params=_params(("arbitrary",)),
        name="modulate_in_proj_ssm",
    )(x, scale, shift, w_in, w_in)


def _mm_kernel(a_ref, w_ref, o_ref):
    o_ref[...] = jnp.dot(a_ref[...], w_ref[...], preferred_element_type=F32).astype(o_ref.dtype)


def _matmul(a, w, col_off, n, tm, tn, out_dtype, name):
    m, k = a.shape
    assert col_off % tn == 0 and n % tn == 0 and m % tm == 0
    off_blocks = col_off // tn
    return pl.pallas_call(
        _mm_kernel,
        out_shape=jax.ShapeDtypeStruct((m, n), out_dtype),
        grid=(n // tn, m // tm),
        in_specs=[pl.BlockSpec((tm, k), lambda j, i: (i, 0)),
                  pl.BlockSpec((k, tn), lambda j, i: (0, j + off_blocks))],
        out_specs=pl.BlockSpec((tm, tn), lambda j, i: (i, j)),
        compiler_params=_params(("arbitrary", "arbitrary")),
        name=name,
    )(a, w)


ATTN_QBLOCKS = 4


def _attn_kernel(bp_ref, bc_ref, q_ref, kp_ref, kc_ref, vp_ref, vc_ref, o_ref, lse_ref):
    scale = HEAD_DIM ** -0.5
    nt = (((1,), (1,)), ((), ()))
    first = pl.program_id(1) == 0
    work = []
    for h in range(HEADS_PER_GROUP):
        hs = slice(h * HEAD_DIM, (h + 1) * HEAD_DIM)
        for qb in range(ATTN_QBLOCKS):
            rows = slice(qb * ATTN_BLOCK, (qb + 1) * ATTN_BLOCK)
            before = slice((qb - 1) * ATTN_BLOCK, qb * ATTN_BLOCK)
            work.append((h, hs, qb, rows, before))

    scores = []
    for h, hs, qb, rows, before in work:
        k_prev = kp_ref[:, hs] if qb == 0 else kc_ref[before, hs]
        q = q_ref[rows, hs]
        s_p = lax.dot_general(q, k_prev, nt, preferred_element_type=F32) * scale + bp_ref[h]
        s_c = lax.dot_general(q, kc_ref[rows, hs], nt, preferred_element_type=F32) * scale + bc_ref[h]
        if qb == 0:
            s_p = jnp.where(first, NEG_BIG, s_p)
        scores.append((s_p, s_c))
    maxes = [jnp.maximum(jnp.max(s_p, axis=-1, keepdims=True), jnp.max(s_c, axis=-1, keepdims=True))
             for s_p, s_c in scores]
    probs = [(jnp.exp(s_p - m), jnp.exp(s_c - m)) for (s_p, s_c), m in zip(scores, maxes)]
    sums = [jnp.sum(p_p, axis=-1, keepdims=True) + jnp.sum(p_c, axis=-1, keepdims=True)
            for p_p, p_c in probs]
    outs = []
    for (h, hs, qb, rows, before), (p_p, p_c) in zip(work, probs):
        v_prev = vp_ref[:, hs] if qb == 0 else vc_ref[before, hs]
        outs.append(jnp.dot(p_p.astype(BF16), v_prev, preferred_element_type=F32)
                    + jnp.dot(p_c.astype(BF16), vc_ref[rows, hs], preferred_element_type=F32))
    for (h, hs, qb, rows, before), o, m, l in zip(work, outs, maxes, sums):
        o_ref[rows, hs] = o / l
        lse_ref[rows, hs] = jnp.broadcast_to(m + jnp.log(l), (ATTN_BLOCK, HEAD_DIM))


def _inproj_dilated_kernel(a_ref, wq_ref, wk_ref, wv_ref, o_ref, acc_ref):
    dilation, rows, _ = o_ref.shape
    a = a_ref[...]
    for j, w_ref in enumerate((wq_ref, wk_ref, wv_ref)):
        tn = w_ref.shape[1]
        res = jnp.dot(a, w_ref[...], preferred_element_type=F32)
        if dilation == 1:
            o_ref[0, :, j * tn:(j + 1) * tn] = res.astype(o_ref.dtype)
            continue
        for c in range(tn // LANES):
            acc_ref[c] = res[:, c * LANES:(c + 1) * LANES]
        for r in range(dilation):
            for c in range(tn // LANES):
                o_ref[r, :, j * tn + c * LANES:j * tn + (c + 1) * LANES] = (
                    acc_ref[c, pl.ds(r, rows, stride=dilation), :].astype(o_ref.dtype))


def _inproj_qkv_group(u, w_in, gi, dilation):
    l, k = u.shape
    tm, tn = 512, ATTN_OUT_WIDTH
    w_spec = lambda j: pl.BlockSpec((k, tn), lambda i: (0, j * N_ATTN_GROUPS + gi))
    return pl.pallas_call(
        _inproj_dilated_kernel,
        out_shape=jax.ShapeDtypeStruct((dilation, l // dilation, 3 * tn), BF16),
        grid=(l // tm,),
        in_specs=[pl.BlockSpec((tm, k), lambda i: (i, 0)), w_spec(0), w_spec(1), w_spec(2)],
        out_specs=pl.BlockSpec((dilation, tm // dilation, 3 * tn), lambda i: (0, i, 0)),
        scratch_shapes=[pltpu.VMEM((tn // LANES, tm, LANES), F32)],
        compiler_params=_params(("arbitrary",)),
        name=f"in_proj_qkv_g{gi}",
    )(u, w_in, w_in, w_in)


def _attention_group(qkv, bias_prev, bias_cur, gi):
    dilation, m, _ = qkv.shape
    tile = ATTN_QBLOCKS * ATTN_BLOCK

    def cur(cb):
        return pl.BlockSpec((None, tile, ATTN_OUT_WIDTH), lambda r, b: (r, b, cb))

    def prev(cb):
        return pl.BlockSpec((None, ATTN_BLOCK, ATTN_OUT_WIDTH),
                            lambda r, b: (r, jnp.maximum(b * ATTN_QBLOCKS - 1, 0), cb))

    bias_spec = pl.BlockSpec((HEADS_PER_GROUP, ATTN_BLOCK, ATTN_BLOCK), lambda r, b: (0, 0, 0))
    out_spec = pl.BlockSpec((None, tile, ATTN_OUT_WIDTH), lambda r, b: (r, b, 0))
    return pl.pallas_call(
        _attn_kernel,
        out_shape=(jax.ShapeDtypeStruct((dilation, m, ATTN_OUT_WIDTH), F32),
                   jax.ShapeDtypeStruct((dilation, m, ATTN_OUT_WIDTH), F32)),
        grid=(dilation, m // tile),
        in_specs=[bias_spec, bias_spec, cur(0), prev(1), cur(1), prev(2), cur(2)],
        out_specs=(out_spec, out_spec),
        compiler_params=_params(("arbitrary", "arbitrary")),
        name=f"dilated_attn_g{gi}",
    )(bias_prev, bias_cur, qkv, qkv, qkv, qkv, qkv)


def _t5_bucket(dist):
    max_exact = REL_BUCKETS // 2
    d_f = jnp.maximum(dist, 1).astype(F32)
    large = max_exact + (jnp.log(d_f / max_exact) / math.log(REL_MAX_DISTANCE / max_exact)
                         * (REL_BUCKETS - max_exact)).astype(jnp.int32)
    large = jnp.minimum(large, REL_BUCKETS - 1)
    return jnp.where(dist < max_exact, dist, large)


def _attn_bias_tables(rel_bias, gi, window, dilation):
    n_keys = window // dilation
    bucket = _t5_bucket(jnp.arange(n_keys + 1, dtype=jnp.int32) * dilation)
    hs = slice(gi * HEADS_PER_GROUP, (gi + 1) * HEADS_PER_GROUP)
    hi = lax.Precision.HIGHEST
    pick_bucket = (bucket[:, None] == jnp.arange(REL_BUCKETS)[None, :]).astype(F32)
    by_dist = jnp.dot(pick_bucket, rel_bias[:, hs].astype(F32), precision=hi)
    qi = jnp.arange(ATTN_BLOCK)[:, None]
    kj = jnp.arange(ATTN_BLOCK)[None, :]
    d_prev = qi + ATTN_BLOCK - kj
    d_cur = qi - kj
    tabs = []
    for dist in (d_prev, d_cur):
        ok = (dist >= 0) & (dist <= n_keys)
        pick_dist = (dist[:, :, None] == jnp.arange(n_keys + 1)[None, None, :]).astype(F32)
        vals = jnp.einsum('ijd,dh->hij', pick_dist, by_dist, precision=hi)
        tabs.append(jnp.where(ok[None], vals, NEG_BIG))
    return tabs


ATTN_COMBINE_TILE = 512


def _attn_combine_kernel(o1, o2, o3, l1, l2, l3, out_ref, *scratch):
    def token_order(ref, buf, c):
        dilation, rows, _ = ref.shape
        cs = slice(c * LANES, (c + 1) * LANES)
        if dilation == 1:
            return ref[0, :, cs]
        for r in range(dilation):
            buf[c, pl.ds(r, rows, stride=dilation), :] = ref[r, :, cs]
        return buf[c]

    for c in range(out_ref.shape[1] // LANES):
        v1, v2, v3, a1, a2, a3 = [token_order(ref, buf, c)
                                  for ref, buf in zip((o1, o2, o3, l1, l2, l3), scratch)]
        m = jnp.maximum(jnp.maximum(a1, a2), a3)
        e1, e2, e3 = jnp.exp(a1 - m), jnp.exp(a2 - m), jnp.exp(a3 - m)
        num = e1 * v1 + e2 * v2 + e3 * v3
        out_ref[:, c * LANES:(c + 1) * LANES] = (num / (e1 + e2 + e3)).astype(out_ref.dtype)


def _attn_combine(outs, lses):
    w = outs[0].shape[-1]
    l = outs[0].shape[0] * outs[0].shape[1]
    tm = ATTN_COMBINE_TILE

    def spec(a):
        dilation = a.shape[0]
        return pl.BlockSpec((dilation, tm // dilation, w), lambda i: (0, i, 0))

    return pl.pallas_call(
        _attn_combine_kernel,
        out_shape=jax.ShapeDtypeStruct((l, w), BF16),
        grid=(l // tm,),
        in_specs=[spec(a) for a in (*outs, *lses)],
        out_specs=pl.BlockSpec((tm, w), lambda i: (i, 0)),
        scratch_shapes=[pltpu.VMEM((w // LANES, tm, LANES), F32)] * 6,
        compiler_params=_params(("arbitrary",)),
        name="attn_combine",
    )(*outs, *lses)


def _ssm_kernel(u_ref, bre_ref, bim_ref, cre_ref, cim_ref, enr_ref, eni_ref, epr_ref, epi_ref,
                lamr_ref, lami_ref, d_ref, tri_ref, y_ref, car_ref, cai_ref):
    @pl.when(pl.program_id(0) == 0)
    def _():
        car_ref[...] = jnp.zeros_like(car_ref)
        cai_ref[...] = jnp.zeros_like(cai_ref)

    tc = tri_ref.shape[0]
    tri = tri_ref[...]
    for j in range(SSM_NBLK):
        ch = slice(j * SSM_BLK_CH, (j + 1) * SSM_BLK_CH)
        st = slice(j * SSM_BLK_ST, (j + 1) * SSM_BLK_ST)
        u = u_ref[:, ch]
        ub = u.astype(BF16)
        bur = jnp.dot(ub, bre_ref[j], preferred_element_type=F32)
        bui = jnp.dot(ub, bim_ref[j], preferred_element_type=F32)
        enr, eni = enr_ref[:, st], eni_ref[:, st]
        epr, epi = epr_ref[:, st], epi_ref[:, st]
        lr, li = lamr_ref[:, st], lami_ref[:, st]
        cr, ci = car_ref[:, st], cai_ref[:, st]
        sr_chunks, si_chunks = [], []
        for q in range(u_ref.shape[0] // tc):
            rows = slice(q * tc, (q + 1) * tc)
            xr = bur[rows] * enr - bui[rows] * eni
            xi = bur[rows] * eni + bui[rows] * enr
            pr = jnp.dot(tri, xr.astype(BF16), preferred_element_type=F32)
            pi = jnp.dot(tri, xi.astype(BF16), preferred_element_type=F32)
            tr = pr + (lr * cr - li * ci)
            ti = pi + (lr * ci + li * cr)
            sr = epr * tr - epi * ti
            si = epr * ti + epi * tr
            cr, ci = sr[tc - 1:tc], si[tc - 1:tc]
            sr_chunks.append(sr.astype(BF16))
            si_chunks.append(si.astype(BF16))
        car_ref[:, st] = cr
        cai_ref[:, st] = ci
        y = (jnp.dot(jnp.concatenate(sr_chunks, axis=0), cre_ref[j], preferred_element_type=F32)
             - jnp.dot(jnp.concatenate(si_chunks, axis=0), cim_ref[j], preferred_element_type=F32))
        y_ref[:, ch] = (y + d_ref[:, ch] * u).astype(y_ref.dtype)


def _ssm_tables(a_re, a_im, log_dt, b_re, b_im, c_re, c_im):
    g, p, hc = b_re.shape
    lam_re = jnp.minimum(a_re.astype(F32), A_RE_MAX)
    lam_im = a_im.astype(F32)
    dt = jnp.exp(log_dt.astype(F32))[:, None]
    mag = jnp.exp(lam_re * dt)
    lb_re, lb_im = mag * jnp.cos(lam_im * dt), mag * jnp.sin(lam_im * dt)
    imag = jnp.exp(-lam_re * dt)
    li_re, li_im = imag * jnp.cos(lam_im * dt), -imag * jnp.sin(lam_im * dt)
    den = lam_re * lam_re + lam_im * lam_im
    nr, ni = lb_re - 1.0, lb_im
    f_re = (nr * lam_re + ni * lam_im) / den
    f_im = (ni * lam_re - nr * lam_im) / den
    bb_re = f_re[:, :, None] * b_re - f_im[:, :, None] * b_im
    bb_im = f_re[:, :, None] * b_im + f_im[:, :, None] * b_re

    def powers(pr, pi):
        er, ei = jnp.ones((1, g * p), F32), jnp.zeros((1, g * p), F32)
        pr, pi = pr.reshape(1, g * p), pi.reshape(1, g * p)
        while er.shape[0] < SSM_CHUNK:
            er, ei = (jnp.concatenate([er, er * pr - ei * pi], axis=0),
                      jnp.concatenate([ei, er * pi + ei * pr], axis=0))
            pr, pi = pr * pr - pi * pi, 2.0 * pr * pi
        return er, ei

    epr, epi = powers(lb_re, lb_im)
    enr, eni = powers(li_re, li_im)
    eye = jnp.eye(SSM_BLK_GROUPS, dtype=F32)

    def b_blocks(b):
        b = b.reshape(SSM_NBLK, SSM_BLK_GROUPS, p, hc)
        return jnp.einsum('jgph,gk->jghkp', b, eye).reshape(SSM_NBLK, SSM_BLK_CH, SSM_BLK_ST).astype(BF16)

    def c_blocks(c):
        c = c.astype(F32).reshape(SSM_NBLK, SSM_BLK_GROUPS, hc, p)
        return jnp.einsum('jghp,gk->jkpgh', c, eye).reshape(SSM_NBLK, SSM_BLK_ST, SSM_BLK_CH).astype(BF16)

    return dict(bre=b_blocks(bb_re), bim=b_blocks(bb_im), cre=c_blocks(c_re), cim=c_blocks(c_im),
                enr=enr, eni=eni, epr=epr, epi=epi,
                lamr=lb_re.reshape(1, g * p), lami=lb_im.reshape(1, g * p))


def _ssm(s_in, tabs, d_skip):
    l, w = s_in.shape
    tc = SSM_CHUNK
    tri = jnp.tril(jnp.ones((tc, tc), F32)).astype(BF16)
    full = lambda shape: pl.BlockSpec(shape, lambda c: (0,) * len(shape))
    rows = tc * SSM_CHUNKS_PER_STEP
    return pl.pallas_call(
        _ssm_kernel,
        out_shape=jax.ShapeDtypeStruct((l, w), BF16),
        grid=(l // rows,),
        in_specs=[pl.BlockSpec((rows, w), lambda c: (c, 0)),
                  full((SSM_NBLK, SSM_BLK_CH, SSM_BLK_ST)), full((SSM_NBLK, SSM_BLK_CH, SSM_BLK_ST)),
                  full((SSM_NBLK, SSM_BLK_ST, SSM_BLK_CH)), full((SSM_NBLK, SSM_BLK_ST, SSM_BLK_CH)),
                  full((tc, SSM_STATES)), full((tc, SSM_STATES)),
                  full((tc, SSM_STATES)), full((tc, SSM_STATES)),
                  full((1, SSM_STATES)), full((1, SSM_STATES)),
                  full((1, w)), full((tc, tc))],
        out_specs=pl.BlockSpec((rows, w), lambda c: (c, 0)),
        scratch_shapes=[pltpu.VMEM((1, SSM_STATES), F32), pltpu.VMEM((1, SSM_STATES), F32)],
        compiler_params=_params(("arbitrary",)),
        name="s5_ssm",
    )(s_in, tabs['bre'], tabs['bim'], tabs['cre'], tabs['cim'], tabs['enr'], tabs['eni'],
      tabs['epr'], tabs['epi'], tabs['lamr'], tabs['lami'], d_skip.reshape(1, w).astype(F32), tri)


def _merge_kernel(attn_ref, y_ref, u_ref, wa_ref, wg1_ref, wg2_ref, wia_ref, wis_ref, o_ref):
    a = jnp.dot(attn_ref[...], wa_ref[...], preferred_element_type=F32)
    y = y_ref[...]
    s = (jnp.dot(y, wg1_ref[...], preferred_element_type=F32)
         * jax.nn.sigmoid(jnp.dot(y, wg2_ref[...], preferred_element_type=F32)))
    u = u_ref[...]
    ga = jax.nn.sigmoid(jnp.dot(u, wia_ref[...], preferred_element_type=F32))
    gs = jax.nn.sigmoid(jnp.dot(u, wis_ref[...], preferred_element_type=F32))
    o_ref[...] = (ga * a + gs * s).astype(o_ref.dtype)


def _merge(attn, y_ssm, u, w_attn_out, w_ssm_glu, w_in):
    l, d = u.shape
    tm, tn = 512, 512
    nd = d // tn
    go = GATE_OFF // tn
    return pl.pallas_call(
        _merge_kernel,
        out_shape=jax.ShapeDtypeStruct((l, d), BF16),
        grid=(nd, l // tm),
        in_specs=[pl.BlockSpec((tm, ATTN_OUT_WIDTH), lambda j, i: (i, 0)),
                  pl.BlockSpec((tm, SSM_WIDTH), lambda j, i: (i, 0)),
                  pl.BlockSpec((tm, d), lambda j, i: (i, 0)),
                  pl.BlockSpec((ATTN_OUT_WIDTH, tn), lambda j, i: (0, j)),
                  pl.BlockSpec((SSM_WIDTH, tn), lambda j, i: (0, j)),
                  pl.BlockSpec((SSM_WIDTH, tn), lambda j, i: (0, j + nd)),
                  pl.BlockSpec((d, tn), lambda j, i: (0, j + go)),
                  pl.BlockSpec((d, tn), lambda j, i: (0, j + go + nd))],
        out_specs=pl.BlockSpec((tm, tn), lambda j, i: (i, j)),
        compiler_params=_params(("arbitrary", "arbitrary")),
        name="branch_merge",
    )(attn, y_ssm, u, w_attn_out, w_ssm_glu, w_ssm_glu, w_in, w_in)


def _layer_norm(h, g, b):
    mu = jnp.mean(h, axis=-1, keepdims=True)
    c = h - mu
    var = jnp.mean(c * c, axis=-1, keepdims=True)
    return c * lax.rsqrt(var + LN_EPS) * g + b


def _pack_rows(v):
    bits = lax.bitcast_convert_type(v.astype(BF16).astype(F32), jnp.uint32)
    half = bits.shape[1] // 2
    return (bits[:, :half] >> 16) | (bits[:, half:] & jnp.uint32(0xFFFF0000))


def _unpack_lo(xp):
    return lax.bitcast_convert_type(xp << 16, F32)


def _unpack_hi(xp):
    return lax.bitcast_convert_type(xp & jnp.uint32(0xFFFF0000), F32)


def _store_tile_rows(ref, packed):
    rows = packed.shape[0]
    for s in range(SUBLANES):
        ref[pl.ds(s, rows, stride=SUBLANES), :] = packed[:, s * LANES:(s + 1) * LANES]


def _load_tile_rows(ref, s):
    return ref[pl.ds(s, ref.shape[0] // SUBLANES, stride=SUBLANES), :]


def _outproj_kernel(mg_ref, x_ref, wo_ref, wr_ref, g1_ref, lg_ref, lb_ref, sc_ref, sh_ref,
                    x1_ref, u2_ref, u2p_ref, sco_ref):
    mix = jnp.dot(mg_ref[...], wo_ref[...], preferred_element_type=F32)
    x1 = _layer_norm(DEEPNORM_ALPHA * x_ref[...] + g1_ref[...] * mix, lg_ref[...], lb_ref[...])
    x1_ref[...] = x1
    u2 = (x1 * (1.0 + sc_ref[...]) + sh_ref[...]).astype(BF16)
    u2_ref[...] = u2
    _store_tile_rows(u2p_ref, _pack_rows(u2))
    logits = lax.dot_general(wr_ref[...], u2, (((1,), (1,)), ((), ())), preferred_element_type=F32)
    sco_ref[...] = jax.nn.sigmoid(logits)


def _outproj(merged, x, w_o, w_router_t, gate1, ln_g, ln_b, scale2, shift2):
    l, d = x.shape
    assert d == 2 * SUBLANES * LANES
    e = w_router_t.shape[0]
    tm = 512
    row = lambda w: pl.BlockSpec((tm, w), lambda i: (i, 0))
    vec = pl.BlockSpec((1, d), lambda i: (0, 0))
    return pl.pallas_call(
        _outproj_kernel,
        out_shape=(jax.ShapeDtypeStruct((l, d), F32), jax.ShapeDtypeStruct((l, d), BF16),
                   jax.ShapeDtypeStruct((l * SUBLANES, LANES), jnp.uint32),
                   jax.ShapeDtypeStruct((e, l), F32)),
        grid=(l // tm,),
        in_specs=[row(d), row(d), pl.BlockSpec((d, d), lambda i: (0, 0)),
                  pl.BlockSpec((e, d), lambda i: (0, 0)), vec, vec, vec, vec, vec],
        out_specs=(row(d), row(d), pl.BlockSpec((tm * SUBLANES, LANES), lambda i: (i, 0)),
                   pl.BlockSpec((e, tm), lambda i: (0, i))),
        compiler_params=_params(("arbitrary",)),
        name="outproj_ln1_router",
    )(merged, x, w_o, w_router_t, gate1, ln_g, ln_b, scale2, shift2)


ROUTE_TILE = 512
EXPERTS_PER_GROUP = N_EXPERTS // N_EXPERT_GROUPS


def _route_kernel(s_ref, b_ref, tri_ref, idx_ref, w_ref, rank_ref, cnt_ref, carry_ref):
    @pl.when(pl.program_id(0) == 0)
    def _():
        carry_ref[...] = jnp.zeros_like(carry_ref)

    ne, tm = s_ref.shape
    neg_inf = -jnp.inf
    s = s_ref[...]
    sel = s + b_ref[...]
    e_iota = lax.broadcasted_iota(jnp.int32, (ne, tm), 0)

    gs_rows = []
    for g in range(N_EXPERT_GROUPS):
        xg = sel[g * EXPERTS_PER_GROUP:(g + 1) * EXPERTS_PER_GROUP]
        m1 = jnp.max(xg, axis=0, keepdims=True)
        n1 = jnp.sum((xg == m1).astype(F32), axis=0, keepdims=True)
        m2 = jnp.max(jnp.where(xg < m1, xg, neg_inf), axis=0, keepdims=True)
        gs_rows.append(m1 + jnp.where(n1 >= 2.0, m1, m2))
    gs = jnp.concatenate(gs_rows, axis=0)

    g_iota = lax.broadcasted_iota(jnp.int32, gs.shape, 0)
    beaten = jnp.zeros(gs.shape, jnp.int32)
    for g2 in range(N_EXPERT_GROUPS):
        row = gs[g2:g2 + 1]
        beats = (row > gs) | ((row == gs) & (g2 < g_iota))
        beaten = beaten + beats.astype(jnp.int32)
    g_ok = beaten < TOPK_GROUPS
    work = jnp.concatenate(
        [jnp.where(g_ok[g:g + 1], sel[g * EXPERTS_PER_GROUP:(g + 1) * EXPERTS_PER_GROUP], neg_inf)
         for g in range(N_EXPERT_GROUPS)], axis=0)

    idxs, vals = [], []
    chosen = jnp.zeros((ne, tm), F32)
    for _ in range(TOP_K):
        m = jnp.max(work, axis=0, keepdims=True)
        i = jnp.min(jnp.where(work == m, e_iota, ne), axis=0, keepdims=True)
        onehot = e_iota == i
        idxs.append(i)
        vals.append(jnp.sum(jnp.where(onehot, s, 0.0), axis=0, keepdims=True))
        chosen = jnp.where(onehot, 1.0, chosen)
        work = jnp.where(onehot, neg_inf, work)
    wsum = vals[0]
    for v in vals[1:]:
        wsum = wsum + v

    before = jnp.dot(chosen.astype(BF16), tri_ref[...], preferred_element_type=F32) + carry_ref[...]
    ranks = [jnp.sum(jnp.where(e_iota == i, before, 0.0), axis=0, keepdims=True) for i in idxs]
    carry_ref[...] = carry_ref[...] + jnp.sum(chosen, axis=1, keepdims=True)

    idx_ref[...] = jnp.concatenate(idxs, axis=0)
    w_ref[...] = jnp.concatenate([v / wsum * ROUTED_SCALE for v in vals], axis=0)
    rank_ref[...] = jnp.concatenate(ranks, axis=0).astype(jnp.int32)
    cnt_ref[...] = carry_ref[...]


def _route(scores_t, router_bias):
    ne, t = scores_t.shape
    tm = ROUTE_TILE
    tri = jnp.triu(jnp.ones((tm, tm), F32), k=1).astype(BF16)
    tok = pl.BlockSpec((TOP_K, tm), lambda i: (0, i))
    return pl.pallas_call(
        _route_kernel,
        out_shape=(jax.ShapeDtypeStruct((TOP_K, t), jnp.int32), jax.ShapeDtypeStruct((TOP_K, t), F32),
                   jax.ShapeDtypeStruct((TOP_K, t), jnp.int32), jax.ShapeDtypeStruct((ne, 1), F32)),
        grid=(t // tm,),
        in_specs=[pl.BlockSpec((ne, tm), lambda i: (0, i)),
                  pl.BlockSpec((ne, 1), lambda i: (0, 0)),
                  pl.BlockSpec((tm, tm), lambda i: (0, 0))],
        out_specs=(tok, tok, tok, pl.BlockSpec((ne, 1), lambda i: (0, 0))),
        scratch_shapes=[pltpu.VMEM((ne, 1), F32)],
        compiler_params=_params(("arbitrary",)),
        name="moe_route",
    )(scores_t, router_bias.astype(F32).reshape(ne, 1), tri)


def _positions_kernel(ps_ref, idx_ref, rank_ref, pos_ref):
    idx = idx_ref[...]

    def add_expert(e, acc):
        return acc + jnp.where(idx == e, ps_ref[e], 0)

    start = lax.fori_loop(0, ps_ref.shape[0], add_expert, jnp.zeros(idx.shape, jnp.int32))
    pos_ref[...] = (start + rank_ref[...]) * SUBLANES


def _positions(idx, rank, pad_start):
    k, t = idx.shape
    tm = 1024
    tok = pl.BlockSpec((k, tm), lambda i: (0, i))
    return pl.pallas_call(
        _positions_kernel,
        out_shape=jax.ShapeDtypeStruct((k, t), jnp.int32),
        grid=(t // tm,),
        in_specs=[pl.BlockSpec(memory_space=pltpu.SMEM), tok, tok],
        out_specs=tok,
        compiler_params=_params(("arbitrary",)),
        name="moe_positions",
    )(pad_start, idx, rank)


DISPATCH_TILE = 512


def _dispatch_kernel(pos_ref, ps_ref, pe_ref, u_ref, xs_ref, zero_ref, sem, zsem, tsem):
    tm = u_ref.shape[0] // SUBLANES
    ne = ps_ref.shape[0]
    block_words = MOE_BLOCK * SUBLANES

    def tile_row(r):
        return pl.ds(pl.multiple_of(r * SUBLANES, SUBLANES), SUBLANES)

    first_unused = pe_ref[ne - 1] // MOE_BLOCK
    n_blocks = xs_ref.shape[0] // block_words

    def unused_copy(b):
        start = pl.multiple_of(b * block_words, block_words)
        return pltpu.make_async_copy(zero_ref, xs_ref.at[pl.ds(start, block_words)], tsem)

    @pl.when(pl.program_id(0) == 0)
    def _():
        zero_ref[...] = jnp.zeros_like(zero_ref)

        def block_copy(start):
            start = pl.multiple_of(start * SUBLANES, block_words)
            return pltpu.make_async_copy(zero_ref, xs_ref.at[pl.ds(start, block_words)], zsem)

        def fill(e, carry):
            @pl.when(pe_ref[e] > ps_ref[e])
            def _():
                block_copy(pe_ref[e] - MOE_BLOCK).start()
            return carry

        def fill_done(e, carry):
            @pl.when(pe_ref[e] > ps_ref[e])
            def _():
                block_copy(pe_ref[e] - MOE_BLOCK).wait()
            return carry

        lax.fori_loop(0, ne, fill, 0)
        lax.fori_loop(0, ne, fill_done, 0)

        def fill_unused(b, carry):
            unused_copy(b).start()
            return carry

        lax.fori_loop(first_unused, n_blocks, fill_unused, 0)

    def issue(pair, carry):
        for t in (2 * pair, 2 * pair + 1):
            for k in range(TOP_K):
                dst = pl.ds(pl.multiple_of(pos_ref[k, t], SUBLANES), SUBLANES)
                pltpu.make_async_copy(u_ref.at[tile_row(t)], xs_ref.at[dst], sem).start(priority=k % 2)
        return carry

    lax.fori_loop(0, tm // 2, issue, 0)
    for _ in range(TOP_K):
        pltpu.make_async_copy(u_ref, xs_ref.at[pl.ds(0, tm * SUBLANES)], sem).wait()

    @pl.when(pl.program_id(0) == pl.num_programs(0) - 1)
    def _():
        def fill_unused_done(b, carry):
            unused_copy(b).wait()
            return carry

        lax.fori_loop(first_unused, n_blocks, fill_unused_done, 0)


def _dispatch(pos, pad_start, pad_end, u2p, n_rows):
    t = u2p.shape[0] // SUBLANES
    tm = DISPATCH_TILE
    smem_tok = pl.BlockSpec((TOP_K, tm), lambda i: (0, i), memory_space=pltpu.SMEM)
    smem_all = pl.BlockSpec(memory_space=pltpu.SMEM)
    return pl.pallas_call(
        _dispatch_kernel,
        out_shape=jax.ShapeDtypeStruct((n_rows * SUBLANES, LANES), jnp.uint32),
        grid=(t // tm,),
        in_specs=[smem_tok, smem_all, smem_all,
                  pl.BlockSpec((tm * SUBLANES, LANES), lambda i: (i, 0))],
        out_specs=pl.BlockSpec(memory_space=pl.ANY),
        scratch_shapes=[pltpu.VMEM((MOE_BLOCK * SUBLANES, LANES), jnp.uint32),
                        pltpu.SemaphoreType.DMA, pltpu.SemaphoreType.DMA, pltpu.SemaphoreType.DMA],
        compiler_params=_params(("arbitrary",)),
        name="moe_dispatch",
    )(pos, pad_start, pad_end, u2p)


WEIGHT_CHUNK_BYTES = 2 * 1024 * 1024


def _expert_kernel(be_ref, bv_ref, nx_ref, nx2_ref, sl_ref, x_ref, win_ref, wout_ref, y_ref,
                   wfi_ref, wfo_ref, wbi_ref, wbo_ref, sem):
    b = pl.program_id(0)
    valid = bv_ref[b] > 0
    e = be_ref[b]
    slot = sl_ref[b]
    new_expert = (b == 0) | (e != be_ref[jnp.maximum(b - 1, 0)])

    def weight_chunks():
        chunks = []
        for hbm, land, wb, j in ((win_ref, wfi_ref, wbi_ref, 0), (wout_ref, wfo_ref, wbo_ref, 1)):
            rows = WEIGHT_CHUNK_BYTES // (hbm.shape[2] * 4)
            for c in range(hbm.shape[1] // rows):
                chunks.append((hbm, land, wb, j, pl.ds(c * rows, rows)))
        return chunks

    def chunk_copy(chunk, ex, s):
        hbm, land, _, j, rs = chunk
        return pltpu.make_async_copy(hbm.at[ex, rs], land.at[s, rs], sem.at[s, j])

    def start_weights(ex, s):
        for n, chunk in enumerate(weight_chunks()):
            chunk_copy(chunk, ex, s).start(priority=n % 2)

    @pl.when(b == 0)
    def _():
        start_weights(e, slot)

        @pl.when(nx_ref[b] >= 0)
        def _():
            start_weights(nx_ref[b], 1 - slot)

    @pl.when(valid & new_expert)
    def _():
        for chunk in weight_chunks():
            chunk_copy(chunk, e, slot).wait()
        for n, chunk in enumerate(weight_chunks()):
            _, land, wb, _, rs = chunk
            for s in range(2):
                @pl.when(slot == s)
                def _():
                    wb[rs] = land[s, rs].astype(BF16)

            @pl.when(nx2_ref[b] >= 0)
            def _():
                chunk_copy(chunk, nx2_ref[b], slot).start(priority=n % 2)

    @pl.when(valid)
    def _():
        words = [_load_tile_rows(x_ref, s) for s in range(SUBLANES)]
        lo = jnp.concatenate([_unpack_lo(wd).astype(BF16) for wd in words], axis=1)
        hi = jnp.concatenate([_unpack_hi(wd).astype(BF16) for wd in words], axis=1)
        half = lo.shape[1]
        h = (jnp.dot(lo, wbi_ref[:half], preferred_element_type=F32)
             + jnp.dot(hi, wbi_ref[half:], preferred_element_type=F32))
        hg, hu = h[:, :EXPERT_FF], h[:, EXPERT_FF:]
        act = (hg * jax.nn.sigmoid(hg) * hu).astype(BF16)
        _store_tile_rows(y_ref, _pack_rows(jnp.dot(act, wbo_ref[...], preferred_element_type=F32)))

    @pl.when(jnp.logical_not(valid))
    def _():
        y_ref[...] = jnp.zeros_like(y_ref)


def _experts(block_expert, block_valid, block_next, block_next2, block_slot, x_rows, e_w_in, e_w_out):
    n_rows = x_rows.shape[0] // SUBLANES
    nb = n_rows // MOE_BLOCK
    _, d, ff2 = e_w_in.shape
    ff = ff2 // 2
    rows_spec = pl.BlockSpec((MOE_BLOCK * SUBLANES, LANES), lambda b, *_: (b, 0))
    grid_spec = pltpu.PrefetchScalarGridSpec(
        num_scalar_prefetch=5,
        grid=(nb,),
        in_specs=[rows_spec,
                  pl.BlockSpec(memory_space=pl.ANY),
                  pl.BlockSpec(memory_space=pl.ANY)],
        out_specs=rows_spec,
        scratch_shapes=[pltpu.VMEM((2, d, ff2), F32), pltpu.VMEM((2, ff, d), F32),
                        pltpu.VMEM((d, ff2), BF16), pltpu.VMEM((ff, d), BF16),
                        pltpu.SemaphoreType.DMA((2, 2))],
    )
    return pl.pallas_call(
        _expert_kernel,
        out_shape=jax.ShapeDtypeStruct((n_rows * SUBLANES, LANES), jnp.uint32),
        grid_spec=grid_spec,
        compiler_params=_params(("arbitrary",)),
        name="routed_experts",
    )(block_expert, block_valid, block_next, block_next2, block_slot, x_rows, e_w_in, e_w_out)


COMBINE_TILE = 256


def _final_kernel(pos_ref, posn_ref, u2_ref, x1_ref, w_ref, ys_ref,
                  win_ref, wout_ref, g2_ref, lg_ref, lb_ref, o_ref, buf0_ref, buf1_ref, sem):
    tm = u2_ref.shape[0]
    i = pl.program_id(0)
    bufs = (buf0_ref, buf1_ref)

    def tile_row(r):
        return pl.ds(pl.multiple_of(r * SUBLANES, SUBLANES), SUBLANES)

    def gather(slots_ref, par):
        def issue(pair, carry):
            for t in (2 * pair, 2 * pair + 1):
                for k in range(TOP_K):
                    src = pl.ds(pl.multiple_of(slots_ref[k, t], SUBLANES), SUBLANES)
                    pltpu.make_async_copy(ys_ref.at[src], bufs[par].at[k, tile_row(t)],
                                          sem.at[par]).start(priority=k % 2)
            return carry

        lax.fori_loop(0, tm // 2, issue, 0)

    @pl.when(i == 0)
    def _():
        gather(pos_ref, 0)

    for par in range(2):
        @pl.when((i + 1 < pl.num_programs(0)) & (i % 2 == par))
        def _():
            gather(posn_ref, 1 - par)

    h = jnp.dot(u2_ref[...], win_ref[...], preferred_element_type=F32)
    hg, hu = h[:, :SHARED_FF], h[:, SHARED_FF:]
    act = (hg * jax.nn.sigmoid(hg) * hu).astype(BF16)
    shared = jnp.dot(act, wout_ref[...], preferred_element_type=F32)

    w = w_ref[...]
    wk = [w[:, k:k + 1] for k in range(TOP_K)]

    for par in range(2):
        @pl.when(i % 2 == par)
        def _():
            buf = bufs[par]
            for k in range(TOP_K):
                pltpu.make_async_copy(ys_ref.at[pl.ds(0, tm * SUBLANES)], buf.at[k], sem.at[par]).wait()
            lo_parts, hi_parts = [], []
            for s in range(SUBLANES):
                lo = jnp.zeros((tm, LANES), F32)
                hi = jnp.zeros((tm, LANES), F32)
                for k in range(TOP_K):
                    words = _load_tile_rows(buf.at[k], s)
                    lo = lo + wk[k] * _unpack_lo(words)
                    hi = hi + wk[k] * _unpack_hi(words)
                lo_parts.append(lo)
                hi_parts.append(hi)
            ffn = shared + jnp.concatenate(lo_parts + hi_parts, axis=1)
            o_ref[...] = _layer_norm(DEEPNORM_ALPHA * x1_ref[...] + g2_ref[...] * ffn,
                                     lg_ref[...], lb_ref[...])


def _final(pos, u2, x1, w_tok, y_rows, s_w_in, s_w_out, gate2, ln_g, ln_b):
    l, d = x1.shape
    tm = COMBINE_TILE
    row = pl.BlockSpec((tm, d), lambda i: (i, 0))
    vec = pl.BlockSpec((1, d), lambda i: (0, 0))
    n_tiles = l // tm
    smem_tok = pl.BlockSpec((TOP_K, tm), lambda i: (0, i), memory_space=pltpu.SMEM)
    smem_next = pl.BlockSpec((TOP_K, tm), lambda i: (0, jnp.minimum(i + 1, n_tiles - 1)),
                             memory_space=pltpu.SMEM)
    row_buf = pltpu.VMEM((TOP_K, tm * SUBLANES, LANES), jnp.uint32)
    return pl.pallas_call(
        _final_kernel,
        out_shape=jax.ShapeDtypeStruct((l, d), F32),
        grid=(n_tiles,),
        in_specs=[smem_tok, smem_next, row, row, pl.BlockSpec((tm, TOP_K), lambda i: (i, 0)),
                  pl.BlockSpec(memory_space=pl.ANY),
                  pl.BlockSpec(s_w_in.shape, lambda i: (0, 0)),
                  pl.BlockSpec(s_w_out.shape, lambda i: (0, 0)), vec, vec, vec],
        out_specs=row,
        scratch_shapes=[row_buf, row_buf, pltpu.SemaphoreType.DMA((2,))],
        compiler_params=_params(("arbitrary",)),
        name="combine_shared_ln2",
    )(pos, pos, u2, x1, w_tok, y_rows, s_w_in, s_w_out, gate2, ln_g, ln_b)


def _block_layout(counts, n_tokens):
    padded = (counts + MOE_BLOCK - 1) // MOE_BLOCK * MOE_BLOCK
    pad_ends = jnp.cumsum(padded)
    pad_starts = (pad_ends - padded).astype(jnp.int32)
    n_rows = -(-(n_tokens * TOP_K + N_EXPERTS * (MOE_BLOCK - 1)) // MOE_BLOCK) * MOE_BLOCK
    block_start = jnp.arange(n_rows // MOE_BLOCK, dtype=jnp.int32) * MOE_BLOCK
    block_expert = jnp.minimum(jnp.sum((block_start[:, None] >= pad_ends[None, :]).astype(jnp.int32), axis=1),
                               N_EXPERTS - 1).astype(jnp.int32)
    block_valid = (block_start < pad_ends[-1]).astype(jnp.int32)
    ar = jnp.arange(N_EXPERTS, dtype=jnp.int32)
    has = counts > 0
    later = (ar[None, :] > ar[:, None]) & has[None, :]
    next_has = jnp.min(jnp.where(later, ar[None, :], N_EXPERTS), axis=1)
    next_has = jnp.where(next_has >= N_EXPERTS, -1, next_has)
    after = (ar[None, :] == next_has[:, None])
    next2_has = jnp.sum(jnp.where(after, next_has[None, :], 0), axis=1)
    next2_has = jnp.where(next_has < 0, -1, next2_has)
    ordinal = jnp.cumsum(has.astype(jnp.int32)) - 1
    mine = block_expert[:, None] == ar[None, :]
    pick = lambda v: jnp.sum(jnp.where(mine, v[None, :], 0), axis=1).astype(jnp.int32)
    return (pad_starts, pad_ends.astype(jnp.int32), block_expert, block_valid, pick(next_has),
            pick(next2_has), pick(ordinal) % 2, n_rows)


def kernel(x, c, w_ada, b_ada, w_in, rel_bias, ssm_a_re, ssm_a_im, ssm_log_dt, ssm_b_re, ssm_b_im, ssm_c_re, ssm_c_im, ssm_d, w_attn_out, w_ssm_glu, w_o, ln1_g, ln1_b, w_router, router_bias, e_w_in, e_w_out, s_w_in, s_w_out, ln2_g, ln2_b):
    bsz, l, d = x.shape
    assert bsz == 1
    xf = x.reshape(l, d)
    i = 0
    mod = _modulation(c, w_ada[i], b_ada[i])
    shift1, scale1, gate1, shift2, scale2, gate2 = [mod[:, k * d:(k + 1) * d] for k in range(6)]

    w_in_b = w_in[i].astype(BF16)
    u, s_in = _modulate_inproj_ssm(xf, scale1, shift1, w_in_b)

    outs, lses = [], []
    for gi, (window, dilation) in enumerate(DILATED_GROUPS):
        bias_prev, bias_cur = _attn_bias_tables(rel_bias, gi, window, dilation)
        qkv = _inproj_qkv_group(u, w_in_b, gi, dilation)
        o, s = _attention_group(qkv, bias_prev, bias_cur, gi)
        outs.append(o)
        lses.append(s)
    attn = _attn_combine(outs, lses)

    tabs = _ssm_tables(ssm_a_re[i], ssm_a_im[i], ssm_log_dt[i], ssm_b_re[i], ssm_b_im[i],
                       ssm_c_re[i], ssm_c_im[i])
    y_ssm = _ssm(s_in, tabs, ssm_d[i])

    merged = _merge(attn, y_ssm, u, w_attn_out[i].astype(BF16), w_ssm_glu[i].astype(BF16), w_in_b)
    x1, u2, u2p, scores_t = _outproj(merged, xf, w_o[i].astype(BF16), w_router[i].T.astype(BF16), gate1,
                                     ln1_g[i].reshape(1, d), ln1_b[i].reshape(1, d), scale2, shift2)

    idx, w, rank, counts = _route(scores_t, router_bias[i])
    (pad_start, pad_end, block_expert, block_valid, block_next, block_next2, block_slot,
     n_rows) = _block_layout(counts[:, 0].astype(jnp.int32), l)
    pos = _positions(idx, rank, pad_start)
    x_rows = _dispatch(pos, pad_start, pad_end, u2p, n_rows)
    y_rows = _experts(block_expert, block_valid, block_next, block_next2, block_slot, x_rows,
                      e_w_in[i], e_w_out[i])
    out = _final(pos, u2, x1, w.T, y_rows, s_w_in[i].astype(BF16),
                 s_w_out[i].astype(BF16), gate2, ln2_g[i].reshape(1, d), ln2_b[i].reshape(1, d))
    return out.reshape(bsz, l, d)
```
